```python
import math
import jax
import jax.numpy as jnp
from jax import lax
import numpy as np

D_MODEL = 1024
BATCH = 8
SEQ = 2048
DEPTH = 2

HEAD_DIM = 64
HGRN_WIDTH = D_MODEL // 4
HGRN_HEADS = HGRN_WIDTH // HEAD_DIM
HGRN_KEY_DIM = HEAD_DIM
HGRN_VAL_DIM = HEAD_DIM
HGRN_KEY_WIDTH = HGRN_HEADS * HGRN_KEY_DIM
HGRN_CHUNK = 64
ATTN_WIDTH = 3 * D_MODEL // 8
ATTN_HEADS = ATTN_WIDTH // HEAD_DIM
DILATED_PAIRS = ((128, 1), (512, 4), (2048, 16))
ATTN_BLOCK = 128
NUM_BUCKETS = 32
MAX_DISTANCE = 2048
SSM_WIDTH = D_MODEL - HGRN_WIDTH - ATTN_WIDTH
SSM_HEADS = SSM_WIDTH // HEAD_DIM
SSM_GROUPS = 2
SSM_STATE = 128
SSM_CONV = 4
SSM_CHUNK = 64
SSM_CONV_CH = SSM_WIDTH + 2 * SSM_GROUPS * SSM_STATE
MIX_WIDTH = HGRN_WIDTH + ATTN_WIDTH + SSM_WIDTH
IN_SIZES = (HGRN_KEY_WIDTH, HGRN_KEY_WIDTH, HGRN_WIDTH, HGRN_WIDTH, ATTN_WIDTH, ATTN_WIDTH, ATTN_WIDTH, SSM_WIDTH, SSM_CONV_CH, SSM_HEADS)
IN_COLS = 2 * HGRN_KEY_WIDTH + 2 * HGRN_WIDTH + 3 * ATTN_WIDTH + SSM_WIDTH + SSM_CONV_CH + SSM_HEADS
PEER_HEADS = 8
PEER_NKEYS = 128
PEER_EXPERTS = PEER_NKEYS * PEER_NKEYS
PEER_DKEY = 128
PEER_HALF = PEER_DKEY // 2
PEER_TOPK = 16
PEER_TOKEN_BLOCK = 128
NORM_EPS = 1e-6

kernel_name = 'hymba_style_hgrn2_dilated_ssd_peer'


def rms_norm(x, w):
    xf = x.astype(jnp.float32)
    xf = xf * lax.rsqrt(jnp.mean(xf * xf, axis=-1, keepdims=True) + NORM_EPS)
    return (xf * w.astype(jnp.float32)).astype(x.dtype)


def t5_bucket(dist):
    max_exact = NUM_BUCKETS // 2
    d = np.maximum(dist, 1).astype(np.float32)
    large = max_exact + (np.log(d / max_exact) / np.log(MAX_DISTANCE / max_exact) * (NUM_BUCKETS - max_exact)).astype(np.int32)
    large = np.minimum(large, NUM_BUCKETS - 1)
    return np.where(dist < max_exact, dist, large).astype(np.int32)


def hgrn2_mixer(q, f, i, g, lower_bound, norm_w):
    bsz, seq, _ = q.shape
    H, K, V, C = HGRN_HEADS, HGRN_KEY_DIM, HGRN_VAL_DIM, HGRN_CHUNK
    lb = lower_bound.astype(jnp.float32)
    forget = lb + (1.0 - lb) * jax.nn.sigmoid(f.astype(jnp.float32))
    log_f = jnp.log(forget)
    key = 1.0 - forget
    qs = q.astype(jnp.float32) * (K ** -0.5)

    def chunks(t, d):
        return t.reshape(bsz, seq // C, C, H, d).swapaxes(0, 1)

    causal = jnp.asarray(np.tril(np.ones((C, C), dtype=bool)))[None, :, :, None, None]

    def step(state, inp):
        qc, kc, vc, lc = inp
        b = jnp.cumsum(lc, axis=1)
        o = jnp.einsum('bthk,bhkv->bthv', qc * jnp.exp(b), state)
        decay = jnp.exp(jnp.where(causal, b[:, :, None] - b[:, None, :], -jnp.inf))
        a = jnp.einsum('bthk,bshk,btshk->btsh', qc, kc, decay)
        o = o + jnp.einsum('btsh,bshv->bthv', a, vc)
        b_end = b[:, -1]
        state = state * jnp.exp(b_end)[..., None] + jnp.einsum('bshk,bshv->bhkv', kc * jnp.exp(b_end[:, None] - b), vc)
        return state, o

    state0 = jnp.zeros((bsz, H, K, V), jnp.float32)
    _, o = lax.scan(step, state0, (chunks(qs, K), chunks(key, K), chunks(i.astype(jnp.float32), V), chunks(log_f, K)))
    o = o.swapaxes(0, 1).reshape(bsz, seq, H, V)
    o = rms_norm(o, norm_w) * jax.nn.silu(g.astype(jnp.float32).reshape(bsz, seq, H, V))
    return o.reshape(bsz, seq, H * V)


def dilated_branch(q, k, v, rel_bias, window, dil):
    bsz, seq, H, Dh = q.shape
    blk = ATTN_BLOCK
    n_back = window // dil
    L = seq // dil
    nb = -(-L // blk)
    Lp = nb * blk

    def residue(t):
        return t.reshape(bsz, L, dil, H, Dh).transpose(0, 2, 3, 1, 4)

    qb = jnp.pad(residue(q), ((0, 0), (0, 0), (0, 0), (0, Lp - L), (0, 0))).reshape(bsz, dil, H, nb, blk, Dh)

    def key_windows(t):
        tp = jnp.pad(residue(t), ((0, 0), (0, 0), (0, 0), (blk, Lp - L), (0, 0)))
        tb = tp.reshape(bsz, dil, H, nb + 1, blk, Dh)
        return jnp.concatenate([tb[:, :, :, :-1], tb[:, :, :, 1:]], axis=4)

    kw, vw = key_windows(k), key_windows(v)
    qi = np.arange(blk)[:, None]
    kj = np.arange(2 * blk)[None, :]
    delta = qi - kj + blk
    band = (delta >= 0) & (delta <= n_back)
    key_pos = (np.arange(nb)[:, None, None] - 1) * blk + kj[None]
    mask = jnp.asarray(band[None] & (key_pos >= 0))
    bucket = t5_bucket(np.maximum(delta, 0) * dil)
    bias = rel_bias.astype(jnp.float32)[bucket].transpose(2, 0, 1)
    s = jnp.einsum('bghnid,bghnjd->bghnij', qb, kw).astype(jnp.float32) + bias[None, None, :, None]
    s = jnp.where(mask[None, None, None], s, -jnp.inf)
    lse = jax.nn.logsumexp(s, axis=-1)
    p = jnp.exp(s - lse[..., None])
    o = jnp.einsum('bghnij,bghnjd->bghnid', p, vw.astype(jnp.float32))
    o = o.reshape(bsz, dil, H, Lp, Dh)[:, :, :, :L].transpose(0, 3, 1, 2, 4).reshape(bsz, seq, H, Dh)
    lse = lse.reshape(bsz, dil, H, Lp)[..., :L].transpose(0, 3, 1, 2).reshape(bsz, seq, H)
    return o, lse


def dilated_attention_mixer(q, k, v, q_norm_w, k_norm_w, rel_bias):
    bsz, seq, _ = q.shape
    H, Dh = ATTN_HEADS, HEAD_DIM
    q = rms_norm(q.reshape(bsz, seq, H, Dh), q_norm_w) * (Dh ** -0.5)
    k = rms_norm(k.reshape(bsz, seq, H, Dh), k_norm_w)
    v = v.reshape(bsz, seq, H, Dh)
    outs, lses = [], []
    for window, dil in DILATED_PAIRS:
        o, lse = dilated_branch(q, k, v, rel_bias, window, dil)
        outs.append(o)
        lses.append(lse)
    w = jax.nn.softmax(jnp.stack(lses, axis=0), axis=0)
    o = jnp.sum(w[..., None] * jnp.stack(outs, axis=0), axis=0)
    return o.reshape(bsz, seq, H * Dh)


def mamba2_mixer(z, xbc, dt, conv_w, conv_b, dt_bias, a_log, d_skip, norm_w):
    bsz, seq, _ = z.shape
    H, P, G, N, Q = SSM_HEADS, HEAD_DIM, SSM_GROUPS, SSM_STATE, SSM_CHUNK
    R = H // G
    xbc = lax.conv_general_dilated(xbc, conv_w[:, None, :], window_strides=(1,), padding=((SSM_CONV - 1, 0),),
                                   dimension_numbers=('NWC', 'WIO', 'NWC'), feature_group_count=SSM_CONV_CH) + conv_b
    xbc = jax.nn.silu(xbc.astype(jnp.float32))
    xs, bm, cm = jnp.split(xbc, [SSM_WIDTH, SSM_WIDTH + G * N], axis=-1)
    xs = xs.reshape(bsz, seq, H, P)
    bm = bm.reshape(bsz, seq, G, N)
    cm = cm.reshape(bsz, seq, G, N)
    dt = jax.nn.softplus(dt.astype(jnp.float32) + dt_bias.astype(jnp.float32))
    A = -jnp.exp(a_log.astype(jnp.float32))

    def chunks(t):
        return t.reshape((bsz, seq // Q, Q) + t.shape[2:]).swapaxes(0, 1)

    causal = jnp.asarray(np.tril(np.ones((Q, Q), dtype=bool)))[None, :, :, None]

    def step(state, inp):
        xc, dtc, bc, cc = inp
        bh = jnp.repeat(bc, R, axis=2)
        ch = jnp.repeat(cc, R, axis=2)
        cum = jnp.cumsum(dtc * A, axis=1)
        decay = jnp.exp(jnp.where(causal, cum[:, :, None, :] - cum[:, None, :, :], -jnp.inf))
        scores = jnp.einsum('bthn,bshn->btsh', ch, bh) * decay
        y = jnp.einsum('btsh,bshp->bthp', scores, xc * dtc[..., None])
        y = y + jnp.einsum('bthn,bhpn->bthp', ch, state) * jnp.exp(cum)[..., None]
        to_end = jnp.exp(cum[:, -1:] - cum) * dtc
        state = state * jnp.exp(cum[:, -1])[:, :, None, None] + jnp.einsum('bsh,bshn,bshp->bhpn', to_end, bh, xc)
        return state, y

    state0 = jnp.zeros((bsz, H, P, N), jnp.float32)
    _, y = lax.scan(step, state0, (chunks(xs), chunks(dt), chunks(bm), chunks(cm)))
    y = y.swapaxes(0, 1).reshape(bsz, seq, H, P)
    y = y + d_skip.astype(jnp.float32)[:, None] * xs
    y = y.reshape(bsz, seq, SSM_WIDTH) * jax.nn.silu(z.astype(jnp.float32))
    y = rms_norm(y.reshape(bsz, seq, G, SSM_WIDTH // G), norm_w.reshape(G, SSM_WIDTH // G))
    return y.reshape(bsz, seq, SSM_WIDTH)


def peer_ffn(x, w_query, sub_keys, down, up):
    bsz, seq, D = x.shape
    T = bsz * seq
    H, K, TB = PEER_HEADS, PEER_TOPK, PEER_TOKEN_BLOCK
    xt = x.reshape(T, D)
    q = (xt @ w_query).reshape(T, H, 2, PEER_HALF)
    s = jnp.einsum('thcd,cnd->thcn', q, sub_keys).astype(jnp.float32)
    sv, si = lax.top_k(s, K)
    cand_s = (sv[:, :, 0, :, None] + sv[:, :, 1, None, :]).reshape(T, H, K * K)
    cand_i = (si[:, :, 0, :, None] * PEER_NKEYS + si[:, :, 1, None, :]).reshape(T, H, K * K)
    top_s, pos = lax.top_k(cand_s, K)
    idx = jnp.take_along_axis(cand_i, pos, axis=-1)
    gate = jax.nn.softmax(top_s, axis=-1)

    def expert_block(args):
        xb, ib, gb = args
        a = jnp.einsum('thkd,td->thk', down[ib], xb).astype(jnp.float32)
        c = (gb * jax.nn.gelu(a)).astype(up.dtype)
        return jnp.einsum('thk,thkd->td', c, up[ib]).astype(x.dtype)

    nblk = T // TB
    out = lax.map(expert_block, (xt.reshape(nblk, TB, D), idx.reshape(nblk, TB, H, K), gate.reshape(nblk, TB, H, K)))
    return out.reshape(bsz, seq, D)


def hybrid_layer(x, attn_norm_w, w_in, lower_bound, hgrn_norm_w, q_norm_w, k_norm_w, rel_bias,
                 ssm_conv_w, ssm_conv_b, ssm_dt_bias, ssm_a_log, ssm_d, ssm_norm_w, w_out,
                 ffn_norm_w, peer_w_query, peer_sub_keys, peer_down, peer_up):
    u = rms_norm(x, attn_norm_w) @ w_in
    points = [int(p) for p in np.cumsum(IN_SIZES)[:-1]]
    hq, hf, hi, hg, aq, ak, av, sz, sxbc, sdt = jnp.split(u, points, axis=-1)
    mix = jnp.concatenate([
        hgrn2_mixer(hq, hf, hi, hg, lower_bound, hgrn_norm_w).astype(x.dtype),
        dilated_attention_mixer(aq, ak, av, q_norm_w, k_norm_w, rel_bias).astype(x.dtype),
        mamba2_mixer(sz, sxbc, sdt, ssm_conv_w, ssm_conv_b, ssm_dt_bias, ssm_a_log, ssm_d, ssm_norm_w).astype(x.dtype),
    ], axis=-1)
    x = x + (mix @ w_out).astype(x.dtype)
    x = x + peer_ffn(rms_norm(x, ffn_norm_w), peer_w_query, peer_sub_keys, peer_down, peer_up)
    return x


def setup_inputs(seed: int = 0) -> dict:
    key = jax.random.key(seed)
    ks = jax.random.split(key, 20)

    def nrm(k, shape, scale):
        return jax.random.normal(k, shape, jnp.float32) * scale

    dt0 = jnp.exp(jax.random.uniform(ks[10], (DEPTH, SSM_HEADS), jnp.float32, math.log(1e-3), math.log(1e-1)))
    return {
        'x': nrm(ks[0], (BATCH, SEQ, D_MODEL), 1.0),
        'attn_norm_w': 1.0 + nrm(ks[1], (DEPTH, D_MODEL), 0.02),
        'w_in': nrm(ks[2], (DEPTH, D_MODEL, IN_COLS), D_MODEL ** -0.5),
        'hgrn_lower_bounds': nrm(ks[3], (DEPTH, HGRN_KEY_WIDTH), 0.5),
        'hgrn_norm_w': 1.0 + nrm(ks[4], (DEPTH, HGRN_VAL_DIM), 0.02),
        'q_norm_w': 1.0 + nrm(ks[5], (DEPTH, HEAD_DIM), 0.02),
        'k_norm_w': 1.0 + nrm(ks[6], (DEPTH, HEAD_DIM), 0.02),
        'rel_bias': nrm(ks[7], (NUM_BUCKETS, ATTN_HEADS), 0.1),
        'ssm_conv_w': nrm(ks[8], (DEPTH, SSM_CONV, SSM_CONV_CH), SSM_CONV ** -0.5),
        'ssm_conv_b': nrm(ks[9], (DEPTH, SSM_CONV_CH), 0.02),
        'ssm_dt_bias': dt0 + jnp.log(-jnp.expm1(-dt0)),
        'ssm_a_log': jnp.log(jax.random.uniform(ks[11], (DEPTH, SSM_HEADS), jnp.float32, 1.0, 16.0)),
        'ssm_d': 1.0 + nrm(ks[12], (DEPTH, SSM_HEADS), 0.02),
        'ssm_norm_w': 1.0 + nrm(ks[13], (DEPTH, SSM_WIDTH), 0.02),
        'w_out': nrm(ks[14], (DEPTH, MIX_WIDTH, D_MODEL), MIX_WIDTH ** -0.5),
        'ffn_norm_w': 1.0 + nrm(ks[15], (DEPTH, D_MODEL), 0.02),
        'peer_w_query': nrm(ks[16], (DEPTH, D_MODEL, PEER_HEADS * PEER_DKEY), D_MODEL ** -0.5),
        'peer_sub_keys': nrm(ks[17], (DEPTH, 2, PEER_NKEYS, PEER_HALF), PEER_HALF ** -0.5),
        'peer_down': nrm(ks[18], (DEPTH, PEER_EXPERTS, D_MODEL), D_MODEL ** -0.5),
        'peer_up': nrm(ks[19], (DEPTH, PEER_EXPERTS, D_MODEL), D_MODEL ** -0.5),
    }


def reference(x, attn_norm_w, w_in, hgrn_lower_bounds, hgrn_norm_w, q_norm_w, k_norm_w, rel_bias,
              ssm_conv_w, ssm_conv_b, ssm_dt_bias, ssm_a_log, ssm_d, ssm_norm_w, w_out,
              ffn_norm_w, peer_w_query, peer_sub_keys, peer_down, peer_up):
    lb = jnp.cumsum(jax.nn.softmax(hgrn_lower_bounds.astype(jnp.float32), axis=0), axis=0)
    lb = lb - lb[0]
    for l in range(DEPTH):
        x = hybrid_layer(x, attn_norm_w[l], w_in[l], lb[l], hgrn_norm_w[l], q_norm_w[l], k_norm_w[l], rel_bias,
                         ssm_conv_w[l], ssm_conv_b[l], ssm_dt_bias[l], ssm_a_log[l], ssm_d[l], ssm_norm_w[l], w_out[l],
                         ffn_norm_w[l], peer_w_query[l], peer_sub_keys[l], peer_down[l], peer_up[l])
    return x
```

```python
import functools
import math

import numpy as np
import jax
import jax.numpy as jnp
from jax import lax
from jax.experimental import pallas as pl
from jax.experimental.pallas import tpu as pltpu

F32 = jnp.float32
BF16 = jnp.bfloat16
HIGHEST = lax.Precision.HIGHEST
NEG = -1e30
EPS = 1e-6

HEAD_DIM = 64
CHUNK = 64
HGRN_HEADS = 4
HGRN_W = HGRN_HEADS * HEAD_DIM
ATTN_HEADS = 6
ATTN_W = ATTN_HEADS * HEAD_DIM
ATTN_BLOCK = 128
DILATED_PAIRS = ((128, 1), (512, 4), (2048, 16))
NUM_BUCKETS = 32
MAX_DISTANCE = 2048
SSM_HEADS = 6
SSM_W = SSM_HEADS * HEAD_DIM
SSM_GROUPS = 2
SSM_STATE = 128
SSM_GN = SSM_GROUPS * SSM_STATE
SSM_CONV = 4
SSM_CONV_CH = SSM_W + 2 * SSM_GN
PEER_HEADS = 8
PEER_NKEYS = 128
PEER_HALF = 64
PEER_TOPK = 16
HG_COLS = 4 * HGRN_W
AT_COLS = 3 * ATTN_W
SS_COLS = SSM_W + SSM_CONV_CH + SSM_W

VMEM_LIMIT = 56 * 1024 * 1024


def _cparams(sem):
    return pltpu.CompilerParams(dimension_semantics=sem, vmem_limit_bytes=VMEM_LIMIT)


def _iota(shape, dim):
    return lax.broadcasted_iota(jnp.int32, shape, dim)


def _mm(a, b):
    return jnp.dot(a.astype(BF16), b.astype(BF16), preferred_element_type=F32)


def _mm_nt(a, b):
    return lax.dot_general(a.astype(BF16), b.astype(BF16), (((1,), (1,)), ((), ())),
                           preferred_element_type=F32)


def _mm_tn(a, b):
    return lax.dot_general(a.astype(BF16), b.astype(BF16), (((0,), (0,)), ((), ())),
                           preferred_element_type=F32)


def _mm_hi(a, b):
    return jnp.dot(a, b, precision=HIGHEST, preferred_element_type=F32)


def _sigmoid(x):
    return 1.0 / (1.0 + jnp.exp(-x))


def _silu(x):
    return x * _sigmoid(x)


def _softplus(x):
    return jnp.maximum(x, 0.0) + jnp.log(1.0 + jnp.exp(-jnp.abs(x)))


def _seg_ones(n, seg):
    r = _iota((n, n), 0)
    c = _iota((n, n), 1)
    same = None
    for s in range(n // seg):
        lo, hi = s * seg, (s + 1) * seg
        t = (r >= lo) & (r < hi) & (c >= lo) & (c < hi)
        same = t if same is None else (same | t)
    return jnp.where(same, 1.0, 0.0).astype(F32)


def _lane_seg_mask(width, seg, idx):
    l = _iota((1, width), 1)
    return jnp.where((l >= idx * seg) & (l < (idx + 1) * seg), 1.0, 0.0).astype(F32)


def _inproj_kernel(x_ref, nw_ref, w_ref, oh_ref, oa_ref, os_ref):
    x = x_ref[...]
    xn = x * lax.rsqrt(jnp.mean(x * x, axis=-1, keepdims=True) + EPS) * nw_ref[...]
    xb = xn.astype(BF16)
    oh_ref[...] = jnp.dot(xb, w_ref[:, 0:HG_COLS], preferred_element_type=F32)
    oa_ref[...] = jnp.dot(xb, w_ref[:, HG_COLS:HG_COLS + AT_COLS], preferred_element_type=F32)
    os_ref[...] = jnp.dot(xb, w_ref[:, HG_COLS + AT_COLS:], preferred_element_type=F32)


def _inproj(x2, norm_w, w_cat, tm=256):
    T, D = x2.shape
    ncol = w_cat.shape[1]
    return pl.pallas_call(
        _inproj_kernel,
        grid=(T // tm,),
        in_specs=[pl.BlockSpec((tm, D), lambda i: (i, 0)),
                  pl.BlockSpec((1, D), lambda i: (0, 0)),
                  pl.BlockSpec((D, ncol), lambda i: (0, 0))],
        out_specs=[pl.BlockSpec((tm, HG_COLS), lambda i: (i, 0)),
                   pl.BlockSpec((tm, AT_COLS), lambda i: (i, 0)),
                   pl.BlockSpec((tm, SS_COLS), lambda i: (i, 0))],
        out_shape=[jax.ShapeDtypeStruct((T, HG_COLS), F32),
                   jax.ShapeDtypeStruct((T, AT_COLS), F32),
                   jax.ShapeDtypeStruct((T, SS_COLS), F32)],
        compiler_params=_cparams(("parallel",)),
        name="inproj",
    )(x2, norm_w.reshape(1, D), w_cat)


def _hgrn_kernel(layer, nchunk, u_ref, lbs_ref, nw_ref, o_ref, st_ref):
    W = HGRN_W

    @pl.when(pl.program_id(1) == 0)
    def _():
        st_ref[...] = jnp.zeros_like(st_ref)

    lbs = lbs_ref[...]
    e = jnp.exp(lbs - jnp.max(lbs, axis=0, keepdims=True))
    sm = e / jnp.sum(e, axis=0, keepdims=True)
    lb = jnp.zeros((1, W), F32)
    for j in range(1, layer + 1):
        lb = lb + sm[j:j + 1]
    nw = nw_ref[...]

    tril = jnp.where(_iota((CHUNK, CHUNK), 0) >= _iota((CHUNK, CHUNK), 1), 1.0, 0.0).astype(F32)
    row = _iota((CHUNK, W), 0)
    hm = [_lane_seg_mask(W, HEAD_DIM, h) for h in range(HGRN_HEADS)]
    hm3 = [jnp.concatenate([m, m, m], axis=1) for m in hm]
    seg = _seg_ones(W, HEAD_DIM)
    seg_b = seg.astype(BF16)
    tt16 = _iota((16, W), 0)
    z32 = jnp.zeros((32, W), F32)
    in_q2 = (row & 16) != 0

    def chunk(c, carry):
        r0 = pl.multiple_of(c * CHUNK, CHUNK)
        u = u_ref[0, pl.ds(r0, CHUNK), :]
        q = u[:, 0:W] * (HEAD_DIM ** -0.5)
        f = u[:, W:2 * W]
        iv = u[:, 2 * W:3 * W]
        g = u[:, 3 * W:4 * W]
        forget = lb + (1.0 - lb) * _sigmoid(f)
        logf = jnp.log(forget)
        kk = 1.0 - forget
        b = _mm_hi(tril, logf)
        b16, b32, b48, bend = b[16:17], b[32:33], b[48:49], b[63:64]

        q1 = q * jnp.exp(jnp.where(row >= 32, b - b32, NEG))
        k1 = kk * jnp.exp(jnp.where(row < 32, b32 - b, NEG))
        ref2 = jnp.where(row < 32, b16, b48)
        q2 = q * jnp.exp(jnp.where(in_q2, b - ref2, NEG))
        k2 = kk * jnp.exp(jnp.where(in_q2, NEG, ref2 - b))
        qcat = jnp.concatenate([q1, jnp.concatenate([q2[:32], z32], 0), jnp.concatenate([z32, q2[32:]], 0)], 1)
        kcat = jnp.concatenate([k1, jnp.concatenate([k2[:32], z32], 0), jnp.concatenate([z32, k2[32:]], 0)], 1)
        qstack = jnp.concatenate([qcat * hm3[h] for h in range(HGRN_HEADS)], axis=0)
        a = _mm_nt(qstack, kcat)
        ofull = _mm(a, iv)
        o = ofull[0:CHUNK] * hm[0]
        for h in range(1, HGRN_HEADS):
            o = o + ofull[h * CHUNK:(h + 1) * CHUNK] * hm[h]

        st = st_ref[...]
        o = o + _mm_nt(q * jnp.exp(b), st)

        diag = []
        for j in range(4):
            bR = b[16 * j:16 * j + 16]
            qR = q[16 * j:16 * j + 16]
            ps = []
            for s in range(16):
                r = 16 * j + s
                ps.append(qR * kk[r:r + 1] * jnp.exp(jnp.where(tt16 >= s, bR - b[r:r + 1], NEG)))
            ab = jnp.dot(jnp.concatenate(ps, axis=0).astype(BF16), seg_b, preferred_element_type=F32)
            od = ab[0:16] * iv[16 * j:16 * j + 1]
            for s in range(1, 16):
                od = od + ab[16 * s:16 * s + 16] * iv[16 * j + s:16 * j + s + 1]
            diag.append(od)
        o = o + jnp.concatenate(diag, axis=0)

        kend = kk * jnp.exp(bend - b)
        st_ref[...] = st * jnp.exp(bend) + _mm_tn(iv, kend) * seg

        ss = _mm_hi(o * o, seg) * (1.0 / HEAD_DIM)
        o_ref[0, pl.ds(r0, CHUNK), :] = o * lax.rsqrt(ss + EPS) * nw * _silu(g)
        return carry

    lax.fori_loop(0, nchunk, chunk, 0)


def _hgrn(uh, lbs, norm_w, layer, blk=256):
    B, S, _ = uh.shape
    return pl.pallas_call(
        functools.partial(_hgrn_kernel, layer, blk // CHUNK),
        grid=(B, S // blk),
        in_specs=[pl.BlockSpec((1, blk, HG_COLS), lambda b, s: (b, s, 0)),
                  pl.BlockSpec(lbs.shape, lambda b, s: (0, 0)),
                  pl.BlockSpec((1, HGRN_W), lambda b, s: (0, 0))],
        out_specs=pl.BlockSpec((1, blk, HGRN_W), lambda b, s: (b, s, 0)),
        out_shape=jax.ShapeDtypeStruct((B, S, HGRN_W), F32),
        scratch_shapes=[pltpu.VMEM((HGRN_W, HGRN_W), F32)],
        compiler_params=_cparams(("parallel", "arbitrary")),
        name="hgrn2",
    )(uh, lbs, norm_w)


def _ssd_kernel(nchunk, u_ref, cw_ref, cb_ref, dtb_ref, alog_ref, d_ref, nw_ref, o_ref,
                st_ref, ext_ref, xc_ref):
    W = SSM_W
    blk = nchunk * CHUNK
    first = pl.program_id(1) == 0

    @pl.when(first)
    def _():
        st_ref[...] = jnp.zeros_like(st_ref)
        ext_ref[0:8, :] = jnp.zeros((8, SSM_CONV_CH), F32)

    @pl.when(jnp.logical_not(first))
    def _():
        ext_ref[0:8, :] = ext_ref[blk:blk + 8, :]

    ext_ref[8:8 + blk, :] = u_ref[0, :, W:W + SSM_CONV_CH]
    cw = cw_ref[...]
    conv = cb_ref[...] + cw[0:1] * ext_ref[5:5 + blk, :]
    for j in range(1, SSM_CONV):
        conv = conv + cw[j:j + 1] * ext_ref[5 + j:5 + j + blk, :]
    xc_ref[...] = _silu(conv)

    dtb = dtb_ref[...]
    a_neg = -jnp.exp(alog_ref[...])
    dsk = d_ref[...]
    nw = nw_ref[...]
    tril = jnp.where(_iota((CHUNK, CHUNK), 0) >= _iota((CHUNK, CHUNK), 1), 1.0, 0.0).astype(F32)
    t_i = _iota((CHUNK, W), 0)
    s_i = _iota((CHUNK, W), 1) & (HEAD_DIM - 1)
    strict = jnp.where(t_i > s_i, 1.0, 0.0).astype(F32)
    lower = t_i >= s_i
    hm = [_lane_seg_mask(W, HEAD_DIM, h) for h in range(SSM_HEADS)]
    zB = jnp.zeros((CHUNK, SSM_STATE), F32)
    gr = _iota((SSM_GN, W), 0) >= SSM_STATE
    gc = _iota((SSM_GN, W), 1) >= (SSM_HEADS // SSM_GROUPS) * HEAD_DIM
    gmask = jnp.where(gr == gc, 1.0, 0.0).astype(F32)
    gw = W // SSM_GROUPS
    seg = _seg_ones(W, gw)

    def chunk(c, carry):
        r0 = pl.multiple_of(c * CHUNK, CHUNK)
        z = u_ref[0, pl.ds(r0, CHUNK), 0:W]
        dtr = u_ref[0, pl.ds(r0, CHUNK), W + SSM_CONV_CH:W + SSM_CONV_CH + W]
        xc = xc_ref[pl.ds(r0, CHUNK), :]
        xs = xc[:, 0:W]
        bm = xc[:, W:W + SSM_GN]
        cm = xc[:, W + SSM_GN:W + 2 * SSM_GN]
        dt = _softplus(dtr + dtb)
        a = dt * a_neg
        dd = _mm_hi(tril, jnp.concatenate([a * strict, a], axis=1))
        dseg = dd[:, 0:W]
        cum = dd[:, W:2 * W]
        decay = jnp.exp(jnp.where(lower, dseg, NEG))
        b0 = jnp.concatenate([bm[:, 0:SSM_STATE], zB], axis=1)
        b1 = jnp.concatenate([zB, bm[:, SSM_STATE:]], axis=1)
        bstack = jnp.concatenate([b0, b0, b0, b1, b1, b1], axis=0)
        scores = _mm_nt(cm, bstack) * decay
        xdt = xs * dt
        xbd = jnp.concatenate([xdt * hm[h] for h in range(SSM_HEADS)], axis=0)
        y = _mm(scores, xbd)
        st = st_ref[...]
        y = y + _mm(cm, st) * jnp.exp(cum)
        cend = cum[CHUNK - 1:CHUNK]
        to_end = jnp.exp(cend - cum) * dt
        st_ref[...] = st * jnp.exp(cend) + _mm_tn(bm, to_end * xs) * gmask
        y = y + dsk * xs
        y = y * _silu(z)
        ss = _mm_hi(y * y, seg) * (1.0 / gw)
        o_ref[0, pl.ds(r0, CHUNK), :] = y * lax.rsqrt(ss + EPS) * nw
        return carry

    lax.fori_loop(0, nchunk, chunk, 0)


def _ssd(us, conv_w, conv_b, dtb, alog, dsk, norm_w, blk=256):
    B, S, _ = us.shape
    vec = lambda n: pl.BlockSpec((1, n), lambda b, s: (0, 0))
    return pl.pallas_call(
        functools.partial(_ssd_kernel, blk // CHUNK),
        grid=(B, S // blk),
        in_specs=[pl.BlockSpec((1, blk, SS_COLS), lambda b, s: (b, s, 0)),
                  pl.BlockSpec((SSM_CONV, SSM_CONV_CH), lambda b, s: (0, 0)),
                  vec(SSM_CONV_CH), vec(SSM_W), vec(SSM_W), vec(SSM_W), vec(SSM_W)],
        out_specs=pl.BlockSpec((1, blk, SSM_W), lambda b, s: (b, s, 0)),
        out_shape=jax.ShapeDtypeStruct((B, S, SSM_W), F32),
        scratch_shapes=[pltpu.VMEM((SSM_GN, SSM_W), F32),
                        pltpu.VMEM((blk + 8, SSM_CONV_CH), F32),
                        pltpu.VMEM((blk, SSM_CONV_CH), F32)],
        compiler_params=_cparams(("parallel", "arbitrary")),
        name="ssd",
    )(us, conv_w, conv_b, dtb, alog, dsk, norm_w)


def _t5_bucket(dist):
    max_exact = NUM_BUCKETS // 2
    d = np.maximum(dist, 1).astype(np.float32)
    large = max_exact + (np.log(d / max_exact) / np.log(MAX_DISTANCE / max_exact)
                         * (NUM_BUCKETS - max_exact)).astype(np.int32)
    large = np.minimum(large, NUM_BUCKETS - 1)
    return np.where(dist < max_exact, dist, large).astype(np.int32)


def _bucket_tables():
    qi = np.arange(ATTN_BLOCK)[:, None]
    kj = np.arange(2 * ATTN_BLOCK)[None, :]
    delta = qi - kj + ATTN_BLOCK
    return np.stack([_t5_bucket(np.maximum(delta, 0) * dil) for _, dil in DILATED_PAIRS])


def _bias_kernel(rb_ref, bk_ref, o_ref):
    bk = bk_ref[0]
    for h in range(ATTN_HEADS):
        acc = jnp.zeros(bk.shape, F32)
        for b in range(NUM_BUCKETS):
            acc = jnp.where(bk == b, rb_ref[b, h], acc)
        o_ref[0, h] = acc


def _bias_tables(rel_bias):
    bk = jnp.asarray(_bucket_tables())
    nbr = len(DILATED_PAIRS)
    return pl.pallas_call(
        _bias_kernel,
        grid=(nbr,),
        in_specs=[pl.BlockSpec(memory_space=pltpu.SMEM),
                  pl.BlockSpec((1, ATTN_BLOCK, 2 * ATTN_BLOCK), lambda i: (i, 0, 0))],
        out_specs=pl.BlockSpec((1, ATTN_HEADS, ATTN_BLOCK, 2 * ATTN_BLOCK), lambda i: (i, 0, 0, 0)),
        out_shape=jax.ShapeDtypeStruct((nbr, ATTN_HEADS, ATTN_BLOCK, 2 * ATTN_BLOCK), F32),
        compiler_params=_cparams(("parallel",)),
        name="attn_bias",
    )(rel_bias, bk)


def _attn_kernel(n_back, cur_ref, prev_ref, qw_ref, kw_ref, bias_ref, o_ref):
    n = pl.program_id(2)
    W = ATTN_W
    cur = cur_ref[0]
    prev = prev_ref[0]
    seg = _seg_ones(W, HEAD_DIM)

    def nrm(x, w):
        ss = _mm_hi(x * x, seg) * (1.0 / HEAD_DIM)
        return x * lax.rsqrt(ss + EPS) * w

    kw = kw_ref[...]
    qn = nrm(cur[:, 0:W], qw_ref[...]) * (HEAD_DIM ** -0.5)
    kmat = jnp.concatenate([nrm(prev[:, W:2 * W], kw), nrm(cur[:, W:2 * W], kw)], axis=0)
    vmat = jnp.concatenate([prev[:, 2 * W:3 * W], cur[:, 2 * W:3 * W]], axis=0)

    qi = _iota((ATTN_BLOCK, 2 * ATTN_BLOCK), 0)
    kj = _iota((ATTN_BLOCK, 2 * ATTN_BLOCK), 1)
    delta = qi - kj + ATTN_BLOCK
    first_key = jnp.where(n > 0, 0, ATTN_BLOCK)
    valid = (delta >= 0) & (delta <= n_back) & (kj >= first_key)
    lane = _iota((ATTN_BLOCK, 2 * HEAD_DIM), 1)
    low = lane < HEAD_DIM
    half = [jnp.where(low[0:1], 1.0, 0.0).astype(F32), jnp.where(low[0:1], 0.0, 1.0).astype(F32)]

    outs, lses = [], []
    for pp in range(ATTN_HEADS // 2):
        sl = slice(2 * HEAD_DIM * pp, 2 * HEAD_DIM * (pp + 1))
        qp, kp, vp = qn[:, sl], kmat[:, sl], vmat[:, sl]
        oh, lh = [], []
        for hh in range(2):
            s = _mm_nt(qp * half[hh], kp) + bias_ref[0, 2 * pp + hh]
            s = jnp.where(valid, s, NEG)
            m = jnp.max(s, axis=-1, keepdims=True)
            p = jnp.exp(s - m)
            l = jnp.sum(p, axis=-1, keepdims=True)
            oh.append(_mm(p, vp) / l)
            lh.append(m + jnp.log(l))
        outs.append(jnp.where(low, oh[0], oh[1]))
        lses.append(jnp.where(low, lh[0], lh[1]))
    o_ref[0] = jnp.concatenate(outs + lses, axis=1)


def _attn_branch(ua, qw, kw, bias_all, branch):
    window, dil = DILATED_PAIRS[branch]
    B, S, _ = ua.shape
    L = S // dil
    nb = L // ATTN_BLOCK
    view = ua.reshape(B, L, dil * AT_COLS)
    out = pl.pallas_call(
        functools.partial(_attn_kernel, window // dil),
        grid=(B, dil, nb),
        in_specs=[pl.BlockSpec((1, ATTN_BLOCK, AT_COLS), lambda b, r, n: (b, n, r)),
                  pl.BlockSpec((1, ATTN_BLOCK, AT_COLS), lambda b, r, n: (b, jnp.maximum(n - 1, 0), r)),
                  pl.BlockSpec((1, ATTN_W), lambda b, r, n: (0, 0)),
                  pl.BlockSpec((1, ATTN_W), lambda b, r, n: (0, 0)),
                  pl.BlockSpec((1, ATTN_HEADS, ATTN_BLOCK, 2 * ATTN_BLOCK), lambda b, r, n: (branch, 0, 0, 0))],
        out_specs=pl.BlockSpec((1, ATTN_BLOCK, 2 * ATTN_W), lambda b, r, n: (b, n, r)),
        out_shape=jax.ShapeDtypeStruct((B, L, dil * 2 * ATTN_W), F32),
        compiler_params=_cparams(("parallel", "parallel", "arbitrary")),
        name=f"dilated_attn_{dil}",
    )(view, view, qw, kw, bias_all)
    return out.reshape(B * S, 2 * ATTN_W)


def _outproj_kernel(x_ref, mh_ref, a1_ref, a2_ref, a3_ref, ms_ref, wo_ref, fw_ref, wq_ref, sk_ref,
                    h_ref, xn_ref, sc_ref):
    W = ATTN_W
    br = [a1_ref[...], a2_ref[...], a3_ref[...]]
    lse = [a[:, W:2 * W] for a in br]
    m = jnp.maximum(jnp.maximum(lse[0], lse[1]), lse[2])
    e = [jnp.exp(l - m) for l in lse]
    oa = (e[0] * br[0][:, 0:W] + e[1] * br[1][:, 0:W] + e[2] * br[2][:, 0:W]) / (e[0] + e[1] + e[2])
    h = (x_ref[...] + _mm(mh_ref[...], wo_ref[0:HGRN_W, :])
         + _mm(oa, wo_ref[HGRN_W:HGRN_W + W, :])
         + _mm(ms_ref[...], wo_ref[HGRN_W + W:, :]))
    h_ref[...] = h
    xn = (h * lax.rsqrt(jnp.mean(h * h, axis=-1, keepdims=True) + EPS) * fw_ref[...]).astype(BF16)
    xn_ref[...] = xn
    qp = jnp.dot(xn, wq_ref[...], preferred_element_type=F32)
    for hd in range(PEER_HEADS):
        qh = qp[:, 2 * PEER_HALF * hd:2 * PEER_HALF * (hd + 1)]
        for c in range(2):
            sc_ref[hd, c] = _mm_nt(sk_ref[c], qh)


def _outproj(x2, mh, a1, a2, a3, ms, w_out, ffn_w, wq, sk_pad, tm=256):
    T, D = x2.shape
    row = lambda n: pl.BlockSpec((tm, n), lambda i: (i, 0))
    full = lambda a: pl.BlockSpec(a.shape, lambda i: (0,) * a.ndim)
    return pl.pallas_call(
        _outproj_kernel,
        grid=(T // tm,),
        in_specs=[row(D), row(HGRN_W), row(2 * ATTN_W), row(2 * ATTN_W), row(2 * ATTN_W), row(SSM_W),
                  full(w_out), pl.BlockSpec((1, D), lambda i: (0, 0)), full(wq), full(sk_pad)],
        out_specs=[row(D), row(D),
                   pl.BlockSpec((PEER_HEADS, 2, PEER_NKEYS, tm), lambda i: (0, 0, 0, i))],
        out_shape=[jax.ShapeDtypeStruct((T, D), F32), jax.ShapeDtypeStruct((T, D), BF16),
                   jax.ShapeDtypeStruct((PEER_HEADS, 2, PEER_NKEYS, T), F32)],
        compiler_params=_cparams(("parallel",)),
        name="outproj_query",
    )(x2, mh, a1, a2, a3, ms, w_out, ffn_w.reshape(1, D), wq, sk_pad)


_NSEL = PEER_TOPK + 1
_PAIRS = [(i, j) for i in range(_NSEL) for j in range(_NSEL) if (i + 1) * (j + 1) <= _NSEL]


def _top_values(v, k):
    out = []
    for _ in range(k):
        m = jnp.max(v, axis=0, keepdims=True)
        out.append(m)
        v = jnp.where(v == m, NEG, v)
    return out


def _route_kernel(sc_ref, th_ref, u1_ref, u2_ref):
    tt = sc_ref.shape[-1]
    row8 = _iota((8, tt), 0)

    def head(hd, carry):
        s1 = sc_ref[hd, 0]
        s2 = sc_ref[hd, 1]
        a = _top_values(s1, _NSEL)
        b = _top_values(s2, _NSEL)
        groups = []
        for g0 in range(0, len(_PAIRS), 8):
            cg = jnp.full((8, tt), NEG, F32)
            for p, (i, j) in enumerate(_PAIRS[g0:g0 + 8]):
                cg = jnp.where(row8 == p, a[i] + b[j], cg)
            groups.append(cg)
        best = _top_values(jnp.concatenate(groups, axis=0), _NSEL)
        zsum = jnp.ones((1, tt), F32)
        for r in range(1, PEER_TOPK):
            zsum = zsum + jnp.exp(best[r] - best[0])
        cut = 0.5 * (best[PEER_TOPK - 1] + best[PEER_TOPK])
        th_ref[hd] = cut - s1
        u1_ref[hd] = jnp.exp(s1 - a[0]) / zsum
        u2_ref[hd] = jnp.exp(s2 - b[0])
        return carry

    lax.fori_loop(0, PEER_HEADS, head, 0)


def _route(sc, tt=256):
    T = sc.shape[-1]
    blk = pl.BlockSpec((PEER_HEADS, PEER_NKEYS, tt), lambda i: (0, 0, i))
    shp = jax.ShapeDtypeStruct((PEER_HEADS, PEER_NKEYS, T), F32)
    return pl.pallas_call(
        _route_kernel,
        grid=(T // tt,),
        in_specs=[pl.BlockSpec((PEER_HEADS, 2, PEER_NKEYS, tt), lambda i: (0, 0, 0, i))],
        out_specs=[blk, blk, blk],
        out_shape=[shp, shp, shp],
        compiler_params=_cparams(("parallel",)),
        name="peer_route",
    )(sc)


def _peer_kernel(nj, xn_ref, dn_ref, up_ref, s2_ref, th_ref, u1_ref, u2_ref, h_ref, o_ref,
                 acc_ref, wg_ref):
    j = pl.program_id(1)
    te = dn_ref.shape[0]
    ng = te // PEER_NKEYS

    @pl.when(j == 0)
    def _():
        acc_ref[...] = jnp.zeros_like(acc_ref)

    at = lax.dot_general(dn_ref[...], xn_ref[...], (((1,), (1,)), ((), ())), preferred_element_type=F32)
    for cc in range(ng):
        c = j * ng + cc
        a = at[cc * PEER_NKEYS:(cc + 1) * PEER_NKEYS]
        g = 0.5 * a * (1.0 + jnp.tanh(math.sqrt(2.0 / math.pi) * (a + 0.044715 * (a * a * a))))
        w = None
        for hd in range(PEER_HEADS):
            th = th_ref[hd, pl.ds(c, 1), :]
            u1 = u1_ref[hd, pl.ds(c, 1), :]
            t = jnp.where(s2_ref[hd] >= th, u2_ref[hd], 0.0) * u1
            w = t if w is None else w + t
        wg_ref[cc * PEER_NKEYS:(cc + 1) * PEER_NKEYS, :] = (w * g).astype(BF16)
    acc_ref[...] += jnp.dot(up_ref[...], wg_ref[...], preferred_element_type=F32)

    @pl.when(j == nj - 1)
    def _():
        o_ref[...] = h_ref[...] + acc_ref[...].T


def _peer(xn, down_b, up_t, sc, th, u1, u2, h, tt=512, te=512):
    T, D = xn.shape
    E = down_b.shape[0]
    nj = E // te
    rt = pl.BlockSpec((PEER_HEADS, PEER_NKEYS, tt), lambda i, j: (0, 0, i))
    return pl.pallas_call(
        functools.partial(_peer_kernel, nj),
        grid=(T // tt, nj),
        in_specs=[pl.BlockSpec((tt, D), lambda i, j: (i, 0)),
                  pl.BlockSpec((te, D), lambda i, j: (j, 0)),
                  pl.BlockSpec((D, te), lambda i, j: (0, j)),
                  pl.BlockSpec((PEER_HEADS, None, PEER_NKEYS, tt), lambda i, j: (0, 1, 0, i)),
                  rt, rt, rt,
                  pl.BlockSpec((tt, D), lambda i, j: (i, 0))],
        out_specs=pl.BlockSpec((tt, D), lambda i, j: (i, 0)),
        out_shape=jax.ShapeDtypeStruct((T, D), F32),
        scratch_shapes=[pltpu.VMEM((D, tt), F32), pltpu.VMEM((te, tt), BF16)],
        compiler_params=_cparams(("parallel", "arbitrary")),
        name="peer_experts",
    )(xn, down_b, up_t, sc, th, u1, u2, h)


def _rep(v, n):
    return jnp.repeat(v, n, axis=-1)


def _layer(x2, B, S, layer, p, bias_all):
    T, D = x2.shape
    n_plain = HG_COLS + AT_COLS + SSM_W + SSM_CONV_CH
    w_in = p["w_in"]
    w_cat = jnp.concatenate([w_in[:, :n_plain], _rep(w_in[:, n_plain:], HEAD_DIM)], axis=1).astype(BF16)
    uh, ua, us = _inproj(x2, p["attn_norm_w"], w_cat)

    mh = _hgrn(uh.reshape(B, S, HG_COLS), p["hgrn_lower_bounds"],
               jnp.tile(p["hgrn_norm_w"], HGRN_HEADS).reshape(1, HGRN_W), layer)
    qw = jnp.tile(p["q_norm_w"], ATTN_HEADS).reshape(1, ATTN_W)
    kw = jnp.tile(p["k_norm_w"], ATTN_HEADS).reshape(1, ATTN_W)
    ua3 = ua.reshape(B, S, AT_COLS)
    att = [_attn_branch(ua3, qw, kw, bias_all, br) for br in range(len(DILATED_PAIRS))]
    ms = _ssd(us.reshape(B, S, SS_COLS), p["ssm_conv_w"], p["ssm_conv_b"].reshape(1, -1),
              _rep(p["ssm_dt_bias"], HEAD_DIM).reshape(1, -1), _rep(p["ssm_a_log"], HEAD_DIM).reshape(1, -1),
              _rep(p["ssm_d"], HEAD_DIM).reshape(1, -1), p["ssm_norm_w"].reshape(1, -1))

    sk = p["peer_sub_keys"]
    zk = jnp.zeros_like(sk[0])
    sk_pad = jnp.stack([jnp.concatenate([sk[0], zk], axis=1), jnp.concatenate([zk, sk[1]], axis=1)])
    h, xn, sc = _outproj(x2, mh.reshape(T, HGRN_W), att[0], att[1], att[2], ms.reshape(T, SSM_W),
                         p["w_out"].astype(BF16), p["ffn_norm_w"], p["peer_w_query"].astype(BF16), sk_pad)
    th, u1, u2 = _route(sc)
    return _peer(xn, p["peer_down"].astype(BF16), p["peer_up"].T.astype(BF16), sc, th, u1, u2, h)


_PER_LAYER = ("attn_norm_w", "w_in", "hgrn_norm_w", "q_norm_w", "k_norm_w", "ssm_conv_w", "ssm_conv_b",
              "ssm_dt_bias", "ssm_a_log", "ssm_d", "ssm_norm_w", "w_out", "ffn_norm_w", "peer_w_query",
              "peer_sub_keys", "peer_down", "peer_up")


def kernel(x, attn_norm_w, w_in, hgrn_lower_bounds, hgrn_norm_w, q_norm_w, k_norm_w, rel_bias, ssm_conv_w, ssm_conv_b, ssm_dt_bias, ssm_a_log, ssm_d, ssm_norm_w, w_out, ffn_norm_w, peer_w_query, peer_sub_keys, peer_down, peer_up):
    stacked = dict(attn_norm_w=attn_norm_w, w_in=w_in, hgrn_norm_w=hgrn_norm_w, q_norm_w=q_norm_w,
                   k_norm_w=k_norm_w, ssm_conv_w=ssm_conv_w, ssm_conv_b=ssm_conv_b, ssm_dt_bias=ssm_dt_bias,
                   ssm_a_log=ssm_a_log, ssm_d=ssm_d, ssm_norm_w=ssm_norm_w, w_out=w_out, ffn_norm_w=ffn_norm_w,
                   peer_w_query=peer_w_query, peer_sub_keys=peer_sub_keys, peer_down=peer_down, peer_up=peer_up)
    B, S, D = x.shape
    bias_all = _bias_tables(rel_bias)
    x2 = x.reshape(B * S, D)
    for layer in range(w_in.shape[0]):
        p = {k: stacked[k][layer] for k in _PER_LAYER}
        p["hgrn_lower_bounds"] = hgrn_lower_bounds
        x2 = _layer(x2, B, S, layer, p, bias_all)
    return x2.reshape(B, S, D)
```

```python
import functools
import math

import numpy as np
import jax
import jax.numpy as jnp
from jax import lax
from jax.experimental import pallas as pl
from jax.experimental.pallas import tpu as pltpu

F32 = jnp.float32
BF16 = jnp.bfloat16
HIGHEST = lax.Precision.HIGHEST
NEG = -1e30
EPS = 1e-6

HEAD_DIM = 64
CHUNK = 64
HGRN_HEADS = 4
HGRN_W = HGRN_HEADS * HEAD_DIM
ATTN_HEADS = 6
ATTN_W = ATTN_HEADS * HEAD_DIM
ATTN_BLOCK = 128
DILATED_PAIRS = ((128, 1), (512, 4), (2048, 16))
NUM_BUCKETS = 32
MAX_DISTANCE = 2048
SSM_HEADS = 6
SSM_W = SSM_HEADS * HEAD_DIM
SSM_GROUPS = 2
SSM_STATE = 128
SSM_GN = SSM_GROUPS * SSM_STATE
SSM_CONV = 4
SSM_CONV_CH = SSM_W + 2 * SSM_GN
PEER_HEADS = 8
PEER_NKEYS = 128
PEER_HALF = 64
PEER_TOPK = 16
HG_COLS = 4 * HGRN_W
AT_COLS = 3 * ATTN_W
SS_COLS = SSM_W + SSM_CONV_CH + SSM_W

VMEM_LIMIT = 56 * 1024 * 1024


def _cparams(sem, flags=None):
    return pltpu.CompilerParams(dimension_semantics=sem, vmem_limit_bytes=VMEM_LIMIT, flags=flags)


def _iota(shape, dim):
    return lax.broadcasted_iota(jnp.int32, shape, dim)


def _mm(a, b):
    return jnp.dot(a.astype(BF16), b.astype(BF16), preferred_element_type=F32)


def _mm_nt(a, b):
    return lax.dot_general(a.astype(BF16), b.astype(BF16), (((1,), (1,)), ((), ())),
                           preferred_element_type=F32)


def _mm_tn(a, b):
    return lax.dot_general(a.astype(BF16), b.astype(BF16), (((0,), (0,)), ((), ())),
                           preferred_element_type=F32)


def _mm_hi(a, b):
    return jnp.dot(a, b, precision=HIGHEST, preferred_element_type=F32)


def _sigmoid(x):
    return 1.0 / (1.0 + jnp.exp(-x))


def _silu(x):
    return x * _sigmoid(x)


def _softplus(x):
    return jnp.maximum(x, 0.0) + jnp.log(1.0 + jnp.exp(-jnp.abs(x)))


def _seg_ones(n, seg):
    r = _iota((n, n), 0)
    c = _iota((n, n), 1)
    same = None
    for s in range(n // seg):
        lo, hi = s * seg, (s + 1) * seg
        t = (r >= lo) & (r < hi) & (c >= lo) & (c < hi)
        same = t if same is None else (same | t)
    return jnp.where(same, 1.0, 0.0).astype(F32)


def _lane_seg_mask(width, seg, idx):
    l = _iota((1, width), 1)
    return jnp.where((l >= idx * seg) & (l < (idx + 1) * seg), 1.0, 0.0).astype(F32)


def _inproj_kernel(x_ref, nw_ref, w_ref, oh_ref, oa_ref, os_ref):
    x = x_ref[...]
    xn = x * lax.rsqrt(jnp.mean(x * x, axis=-1, keepdims=True) + EPS) * nw_ref[...]
    xb = xn.astype(BF16)
    oh_ref[...] = jnp.dot(xb, w_ref[:, 0:HG_COLS], preferred_element_type=F32)
    oa_ref[...] = jnp.dot(xb, w_ref[:, HG_COLS:HG_COLS + AT_COLS], preferred_element_type=F32)
    os_ref[...] = jnp.dot(xb, w_ref[:, HG_COLS + AT_COLS:], preferred_element_type=F32)


def _inproj(x2, norm_w, w_cat, tm=256):
    T, D = x2.shape
    ncol = w_cat.shape[1]
    return pl.pallas_call(
        _inproj_kernel,
        grid=(T // tm,),
        in_specs=[pl.BlockSpec((tm, D), lambda i: (i, 0)),
                  pl.BlockSpec((1, D), lambda i: (0, 0)),
                  pl.BlockSpec((D, ncol), lambda i: (0, 0))],
        out_specs=[pl.BlockSpec((tm, HG_COLS), lambda i: (i, 0)),
                   pl.BlockSpec((tm, AT_COLS), lambda i: (i, 0)),
                   pl.BlockSpec((tm, SS_COLS), lambda i: (i, 0))],
        out_shape=[jax.ShapeDtypeStruct((T, HG_COLS), F32),
                   jax.ShapeDtypeStruct((T, AT_COLS), F32),
                   jax.ShapeDtypeStruct((T, SS_COLS), F32)],
        compiler_params=_cparams(("parallel",)),
        name="inproj",
    )(x2, norm_w.reshape(1, D), w_cat)


def _hgrn_kernel(layer, nchunk, u_ref, lbs_ref, nw_ref, o_ref, st_ref):
    W = HGRN_W

    @pl.when(pl.program_id(1) == 0)
    def _():
        st_ref[...] = jnp.zeros_like(st_ref)

    lbs = lbs_ref[...]
    e = jnp.exp(lbs - jnp.max(lbs, axis=0, keepdims=True))
    sm = e / jnp.sum(e, axis=0, keepdims=True)
    lb = jnp.zeros((1, W), F32)
    for j in range(1, layer + 1):
        lb = lb + sm[j:j + 1]
    nw = nw_ref[...]

    tril = jnp.where(_iota((CHUNK, CHUNK), 0) >= _iota((CHUNK, CHUNK), 1), 1.0, 0.0).astype(F32)
    row = _iota((CHUNK, W), 0)
    hm = [_lane_seg_mask(W, HEAD_DIM, h) for h in range(HGRN_HEADS)]
    hm3 = [jnp.concatenate([m, m, m], axis=1) for m in hm]
    seg = _seg_ones(W, HEAD_DIM)
    seg_b = seg.astype(BF16)
    tt16 = _iota((16, W), 0)
    z32 = jnp.zeros((32, W), F32)
    in_q2 = (row & 16) != 0

    def chunk(c, carry):
        r0 = pl.multiple_of(c * CHUNK, CHUNK)
        u = u_ref[0, pl.ds(r0, CHUNK), :]
        q = u[:, 0:W] * (HEAD_DIM ** -0.5)
        f = u[:, W:2 * W]
        iv = u[:, 2 * W:3 * W]
        g = u[:, 3 * W:4 * W]
        forget = lb + (1.0 - lb) * _sigmoid(f)
        logf = jnp.log(forget)
        kk = 1.0 - forget
        b = _mm_hi(tril, logf)
        b16, b32, b48, bend = b[16:17], b[32:33], b[48:49], b[63:64]

        q1 = q * jnp.exp(jnp.where(row >= 32, b - b32, NEG))
        k1 = kk * jnp.exp(jnp.where(row < 32, b32 - b, NEG))
        ref2 = jnp.where(row < 32, b16, b48)
        q2 = q * jnp.exp(jnp.where(in_q2, b - ref2, NEG))
        k2 = kk * jnp.exp(jnp.where(in_q2, NEG, ref2 - b))
        qcat = jnp.concatenate([q1, jnp.concatenate([q2[:32], z32], 0), jnp.concatenate([z32, q2[32:]], 0)], 1)
        kcat = jnp.concatenate([k1, jnp.concatenate([k2[:32], z32], 0), jnp.concatenate([z32, k2[32:]], 0)], 1)
        qstack = jnp.concatenate([qcat * hm3[h] for h in range(HGRN_HEADS)], axis=0)
        a = _mm_nt(qstack, kcat)
        ofull = _mm(a, iv)
        o = ofull[0:CHUNK] * hm[0]
        for h in range(1, HGRN_HEADS):
            o = o + ofull[h * CHUNK:(h + 1) * CHUNK] * hm[h]

        st = st_ref[...]
        o = o + _mm_nt(q * jnp.exp(b), st)

        diag = []
        for j in range(4):
            bR = b[16 * j:16 * j + 16]
            qR = q[16 * j:16 * j + 16]
            ps = []
            for s in range(16):
                r = 16 * j + s
                ps.append(qR * kk[r:r + 1] * jnp.exp(jnp.where(tt16 >= s, bR - b[r:r + 1], NEG)))
            ab = jnp.dot(jnp.concatenate(ps, axis=0).astype(BF16), seg_b, preferred_element_type=F32)
            od = ab[0:16] * iv[16 * j:16 * j + 1]
            for s in range(1, 16):
                od = od + ab[16 * s:16 * s + 16] * iv[16 * j + s:16 * j + s + 1]
            diag.append(od)
        o = o + jnp.concatenate(diag, axis=0)

        kend = kk * jnp.exp(bend - b)
        st_ref[...] = st * jnp.exp(bend) + _mm_tn(iv, kend) * seg

        ss = _mm_hi(o * o, seg) * (1.0 / HEAD_DIM)
        o_ref[0, pl.ds(r0, CHUNK), :] = o * lax.rsqrt(ss + EPS) * nw * _silu(g)
        return carry

    lax.fori_loop(0, nchunk, chunk, 0)


def _hgrn(uh, lbs, norm_w, layer, blk=256):
    B, S, _ = uh.shape
    return pl.pallas_call(
        functools.partial(_hgrn_kernel, layer, blk // CHUNK),
        grid=(B, S // blk),
        in_specs=[pl.BlockSpec((1, blk, HG_COLS), lambda b, s: (b, s, 0)),
                  pl.BlockSpec(lbs.shape, lambda b, s: (0, 0)),
                  pl.BlockSpec((1, HGRN_W), lambda b, s: (0, 0))],
        out_specs=pl.BlockSpec((1, blk, HGRN_W), lambda b, s: (b, s, 0)),
        out_shape=jax.ShapeDtypeStruct((B, S, HGRN_W), F32),
        scratch_shapes=[pltpu.VMEM((HGRN_W, HGRN_W), F32)],
        compiler_params=_cparams(("parallel", "arbitrary")),
        name="hgrn2",
    )(uh, lbs, norm_w)


def _ssd_kernel(nchunk, u_ref, cw_ref, cb_ref, dtb_ref, alog_ref, d_ref, nw_ref, o_ref,
                st_ref, ext_ref, xc_ref):
    W = SSM_W
    blk = nchunk * CHUNK
    first = pl.program_id(1) == 0

    @pl.when(first)
    def _():
        st_ref[...] = jnp.zeros_like(st_ref)
        ext_ref[0:8, :] = jnp.zeros((8, SSM_CONV_CH), F32)

    @pl.when(jnp.logical_not(first))
    def _():
        ext_ref[0:8, :] = ext_ref[blk:blk + 8, :]

    ext_ref[8:8 + blk, :] = u_ref[0, :, W:W + SSM_CONV_CH]
    cw = cw_ref[...]
    conv = cb_ref[...] + cw[0:1] * ext_ref[5:5 + blk, :]
    for j in range(1, SSM_CONV):
        conv = conv + cw[j:j + 1] * ext_ref[5 + j:5 + j + blk, :]
    xc_ref[...] = _silu(conv)

    dtb = dtb_ref[...]
    a_neg = -jnp.exp(alog_ref[...])
    dsk = d_ref[...]
    nw = nw_ref[...]
    tril = jnp.where(_iota((CHUNK, CHUNK), 0) >= _iota((CHUNK, CHUNK), 1), 1.0, 0.0).astype(F32)
    t_i = _iota((CHUNK, W), 0)
    s_i = _iota((CHUNK, W), 1) & (HEAD_DIM - 1)
    strict = jnp.where(t_i > s_i, 1.0, 0.0).astype(F32)
    lower = t_i >= s_i
    hm = [_lane_seg_mask(W, HEAD_DIM, h) for h in range(SSM_HEADS)]
    zB = jnp.zeros((CHUNK, SSM_STATE), F32)
    gr = _iota((SSM_GN, W), 0) >= SSM_STATE
    gc = _iota((SSM_GN, W), 1) >= (SSM_HEADS // SSM_GROUPS) * HEAD_DIM
    gmask = jnp.where(gr == gc, 1.0, 0.0).astype(F32)
    gw = W // SSM_GROUPS
    seg = _seg_ones(W, gw)

    def chunk(c, carry):
        r0 = pl.multiple_of(c * CHUNK, CHUNK)
        z = u_ref[0, pl.ds(r0, CHUNK), 0:W]
        dtr = u_ref[0, pl.ds(r0, CHUNK), W + SSM_CONV_CH:W + SSM_CONV_CH + W]
        xc = xc_ref[pl.ds(r0, CHUNK), :]
        xs = xc[:, 0:W]
        bm = xc[:, W:W + SSM_GN]
        cm = xc[:, W + SSM_GN:W + 2 * SSM_GN]
        dt = _softplus(dtr + dtb)
        a = dt * a_neg
        dd = _mm_hi(tril, jnp.concatenate([a * strict, a], axis=1))
        dseg = dd[:, 0:W]
        cum = dd[:, W:2 * W]
        decay = jnp.exp(jnp.where(lower, dseg, NEG))
        b0 = jnp.concatenate([bm[:, 0:SSM_STATE], zB], axis=1)
        b1 = jnp.concatenate([zB, bm[:, SSM_STATE:]], axis=1)
        bstack = jnp.concatenate([b0, b0, b0, b1, b1, b1], axis=0)
        scores = _mm_nt(cm, bstack) * decay
        xdt = xs * dt
        xbd = jnp.concatenate([xdt * hm[h] for h in range(SSM_HEADS)], axis=0)
        y = _mm(scores, xbd)
        st = st_ref[...]
        y = y + _mm(cm, st) * jnp.exp(cum)
        cend = cum[CHUNK - 1:CHUNK]
        to_end = jnp.exp(cend - cum) * dt
        st_ref[...] = st * jnp.exp(cend) + _mm_tn(bm, to_end * xs) * gmask
        y = y + dsk * xs
        y = y * _silu(z)
        ss = _mm_hi(y * y, seg) * (1.0 / gw)
        o_ref[0, pl.ds(r0, CHUNK), :] = y * lax.rsqrt(ss + EPS) * nw
        return carry

    lax.fori_loop(0, nchunk, chunk, 0)


def _ssd(us, conv_w, conv_b, dtb, alog, dsk, norm_w, blk=256):
    B, S, _ = us.shape
    vec = lambda n: pl.BlockSpec((1, n), lambda b, s: (0, 0))
    return pl.pallas_call(
        functools.partial(_ssd_kernel, blk // CHUNK),
        grid=(B, S // blk),
        in_specs=[pl.BlockSpec((1, blk, SS_COLS), lambda b, s: (b, s, 0)),
                  pl.BlockSpec((SSM_CONV, SSM_CONV_CH), lambda b, s: (0, 0)),
                  vec(SSM_CONV_CH), vec(SSM_W), vec(SSM_W), vec(SSM_W), vec(SSM_W)],
        out_specs=pl.BlockSpec((1, blk, SSM_W), lambda b, s: (b, s, 0)),
        out_shape=jax.ShapeDtypeStruct((B, S, SSM_W), F32),
        scratch_shapes=[pltpu.VMEM((SSM_GN, SSM_W), F32),
                        pltpu.VMEM((blk + 8, SSM_CONV_CH), F32),
                        pltpu.VMEM((blk, SSM_CONV_CH), F32)],
        compiler_params=_cparams(("parallel", "arbitrary")),
        name="ssd",
    )(us, conv_w, conv_b, dtb, alog, dsk, norm_w)


def _t5_bucket(dist):
    max_exact = NUM_BUCKETS // 2
    d = np.maximum(dist, 1).astype(np.float32)
    large = max_exact + (np.log(d / max_exact) / np.log(MAX_DISTANCE / max_exact)
                         * (NUM_BUCKETS - max_exact)).astype(np.int32)
    large = np.minimum(large, NUM_BUCKETS - 1)
    return np.where(dist < max_exact, dist, large).astype(np.int32)


def _bucket_tables():
    qi = np.arange(ATTN_BLOCK)[:, None]
    kj = np.arange(2 * ATTN_BLOCK)[None, :]
    delta = qi - kj + ATTN_BLOCK
    return np.stack([_t5_bucket(np.maximum(delta, 0) * dil) for _, dil in DILATED_PAIRS])


def _bias_kernel(rb_ref, bk_ref, o_ref):
    bk = bk_ref[0]
    for h in range(ATTN_HEADS):
        acc = jnp.zeros(bk.shape, F32)
        for b in range(NUM_BUCKETS):
            acc = jnp.where(bk == b, rb_ref[b, h], acc)
        o_ref[0, h] = acc


def _bias_tables(rel_bias):
    bk = jnp.asarray(_bucket_tables())
    nbr = len(DILATED_PAIRS)
    return pl.pallas_call(
        _bias_kernel,
        grid=(nbr,),
        in_specs=[pl.BlockSpec(memory_space=pltpu.SMEM),
                  pl.BlockSpec((1, ATTN_BLOCK, 2 * ATTN_BLOCK), lambda i: (i, 0, 0))],
        out_specs=pl.BlockSpec((1, ATTN_HEADS, ATTN_BLOCK, 2 * ATTN_BLOCK), lambda i: (i, 0, 0, 0)),
        out_shape=jax.ShapeDtypeStruct((nbr, ATTN_HEADS, ATTN_BLOCK, 2 * ATTN_BLOCK), F32),
        compiler_params=_cparams(("parallel",)),
        name="attn_bias",
    )(rel_bias, bk)


def _attn_kernel(n_back, cur_ref, prev_ref, qw_ref, kw_ref, bias_ref, o_ref):
    n = pl.program_id(2)
    W = ATTN_W
    cur = cur_ref[0]
    prev = prev_ref[0]
    seg = _seg_ones(W, HEAD_DIM)

    def nrm(x, w):
        ss = _mm_hi(x * x, seg) * (1.0 / HEAD_DIM)
        return x * lax.rsqrt(ss + EPS) * w

    kw = kw_ref[...]
    qn = nrm(cur[:, 0:W], qw_ref[...]) * (HEAD_DIM ** -0.5)
    kmat = jnp.concatenate([nrm(prev[:, W:2 * W], kw), nrm(cur[:, W:2 * W], kw)], axis=0)
    vmat = jnp.concatenate([prev[:, 2 * W:3 * W], cur[:, 2 * W:3 * W]], axis=0)

    qi = _iota((ATTN_BLOCK, 2 * ATTN_BLOCK), 0)
    kj = _iota((ATTN_BLOCK, 2 * ATTN_BLOCK), 1)
    delta = qi - kj + ATTN_BLOCK
    first_key = jnp.where(n > 0, 0, ATTN_BLOCK)
    valid = (delta >= 0) & (delta <= n_back) & (kj >= first_key)
    lane = _iota((ATTN_BLOCK, 2 * HEAD_DIM), 1)
    low = lane < HEAD_DIM
    half = [jnp.where(low[0:1], 1.0, 0.0).astype(F32), jnp.where(low[0:1], 0.0, 1.0).astype(F32)]

    outs, lses = [], []
    for pp in range(ATTN_HEADS // 2):
        sl = slice(2 * HEAD_DIM * pp, 2 * HEAD_DIM * (pp + 1))
        qp, kp, vp = qn[:, sl], kmat[:, sl], vmat[:, sl]
        oh, lh = [], []
        for hh in range(2):
            s = _mm_nt(qp * half[hh], kp) + bias_ref[0, 2 * pp + hh]
            s = jnp.where(valid, s, NEG)
            m = jnp.max(s, axis=-1, keepdims=True)
            p = jnp.exp(s - m)
            l = jnp.sum(p, axis=-1, keepdims=True)
            oh.append(_mm(p, vp) / l)
            lh.append(m + jnp.log(l))
        outs.append(jnp.where(low, oh[0], oh[1]))
        lses.append(jnp.where(low, lh[0], lh[1]))
    o_ref[0] = jnp.concatenate(outs + lses, axis=1)


def _attn_branch(ua, qw, kw, bias_all, branch):
    window, dil = DILATED_PAIRS[branch]
    B, S, _ = ua.shape
    L = S // dil
    nb = L // ATTN_BLOCK
    view = ua.reshape(B, L, dil * AT_COLS)
    out = pl.pallas_call(
        functools.partial(_attn_kernel, window // dil),
        grid=(B, dil, nb),
        in_specs=[pl.BlockSpec((1, ATTN_BLOCK, AT_COLS), lambda b, r, n: (b, n, r)),
                  pl.BlockSpec((1, ATTN_BLOCK, AT_COLS), lambda b, r, n: (b, jnp.maximum(n - 1, 0), r)),
                  pl.BlockSpec((1, ATTN_W), lambda b, r, n: (0, 0)),
                  pl.BlockSpec((1, ATTN_W), lambda b, r, n: (0, 0)),
                  pl.BlockSpec((1, ATTN_HEADS, ATTN_BLOCK, 2 * ATTN_BLOCK), lambda b, r, n: (branch, 0, 0, 0))],
        out_specs=pl.BlockSpec((1, ATTN_BLOCK, 2 * ATTN_W), lambda b, r, n: (b, n, r)),
        out_shape=jax.ShapeDtypeStruct((B, L, dil * 2 * ATTN_W), F32),
        compiler_params=_cparams(("parallel", "parallel", "arbitrary")),
        name=f"dilated_attn_{dil}",
    )(view, view, qw, kw, bias_all)
    return out.reshape(B * S, 2 * ATTN_W)


def _outproj_kernel(x_ref, mh_ref, a1_ref, a2_ref, a3_ref, ms_ref, wo_ref, fw_ref, wq_ref, sk_ref,
                    h_ref, xn_ref, sc_ref):
    W = ATTN_W
    br = [a1_ref[...], a2_ref[...], a3_ref[...]]
    lse = [a[:, W:2 * W] for a in br]
    m = jnp.maximum(jnp.maximum(lse[0], lse[1]), lse[2])
    e = [jnp.exp(l - m) for l in lse]
    oa = (e[0] * br[0][:, 0:W] + e[1] * br[1][:, 0:W] + e[2] * br[2][:, 0:W]) / (e[0] + e[1] + e[2])
    h = (x_ref[...] + _mm(mh_ref[...], wo_ref[0:HGRN_W, :])
         + _mm(oa, wo_ref[HGRN_W:HGRN_W + W, :])
         + _mm(ms_ref[...], wo_ref[HGRN_W + W:, :]))
    h_ref[...] = h
    xn = (h * lax.rsqrt(jnp.mean(h * h, axis=-1, keepdims=True) + EPS) * fw_ref[...]).astype(BF16)
    xn_ref[...] = xn
    qp = jnp.dot(xn, wq_ref[...], preferred_element_type=F32)
    for hd in range(PEER_HEADS):
        qh = qp[:, 2 * PEER_HALF * hd:2 * PEER_HALF * (hd + 1)]
        for c in range(2):
            sc_ref[hd, c] = _mm_nt(sk_ref[c], qh)


def _outproj(x2, mh, a1, a2, a3, ms, w_out, ffn_w, wq, sk_pad, tm=256):
    T, D = x2.shape
    row = lambda n: pl.BlockSpec((tm, n), lambda i: (i, 0))
    full = lambda a: pl.BlockSpec(a.shape, lambda i: (0,) * a.ndim)
    return pl.pallas_call(
        _outproj_kernel,
        grid=(T // tm,),
        in_specs=[row(D), row(HGRN_W), row(2 * ATTN_W), row(2 * ATTN_W), row(2 * ATTN_W), row(SSM_W),
                  full(w_out), pl.BlockSpec((1, D), lambda i: (0, 0)), full(wq), full(sk_pad)],
        out_specs=[row(D), row(D),
                   pl.BlockSpec((PEER_HEADS, 2, PEER_NKEYS, tm), lambda i: (0, 0, 0, i))],
        out_shape=[jax.ShapeDtypeStruct((T, D), F32), jax.ShapeDtypeStruct((T, D), BF16),
                   jax.ShapeDtypeStruct((PEER_HEADS, 2, PEER_NKEYS, T), F32)],
        compiler_params=_cparams(("parallel",)),
        name="outproj_query",
    )(x2, mh, a1, a2, a3, ms, w_out, ffn_w.reshape(1, D), wq, sk_pad)


_NSEL = PEER_TOPK + 1
_PAIRS = [(i, j) for i in range(_NSEL) for j in range(_NSEL) if (i + 1) * (j + 1) <= _NSEL]


def _top_values(v, k):
    out = []
    for _ in range(k):
        m = jnp.max(v, axis=0, keepdims=True)
        out.append(m)
        v = jnp.where(v == m, NEG, v)
    return out


def _route_kernel(sc_ref, th_ref, u1_ref, u2_ref):
    tt = sc_ref.shape[-1]
    row8 = _iota((8, tt), 0)

    def head(hd, carry):
        s1 = sc_ref[hd, 0]
        s2 = sc_ref[hd, 1]
        a = _top_values(s1, _NSEL)
        b = _top_values(s2, _NSEL)
        groups = []
        for g0 in range(0, len(_PAIRS), 8):
            cg = jnp.full((8, tt), NEG, F32)
            for p, (i, j) in enumerate(_PAIRS[g0:g0 + 8]):
                cg = jnp.where(row8 == p, a[i] + b[j], cg)
            groups.append(cg)
        best = _top_values(jnp.concatenate(groups, axis=0), _NSEL)
        zsum = jnp.ones((1, tt), F32)
        for r in range(1, PEER_TOPK):
            zsum = zsum + jnp.exp(best[r] - best[0])
        cut = 0.5 * (best[PEER_TOPK - 1] + best[PEER_TOPK])
        th_ref[hd] = cut - s1
        u1_ref[hd] = jnp.exp(s1 - a[0]) / zsum
        u2_ref[hd] = jnp.exp(s2 - b[0])
        return carry

    lax.fori_loop(0, PEER_HEADS, head, 0)


def _route(sc, tt=256):
    T = sc.shape[-1]
    blk = pl.BlockSpec((PEER_HEADS, PEER_NKEYS, tt), lambda i: (0, 0, i))
    shp = jax.ShapeDtypeStruct((PEER_HEADS, PEER_NKEYS, T), F32)
    return pl.pallas_call(
        _route_kernel,
        grid=(T // tt,),
        in_specs=[pl.BlockSpec((PEER_HEADS, 2, PEER_NKEYS, tt), lambda i: (0, 0, 0, i))],
        out_specs=[blk, blk, blk],
        out_shape=[shp, shp, shp],
        compiler_params=_cparams(("parallel",)),
        name="peer_route",
    )(sc)


_GELU_C1 = -2.0 * math.sqrt(2.0 / math.pi)
_GELU_C2 = _GELU_C1 * 0.044715
_PEER_FLAGS = None
_PEER_EB = 256
_PEER_TOK = 256
_PEER_LANES = 128


def _peer_kernel(nj, xn_ref, dn_ref, upt_ref, s2_ref, th_ref, u1_ref, u2_ref, h_ref, o_ref,
                 acc_ref, at0_ref, at1_ref, wg0_ref, wg1_ref):
    j = pl.program_id(1)
    tt = xn_ref.shape[0]
    te = dn_ref.shape[0]
    nb = te // _PEER_EB
    nh = tt // _PEER_TOK
    gpb = _PEER_EB // PEER_NKEYS
    at = (at0_ref, at1_ref)
    wg = (wg0_ref, wg1_ref)

    @pl.when(j == 0)
    def _():
        acc_ref[...] = jnp.zeros_like(acc_ref)

    def scores(k, at_ref):
        rows = pl.ds(pl.multiple_of(k * _PEER_EB, _PEER_EB), _PEER_EB)
        for hf in range(nh):
            at_ref[hf] = lax.dot_general(dn_ref[rows, :], xn_ref[hf * _PEER_TOK:(hf + 1) * _PEER_TOK, :],
                                         (((1,), (1,)), ((), ())), preferred_element_type=F32)

    def weigh(k, at_ref, wg_ref):
        for g in range(gpb):
            c = (j * nb + k) * gpb + g
            er = slice(g * PEER_NKEYS, (g + 1) * PEER_NKEYS)
            for hf in range(nh):
                tw = slice(hf * _PEER_TOK, (hf + 1) * _PEER_TOK)
                ths = [th_ref[hd, pl.ds(c, 1), tw] for hd in range(PEER_HEADS)]
                u1s = [u1_ref[hd, pl.ds(c, 1), tw] for hd in range(PEER_HEADS)]
                for lt in range(_PEER_TOK // _PEER_LANES):
                    ls = slice(lt * _PEER_LANES, (lt + 1) * _PEER_LANES)
                    tl = slice(hf * _PEER_TOK + lt * _PEER_LANES, hf * _PEER_TOK + (lt + 1) * _PEER_LANES)
                    w = None
                    for hd in range(PEER_HEADS):
                        t = jnp.where(s2_ref[hd, :, tl] >= ths[hd][:, ls], u2_ref[hd, :, tl], 0.0) * u1s[hd][:, ls]
                        w = t if w is None else w + t
                    a = at_ref[hf, er, ls]
                    gl = a / (1.0 + jnp.exp(a * (_GELU_C1 + _GELU_C2 * (a * a))))
                    wg_ref[hf, er, ls] = (w * gl).astype(BF16)

    def project(k, wg_ref):
        for hf in range(nh):
            tw = slice(hf * _PEER_TOK, (hf + 1) * _PEER_TOK)
            acc_ref[:, tw] += jnp.dot(upt_ref[k], wg_ref[hf], preferred_element_type=F32)

    scores(0, at[0])
    scores(1, at[1])
    weigh(0, at[0], wg[0])

    def body(m, carry):
        k = 2 * m + 1
        scores(k + 1, at[0])
        weigh(k, at[1], wg[1])
        project(k - 1, wg[0])
        scores(k + 2, at[1])
        weigh(k + 1, at[0], wg[0])
        project(k, wg[1])
        return carry

    lax.fori_loop(0, nb // 2 - 1, body, 0)
    weigh(nb - 1, at[1], wg[1])
    project(nb - 2, wg[0])
    project(nb - 1, wg[1])

    @pl.when(j == nj - 1)
    def _():
        o_ref[...] = h_ref[...] + acc_ref[...].T


def _peer(xn, down_b, up_t3, sc, th, u1, u2, h, tt=512, te=2048):
    T, D = xn.shape
    E = down_b.shape[0]
    nj = E // te
    nh = tt // _PEER_TOK
    rt = pl.BlockSpec((PEER_HEADS, PEER_NKEYS, tt), lambda i, j: (0, 0, i))
    return pl.pallas_call(
        functools.partial(_peer_kernel, nj),
        grid=(T // tt, nj),
        in_specs=[pl.BlockSpec((tt, D), lambda i, j: (i, 0)),
                  pl.BlockSpec((te, D), lambda i, j: (j, 0)),
                  pl.BlockSpec((te // _PEER_EB, D, _PEER_EB), lambda i, j: (j, 0, 0)),
                  pl.BlockSpec((PEER_HEADS, None, PEER_NKEYS, tt), lambda i, j: (0, 1, 0, i)),
                  rt, rt, rt,
                  pl.BlockSpec((tt, D), lambda i, j: (i, 0))],
        out_specs=pl.BlockSpec((tt, D), lambda i, j: (i, 0)),
        out_shape=jax.ShapeDtypeStruct((T, D), F32),
        scratch_shapes=[pltpu.VMEM((D, tt), F32),
                        pltpu.VMEM((nh, _PEER_EB, _PEER_TOK), F32), pltpu.VMEM((nh, _PEER_EB, _PEER_TOK), F32),
                        pltpu.VMEM((nh, _PEER_EB, _PEER_TOK), BF16), pltpu.VMEM((nh, _PEER_EB, _PEER_TOK), BF16)],
        compiler_params=_cparams(("parallel", "arbitrary"), flags=_PEER_FLAGS),
        name="peer_experts",
    )(xn, down_b, up_t3, sc, th, u1, u2, h)


def _transpose_cast_kernel(x_ref, o_ref):
    o_ref[0] = x_ref[...].T.astype(o_ref.dtype)


def _transpose_blocks_bf16(x, tr=_PEER_EB):
    R, C = x.shape
    return pl.pallas_call(
        _transpose_cast_kernel,
        grid=(R // tr,),
        in_specs=[pl.BlockSpec((tr, C), lambda i: (i, 0))],
        out_specs=pl.BlockSpec((1, C, tr), lambda i: (i, 0, 0)),
        out_shape=jax.ShapeDtypeStruct((R // tr, C, tr), BF16),
        compiler_params=_cparams(("parallel",)),
        name="transpose_cast",
    )(x)


def _rep(v, n):
    return jnp.repeat(v, n, axis=-1)


def _layer(x2, B, S, layer, p, bias_all):
    T, D = x2.shape
    n_plain = HG_COLS + AT_COLS + SSM_W + SSM_CONV_CH
    w_in = p["w_in"]
    w_cat = jnp.concatenate([w_in[:, :n_plain], _rep(w_in[:, n_plain:], HEAD_DIM)], axis=1).astype(BF16)
    uh, ua, us = _inproj(x2, p["attn_norm_w"], w_cat)

    mh = _hgrn(uh.reshape(B, S, HG_COLS), p["hgrn_lower_bounds"],
               jnp.tile(p["hgrn_norm_w"], HGRN_HEADS).reshape(1, HGRN_W), layer)
    qw = jnp.tile(p["q_norm_w"], ATTN_HEADS).reshape(1, ATTN_W)
    kw = jnp.tile(p["k_norm_w"], ATTN_HEADS).reshape(1, ATTN_W)
    ua3 = ua.reshape(B, S, AT_COLS)
    att = [_attn_branch(ua3, qw, kw, bias_all, br) for br in range(len(DILATED_PAIRS))]
    ms = _ssd(us.reshape(B, S, SS_COLS), p["ssm_conv_w"], p["ssm_conv_b"].reshape(1, -1),
              _rep(p["ssm_dt_bias"], HEAD_DIM).reshape(1, -1), _rep(p["ssm_a_log"], HEAD_DIM).reshape(1, -1),
              _rep(p["ssm_d"], HEAD_DIM).reshape(1, -1), p["ssm_norm_w"].reshape(1, -1))

    sk = p["peer_sub_keys"]
    zk = jnp.zeros_like(sk[0])
    sk_pad = jnp.stack([jnp.concatenate([sk[0], zk], axis=1), jnp.concatenate([zk, sk[1]], axis=1)])
    h, xn, sc = _outproj(x2, mh.reshape(T, HGRN_W), att[0], att[1], att[2], ms.reshape(T, SSM_W),
                         p["w_out"].astype(BF16), p["ffn_norm_w"], p["peer_w_query"].astype(BF16), sk_pad)
    th, u1, u2 = _route(sc)
    return _peer(xn, p["peer_down"].astype(BF16), _transpose_blocks_bf16(p["peer_up"]), sc, th, u1, u2, h)


_PER_LAYER = ("attn_norm_w", "w_in", "hgrn_norm_w", "q_norm_w", "k_norm_w", "ssm_conv_w", "ssm_conv_b",
              "ssm_dt_bias", "ssm_a_log", "ssm_d", "ssm_norm_w", "w_out", "ffn_norm_w", "peer_w_query",
              "peer_sub_keys", "peer_down", "peer_up")


def kernel(x, attn_norm_w, w_in, hgrn_lower_bounds, hgrn_norm_w, q_norm_w, k_norm_w, rel_bias, ssm_conv_w, ssm_conv_b, ssm_dt_bias, ssm_a_log, ssm_d, ssm_norm_w, w_out, ffn_norm_w, peer_w_query, peer_sub_keys, peer_down, peer_up):
    stacked = dict(attn_norm_w=attn_norm_w, w_in=w_in, hgrn_norm_w=hgrn_norm_w, q_norm_w=q_norm_w,
                   k_norm_w=k_norm_w, ssm_conv_w=ssm_conv_w, ssm_conv_b=ssm_conv_b, ssm_dt_bias=ssm_dt_bias,
                   ssm_a_log=ssm_a_log, ssm_d=ssm_d, ssm_norm_w=ssm_norm_w, w_out=w_out, ffn_norm_w=ffn_norm_w,
                   peer_w_query=peer_w_query, peer_sub_keys=peer_sub_keys, peer_down=peer_down, peer_up=peer_up)
    B, S, D = x.shape
    bias_all = _bias_tables(rel_bias)
    x2 = x.reshape(B * S, D)
    for layer in range(w_in.shape[0]):
        p = {k: stacked[k][layer] for k in _PER_LAYER}
        p["hgrn_lower_bounds"] = hgrn_lower_bounds
        x2 = _layer(x2, B, S, layer, p, bias_all)
    return x2.reshape(B, S, D)
```

```python
import functools
import math

import numpy as np
import jax
import jax.numpy as jnp
from jax import lax
from jax.experimental import pallas as pl
from jax.experimental.pallas import tpu as pltpu

F32 = jnp.float32
BF16 = jnp.bfloat16
HIGHEST = lax.Precision.HIGHEST
NEG = -1e30
EPS = 1e-6

HEAD_DIM = 64
CHUNK = 64
HGRN_HEADS = 4
HGRN_W = HGRN_HEADS * HEAD_DIM
ATTN_HEADS = 6
ATTN_W = ATTN_HEADS * HEAD_DIM
ATTN_BLOCK = 128
DILATED_PAIRS = ((128, 1), (512, 4), (2048, 16))
NUM_BUCKETS = 32
MAX_DISTANCE = 2048
SSM_HEADS = 6
SSM_W = SSM_HEADS * HEAD_DIM
SSM_GROUPS = 2
SSM_STATE = 128
SSM_GN = SSM_GROUPS * SSM_STATE
SSM_CONV = 4
SSM_CONV_CH = SSM_W + 2 * SSM_GN
PEER_HEADS = 8
PEER_NKEYS = 128
PEER_HALF = 64
PEER_TOPK = 16
HG_COLS = 4 * HGRN_W
AT_COLS = 3 * ATTN_W
SS_COLS = SSM_W + SSM_CONV_CH + SSM_W

VMEM_LIMIT = 56 * 1024 * 1024


def _cparams(sem, flags=None):
    return pltpu.CompilerParams(dimension_semantics=sem, vmem_limit_bytes=VMEM_LIMIT, flags=flags)


def _iota(shape, dim):
    return lax.broadcasted_iota(jnp.int32, shape, dim)


def _mm(a, b):
    return jnp.dot(a.astype(BF16), b.astype(BF16), preferred_element_type=F32)


def _mm_nt(a, b):
    return lax.dot_general(a.astype(BF16), b.astype(BF16), (((1,), (1,)), ((), ())),
                           preferred_element_type=F32)


def _mm_tn(a, b):
    return lax.dot_general(a.astype(BF16), b.astype(BF16), (((0,), (0,)), ((), ())),
                           preferred_element_type=F32)


def _mm_hi(a, b):
    return jnp.dot(a, b, precision=HIGHEST, preferred_element_type=F32)


def _sigmoid(x):
    return 1.0 / (1.0 + jnp.exp(-x))


def _silu(x):
    return x * _sigmoid(x)


def _softplus(x):
    return jnp.maximum(x, 0.0) + jnp.log(1.0 + jnp.exp(-jnp.abs(x)))


def _seg_ones(n, seg):
    r = _iota((n, n), 0)
    c = _iota((n, n), 1)
    same = None
    for s in range(n // seg):
        lo, hi = s * seg, (s + 1) * seg
        t = (r >= lo) & (r < hi) & (c >= lo) & (c < hi)
        same = t if same is None else (same | t)
    return jnp.where(same, 1.0, 0.0).astype(F32)


def _lane_seg_mask(width, seg, idx):
    l = _iota((1, width), 1)
    return jnp.where((l >= idx * seg) & (l < (idx + 1) * seg), 1.0, 0.0).astype(F32)


def _inproj_kernel(x_ref, nw_ref, w_ref, oh_ref, oa_ref, os_ref):
    x = x_ref[...]
    xn = x * lax.rsqrt(jnp.mean(x * x, axis=-1, keepdims=True) + EPS) * nw_ref[...]
    xb = xn.astype(BF16)
    c0, c1, c2 = HG_COLS, HG_COLS + AT_COLS, HG_COLS + AT_COLS + SSM_W + SSM_CONV_CH
    oh_ref[...] = jnp.dot(xb, w_ref[:, 0:c0], preferred_element_type=F32)
    oa_ref[...] = jnp.dot(xb, w_ref[:, c0:c1], preferred_element_type=F32)
    os_ref[:, 0:c2 - c1] = jnp.dot(xb, w_ref[:, c1:c2], preferred_element_type=F32)
    dt = jnp.dot(xb, w_ref[:, c2:], preferred_element_type=F32)
    lanes = w_ref.shape[1] - c2
    expand = jnp.where((_iota((lanes, SSM_W), 1) >> 6) == _iota((lanes, SSM_W), 0), 1.0, 0.0).astype(F32)
    os_ref[:, c2 - c1:] = _mm_hi(dt, expand)


def _inproj(x2, norm_w, w_pad, layer, tm=256):
    T, D = x2.shape
    ncol = w_pad.shape[2]
    return pl.pallas_call(
        _inproj_kernel,
        grid=(T // tm,),
        in_specs=[pl.BlockSpec((tm, D), lambda i: (i, 0)),
                  pl.BlockSpec((1, D), lambda i: (0, 0)),
                  pl.BlockSpec((None, D, ncol), lambda i: (layer, 0, 0))],
        out_specs=[pl.BlockSpec((tm, HG_COLS), lambda i: (i, 0)),
                   pl.BlockSpec((tm, AT_COLS), lambda i: (i, 0)),
                   pl.BlockSpec((tm, SS_COLS), lambda i: (i, 0))],
        out_shape=[jax.ShapeDtypeStruct((T, HG_COLS), F32),
                   jax.ShapeDtypeStruct((T, AT_COLS), F32),
                   jax.ShapeDtypeStruct((T, SS_COLS), F32)],
        compiler_params=_cparams(("parallel",)),
        name="inproj",
    )(x2, norm_w.reshape(1, D), w_pad)


def _hgrn_kernel(layer, nchunk, u_ref, lbs_ref, nw_ref, o_ref, st_ref):
    W = HGRN_W

    @pl.when(pl.program_id(1) == 0)
    def _():
        st_ref[...] = jnp.zeros_like(st_ref)

    lbs = lbs_ref[...]
    e = jnp.exp(lbs - jnp.max(lbs, axis=0, keepdims=True))
    sm = e / jnp.sum(e, axis=0, keepdims=True)
    lb = jnp.zeros((1, W), F32)
    for j in range(1, layer + 1):
        lb = lb + sm[j:j + 1]
    nw = nw_ref[...]

    tril = jnp.where(_iota((CHUNK, CHUNK), 0) >= _iota((CHUNK, CHUNK), 1), 1.0, 0.0).astype(F32)
    row = _iota((CHUNK, W), 0)
    hm = [_lane_seg_mask(W, HEAD_DIM, h) for h in range(HGRN_HEADS)]
    hm3 = [jnp.concatenate([m, m, m], axis=1) for m in hm]
    seg = _seg_ones(W, HEAD_DIM)
    seg_b = seg.astype(BF16)
    tt16 = _iota((16, W), 0)
    z32 = jnp.zeros((32, W), F32)
    in_q2 = (row & 16) != 0

    def chunk(c, carry):
        r0 = pl.multiple_of(c * CHUNK, CHUNK)
        u = u_ref[0, pl.ds(r0, CHUNK), :]
        q = u[:, 0:W] * (HEAD_DIM ** -0.5)
        f = u[:, W:2 * W]
        iv = u[:, 2 * W:3 * W]
        g = u[:, 3 * W:4 * W]
        forget = lb + (1.0 - lb) * _sigmoid(f)
        logf = jnp.log(forget)
        kk = 1.0 - forget
        b = _mm_hi(tril, logf)
        b16, b32, b48, bend = b[16:17], b[32:33], b[48:49], b[63:64]

        q1 = q * jnp.exp(jnp.where(row >= 32, b - b32, NEG))
        k1 = kk * jnp.exp(jnp.where(row < 32, b32 - b, NEG))
        ref2 = jnp.where(row < 32, b16, b48)
        q2 = q * jnp.exp(jnp.where(in_q2, b - ref2, NEG))
        k2 = kk * jnp.exp(jnp.where(in_q2, NEG, ref2 - b))
        qcat = jnp.concatenate([q1, jnp.concatenate([q2[:32], z32], 0), jnp.concatenate([z32, q2[32:]], 0)], 1)
        kcat = jnp.concatenate([k1, jnp.concatenate([k2[:32], z32], 0), jnp.concatenate([z32, k2[32:]], 0)], 1)
        qstack = jnp.concatenate([qcat * hm3[h] for h in range(HGRN_HEADS)], axis=0)
        a = _mm_nt(qstack, kcat)
        ofull = _mm(a, iv)
        o = ofull[0:CHUNK] * hm[0]
        for h in range(1, HGRN_HEADS):
            o = o + ofull[h * CHUNK:(h + 1) * CHUNK] * hm[h]

        st = st_ref[...]
        o = o + _mm_nt(q * jnp.exp(b), st)

        diag = []
        for j in range(4):
            bR = b[16 * j:16 * j + 16]
            qR = q[16 * j:16 * j + 16]
            ps = []
            for s in range(16):
                r = 16 * j + s
                ps.append(qR * kk[r:r + 1] * jnp.exp(jnp.where(tt16 >= s, bR - b[r:r + 1], NEG)))
            ab = jnp.dot(jnp.concatenate(ps, axis=0).astype(BF16), seg_b, preferred_element_type=F32)
            od = ab[0:16] * iv[16 * j:16 * j + 1]
            for s in range(1, 16):
                od = od + ab[16 * s:16 * s + 16] * iv[16 * j + s:16 * j + s + 1]
            diag.append(od)
        o = o + jnp.concatenate(diag, axis=0)

        kend = kk * jnp.exp(bend - b)
        st_ref[...] = st * jnp.exp(bend) + _mm_tn(iv, kend) * seg

        ss = _mm(o * o, seg_b) * (1.0 / HEAD_DIM)
        o_ref[0, pl.ds(r0, CHUNK), :] = o * lax.rsqrt(ss + EPS) * nw * _silu(g)
        return carry

    lax.fori_loop(0, nchunk, chunk, 0)


def _hgrn(uh, lbs, norm_w, layer, blk=256):
    B, S, _ = uh.shape
    return pl.pallas_call(
        functools.partial(_hgrn_kernel, layer, blk // CHUNK),
        grid=(B, S // blk),
        in_specs=[pl.BlockSpec((1, blk, HG_COLS), lambda b, s: (b, s, 0)),
                  pl.BlockSpec(lbs.shape, lambda b, s: (0, 0)),
                  pl.BlockSpec((1, HGRN_W), lambda b, s: (0, 0))],
        out_specs=pl.BlockSpec((1, blk, HGRN_W), lambda b, s: (b, s, 0)),
        out_shape=jax.ShapeDtypeStruct((B, S, HGRN_W), F32),
        scratch_shapes=[pltpu.VMEM((HGRN_W, HGRN_W), F32)],
        compiler_params=_cparams(("parallel", "arbitrary")),
        name="hgrn2",
    )(uh, lbs, norm_w)


def _ssd_kernel(nchunk, u_ref, cw_ref, cb_ref, dtb_ref, alog_ref, d_ref, nw_ref, o_ref,
                st_ref, ext_ref, xc_ref):
    W = SSM_W
    blk = nchunk * CHUNK
    first = pl.program_id(1) == 0

    @pl.when(first)
    def _():
        st_ref[...] = jnp.zeros_like(st_ref)
        ext_ref[0:8, :] = jnp.zeros((8, SSM_CONV_CH), F32)

    @pl.when(jnp.logical_not(first))
    def _():
        ext_ref[0:8, :] = ext_ref[blk:blk + 8, :]

    ext_ref[8:8 + blk, :] = u_ref[0, :, W:W + SSM_CONV_CH]
    cw = cw_ref[...]
    conv = cb_ref[...] + cw[0:1] * ext_ref[5:5 + blk, :]
    for j in range(1, SSM_CONV):
        conv = conv + cw[j:j + 1] * ext_ref[5 + j:5 + j + blk, :]
    xc_ref[...] = _silu(conv)

    dtb = dtb_ref[...]
    a_neg = -jnp.exp(alog_ref[...])
    dsk = d_ref[...]
    nw = nw_ref[...]
    tril = jnp.where(_iota((CHUNK, CHUNK), 0) >= _iota((CHUNK, CHUNK), 1), 1.0, 0.0).astype(F32)
    t_i = _iota((CHUNK, W), 0)
    s_i = _iota((CHUNK, W), 1) & (HEAD_DIM - 1)
    strict = jnp.where(t_i > s_i, 1.0, 0.0).astype(F32)
    lower = t_i >= s_i
    hm = [_lane_seg_mask(W, HEAD_DIM, h) for h in range(SSM_HEADS)]
    zB = jnp.zeros((CHUNK, SSM_STATE), F32)
    gr = _iota((SSM_GN, W), 0) >= SSM_STATE
    gc = _iota((SSM_GN, W), 1) >= (SSM_HEADS // SSM_GROUPS) * HEAD_DIM
    gmask = jnp.where(gr == gc, 1.0, 0.0).astype(F32)
    gw = W // SSM_GROUPS
    seg = _seg_ones(W, gw).astype(BF16)

    def chunk(c, carry):
        r0 = pl.multiple_of(c * CHUNK, CHUNK)
        z = u_ref[0, pl.ds(r0, CHUNK), 0:W]
        dtr = u_ref[0, pl.ds(r0, CHUNK), W + SSM_CONV_CH:W + SSM_CONV_CH + W]
        xc = xc_ref[pl.ds(r0, CHUNK), :]
        xs = xc[:, 0:W]
        bm = xc[:, W:W + SSM_GN]
        cm = xc[:, W + SSM_GN:W + 2 * SSM_GN]
        dt = _softplus(dtr + dtb)
        a = dt * a_neg
        dd = _mm_hi(tril, jnp.concatenate([a * strict, a], axis=1))
        dseg = dd[:, 0:W]
        cum = dd[:, W:2 * W]
        decay = jnp.exp(jnp.where(lower, dseg, NEG))
        b0 = jnp.concatenate([bm[:, 0:SSM_STATE], zB], axis=1)
        b1 = jnp.concatenate([zB, bm[:, SSM_STATE:]], axis=1)
        bstack = jnp.concatenate([b0, b0, b0, b1, b1, b1], axis=0)
        scores = _mm_nt(cm, bstack) * decay
        xdt = xs * dt
        xbd = jnp.concatenate([xdt * hm[h] for h in range(SSM_HEADS)], axis=0)
        y = _mm(scores, xbd)
        st = st_ref[...]
        y = y + _mm(cm, st) * jnp.exp(cum)
        cend = cum[CHUNK - 1:CHUNK]
        to_end = jnp.exp(cend - cum) * dt
        st_ref[...] = st * jnp.exp(cend) + _mm_tn(bm, to_end * xs) * gmask
        y = y + dsk * xs
        y = y * _silu(z)
        ss = _mm(y * y, seg) * (1.0 / gw)
        o_ref[0, pl.ds(r0, CHUNK), :] = y * lax.rsqrt(ss + EPS) * nw
        return carry

    lax.fori_loop(0, nchunk, chunk, 0)


def _ssd(us, conv_w, conv_b, dtb, alog, dsk, norm_w, blk=256):
    B, S, _ = us.shape
    vec = lambda n: pl.BlockSpec((1, n), lambda b, s: (0, 0))
    return pl.pallas_call(
        functools.partial(_ssd_kernel, blk // CHUNK),
        grid=(B, S // blk),
        in_specs=[pl.BlockSpec((1, blk, SS_COLS), lambda b, s: (b, s, 0)),
                  pl.BlockSpec((SSM_CONV, SSM_CONV_CH), lambda b, s: (0, 0)),
                  vec(SSM_CONV_CH), vec(SSM_W), vec(SSM_W), vec(SSM_W), vec(SSM_W)],
        out_specs=pl.BlockSpec((1, blk, SSM_W), lambda b, s: (b, s, 0)),
        out_shape=jax.ShapeDtypeStruct((B, S, SSM_W), F32),
        scratch_shapes=[pltpu.VMEM((SSM_GN, SSM_W), F32),
                        pltpu.VMEM((blk + 8, SSM_CONV_CH), F32),
                        pltpu.VMEM((blk, SSM_CONV_CH), F32)],
        compiler_params=_cparams(("parallel", "arbitrary")),
        name="ssd",
    )(us, conv_w, conv_b, dtb, alog, dsk, norm_w)


def _t5_bucket(dist):
    max_exact = NUM_BUCKETS // 2
    d = np.maximum(dist, 1).astype(np.float32)
    large = max_exact + (np.log(d / max_exact) / np.log(MAX_DISTANCE / max_exact)
                         * (NUM_BUCKETS - max_exact)).astype(np.int32)
    large = np.minimum(large, NUM_BUCKETS - 1)
    return np.where(dist < max_exact, dist, large).astype(np.int32)


def _bucket_tables():
    qi = np.arange(ATTN_BLOCK)[:, None]
    kj = np.arange(2 * ATTN_BLOCK)[None, :]
    delta = qi - kj + ATTN_BLOCK
    return np.stack([_t5_bucket(np.maximum(delta, 0) * dil) for _, dil in DILATED_PAIRS])


def _bias_kernel(rb_ref, bk_ref, o_ref):
    bk = bk_ref[0]
    for h in range(ATTN_HEADS):
        acc = jnp.zeros(bk.shape, F32)
        for b in range(NUM_BUCKETS):
            acc = jnp.where(bk == b, rb_ref[b, h], acc)
        o_ref[0, h] = acc


def _bias_tables(rel_bias):
    bk = jnp.asarray(_bucket_tables())
    nbr = len(DILATED_PAIRS)
    return pl.pallas_call(
        _bias_kernel,
        grid=(nbr,),
        in_specs=[pl.BlockSpec(memory_space=pltpu.SMEM),
                  pl.BlockSpec((1, ATTN_BLOCK, 2 * ATTN_BLOCK), lambda i: (i, 0, 0))],
        out_specs=pl.BlockSpec((1, ATTN_HEADS, ATTN_BLOCK, 2 * ATTN_BLOCK), lambda i: (i, 0, 0, 0)),
        out_shape=jax.ShapeDtypeStruct((nbr, ATTN_HEADS, ATTN_BLOCK, 2 * ATTN_BLOCK), F32),
        compiler_params=_cparams(("parallel",)),
        name="attn_bias",
    )(rel_bias, bk)


def _attn_kernel(n_back, cur_ref, prev_ref, qw_ref, kw_ref, bias_ref, o_ref):
    n = pl.program_id(2)
    W = ATTN_W
    cur = cur_ref[0]
    prev = prev_ref[0]
    seg = _seg_ones(W, HEAD_DIM).astype(BF16)

    def nrm(x, w):
        ss = _mm(x * x, seg) * (1.0 / HEAD_DIM)
        return x * lax.rsqrt(ss + EPS) * w

    kw = kw_ref[...]
    qn = nrm(cur[:, 0:W], qw_ref[...]) * (HEAD_DIM ** -0.5)
    kmat = jnp.concatenate([nrm(prev[:, W:2 * W], kw), nrm(cur[:, W:2 * W], kw)], axis=0)
    vmat = jnp.concatenate([prev[:, 2 * W:3 * W], cur[:, 2 * W:3 * W]], axis=0)

    qi = _iota((ATTN_BLOCK, 2 * ATTN_BLOCK), 0)
    kj = _iota((ATTN_BLOCK, 2 * ATTN_BLOCK), 1)
    delta = qi - kj + ATTN_BLOCK
    first_key = jnp.where(n > 0, 0, ATTN_BLOCK)
    valid = (delta >= 0) & (delta <= n_back) & (kj >= first_key)
    lane = _iota((ATTN_BLOCK, 2 * HEAD_DIM), 1)
    low = lane < HEAD_DIM
    half = [jnp.where(low[0:1], 1.0, 0.0).astype(F32), jnp.where(low[0:1], 0.0, 1.0).astype(F32)]

    outs, lses = [], []
    for pp in range(ATTN_HEADS // 2):
        sl = slice(2 * HEAD_DIM * pp, 2 * HEAD_DIM * (pp + 1))
        qp, kp, vp = qn[:, sl], kmat[:, sl], vmat[:, sl]
        oh, lh = [], []
        for hh in range(2):
            s = _mm_nt(qp * half[hh], kp) + bias_ref[0, 2 * pp + hh]
            s = jnp.where(valid, s, NEG)
            m = jnp.max(s, axis=-1, keepdims=True)
            p = jnp.exp(s - m)
            l = jnp.sum(p, axis=-1, keepdims=True)
            oh.append(_mm(p, vp) / l)
            lh.append(m + jnp.log(l))
        outs.append(jnp.where(low, oh[0], oh[1]))
        lses.append(jnp.where(low, lh[0], lh[1]))
    o_ref[0] = jnp.concatenate(outs + lses, axis=1)


def _attn_branch(ua, qw, kw, bias_all, branch):
    window, dil = DILATED_PAIRS[branch]
    B, S, _ = ua.shape
    L = S // dil
    nb = L // ATTN_BLOCK
    view = ua.reshape(B, L, dil * AT_COLS)
    out = pl.pallas_call(
        functools.partial(_attn_kernel, window // dil),
        grid=(B, dil, nb),
        in_specs=[pl.BlockSpec((1, ATTN_BLOCK, AT_COLS), lambda b, r, n: (b, n, r)),
                  pl.BlockSpec((1, ATTN_BLOCK, AT_COLS), lambda b, r, n: (b, jnp.maximum(n - 1, 0), r)),
                  pl.BlockSpec((1, ATTN_W), lambda b, r, n: (0, 0)),
                  pl.BlockSpec((1, ATTN_W), lambda b, r, n: (0, 0)),
                  pl.BlockSpec((1, ATTN_HEADS, ATTN_BLOCK, 2 * ATTN_BLOCK), lambda b, r, n: (branch, 0, 0, 0))],
        out_specs=pl.BlockSpec((1, ATTN_BLOCK, 2 * ATTN_W), lambda b, r, n: (b, n, r)),
        out_shape=jax.ShapeDtypeStruct((B, L, dil * 2 * ATTN_W), F32),
        compiler_params=_cparams(("parallel", "parallel", "arbitrary")),
        name=f"dilated_attn_{dil}",
    )(view, view, qw, kw, bias_all)
    return out.reshape(B * S, 2 * ATTN_W)


def _outproj_kernel(x_ref, mh_ref, a1_ref, a2_ref, a3_ref, ms_ref, wo_ref, fw_ref, wq_ref, sk_ref,
                    h_ref, xn_ref, sc_ref):
    W = ATTN_W
    br = [a1_ref[...], a2_ref[...], a3_ref[...]]
    lse = [a[:, W:2 * W] for a in br]
    m = jnp.maximum(jnp.maximum(lse[0], lse[1]), lse[2])
    e = [jnp.exp(l - m) for l in lse]
    oa = (e[0] * br[0][:, 0:W] + e[1] * br[1][:, 0:W] + e[2] * br[2][:, 0:W]) / (e[0] + e[1] + e[2])
    h = (x_ref[...] + _mm(mh_ref[...], wo_ref[0:HGRN_W, :])
         + _mm(oa, wo_ref[HGRN_W:HGRN_W + W, :])
         + _mm(ms_ref[...], wo_ref[HGRN_W + W:, :]))
    h_ref[...] = h
    xn = (h * lax.rsqrt(jnp.mean(h * h, axis=-1, keepdims=True) + EPS) * fw_ref[...]).astype(BF16)
    xn_ref[...] = xn
    qp = jnp.dot(xn, wq_ref[...], preferred_element_type=F32)
    for hd in range(PEER_HEADS):
        qh = qp[:, 2 * PEER_HALF * hd:2 * PEER_HALF * (hd + 1)]
        for c in range(2):
            sc_ref[hd, c] = _mm_nt(sk_ref[c], qh)


def _outproj(x2, mh, a1, a2, a3, ms, w_out, ffn_w, wq, sk_pad, tm=256):
    T, D = x2.shape
    row = lambda n: pl.BlockSpec((tm, n), lambda i: (i, 0))
    full = lambda a: pl.BlockSpec(a.shape, lambda i: (0,) * a.ndim)
    return pl.pallas_call(
        _outproj_kernel,
        grid=(T // tm,),
        in_specs=[row(D), row(HGRN_W), row(2 * ATTN_W), row(2 * ATTN_W), row(2 * ATTN_W), row(SSM_W),
                  full(w_out), pl.BlockSpec((1, D), lambda i: (0, 0)), full(wq), full(sk_pad)],
        out_specs=[row(D), row(D),
                   pl.BlockSpec((PEER_HEADS, 2, PEER_NKEYS, tm), lambda i: (0, 0, 0, i))],
        out_shape=[jax.ShapeDtypeStruct((T, D), F32), jax.ShapeDtypeStruct((T, D), BF16),
                   jax.ShapeDtypeStruct((PEER_HEADS, 2, PEER_NKEYS, T), F32)],
        compiler_params=_cparams(("parallel",)),
        name="outproj_query",
    )(x2, mh, a1, a2, a3, ms, w_out, ffn_w.reshape(1, D), wq, sk_pad)


_NSEL = PEER_TOPK + 1
_PAIRS = [(i, j) for i in range(_NSEL) for j in range(_NSEL) if (i + 1) * (j + 1) <= _NSEL]


def _top_values(v, k):
    out = []
    for _ in range(k):
        m = jnp.max(v, axis=0, keepdims=True)
        out.append(m)
        v = jnp.where(v == m, NEG, v)
    return out


def _route_kernel(sc_ref, th_ref, u1_ref, u2_ref):
    tt = sc_ref.shape[-1]
    row8 = _iota((8, tt), 0)

    def head(hd, carry):
        s1 = sc_ref[hd, 0]
        s2 = sc_ref[hd, 1]
        a = _top_values(s1, _NSEL)
        b = _top_values(s2, _NSEL)
        groups = []
        for g0 in range(0, len(_PAIRS), 8):
            cg = jnp.full((8, tt), NEG, F32)
            for p, (i, j) in enumerate(_PAIRS[g0:g0 + 8]):
                cg = jnp.where(row8 == p, a[i] + b[j], cg)
            groups.append(cg)
        best = _top_values(jnp.concatenate(groups, axis=0), _NSEL)
        zsum = jnp.ones((1, tt), F32)
        for r in range(1, PEER_TOPK):
            zsum = zsum + jnp.exp(best[r] - best[0])
        cut = 0.5 * (best[PEER_TOPK - 1] + best[PEER_TOPK])
        th_ref[hd] = cut - s1
        u1_ref[hd] = jnp.exp(s1 - a[0]) / zsum
        u2_ref[hd] = jnp.exp(s2 - b[0])
        return carry

    lax.fori_loop(0, PEER_HEADS, head, 0)


def _route(sc, tt=256):
    T = sc.shape[-1]
    blk = pl.BlockSpec((PEER_HEADS, PEER_NKEYS, tt), lambda i: (0, 0, i))
    shp = jax.ShapeDtypeStruct((PEER_HEADS, PEER_NKEYS, T), F32)
    return pl.pallas_call(
        _route_kernel,
        grid=(T // tt,),
        in_specs=[pl.BlockSpec((PEER_HEADS, 2, PEER_NKEYS, tt), lambda i: (0, 0, 0, i))],
        out_specs=[blk, blk, blk],
        out_shape=[shp, shp, shp],
        compiler_params=_cparams(("parallel",)),
        name="peer_route",
    )(sc)


_GELU_C1 = -2.0 * math.sqrt(2.0 / math.pi)
_GELU_C2 = _GELU_C1 * 0.044715
_PEER_FLAGS = None
_PEER_EB = 256
_PEER_TOK = 256
_PEER_LANES = 128


def _peer_kernel(nj, xn_ref, dn_ref, upt_ref, s2_ref, th_ref, u1_ref, u2_ref, h_ref, o_ref,
                 acc_ref, at0_ref, at1_ref, wg0_ref, wg1_ref):
    j = pl.program_id(1)
    tt = xn_ref.shape[0]
    te = dn_ref.shape[0]
    nb = te // _PEER_EB
    nh = tt // _PEER_TOK
    gpb = _PEER_EB // PEER_NKEYS
    at = (at0_ref, at1_ref)
    wg = (wg0_ref, wg1_ref)

    @pl.when(j == 0)
    def _():
        acc_ref[...] = jnp.zeros_like(acc_ref)

    def scores(k, at_ref):
        rows = pl.ds(pl.multiple_of(k * _PEER_EB, _PEER_EB), _PEER_EB)
        for hf in range(nh):
            at_ref[hf] = lax.dot_general(dn_ref[rows, :], xn_ref[hf * _PEER_TOK:(hf + 1) * _PEER_TOK, :],
                                         (((1,), (1,)), ((), ())), preferred_element_type=F32)

    def weigh(k, at_ref, wg_ref):
        for g in range(gpb):
            c = (j * nb + k) * gpb + g
            er = slice(g * PEER_NKEYS, (g + 1) * PEER_NKEYS)
            for hf in range(nh):
                tw = slice(hf * _PEER_TOK, (hf + 1) * _PEER_TOK)
                ths = [th_ref[hd, pl.ds(c, 1), tw] for hd in range(PEER_HEADS)]
                u1s = [u1_ref[hd, pl.ds(c, 1), tw] for hd in range(PEER_HEADS)]
                for lt in range(_PEER_TOK // _PEER_LANES):
                    ls = slice(lt * _PEER_LANES, (lt + 1) * _PEER_LANES)
                    tl = slice(hf * _PEER_TOK + lt * _PEER_LANES, hf * _PEER_TOK + (lt + 1) * _PEER_LANES)
                    w = None
                    for hd in range(PEER_HEADS):
                        t = jnp.where(s2_ref[hd, :, tl] >= ths[hd][:, ls], u2_ref[hd, :, tl], 0.0) * u1s[hd][:, ls]
                        w = t if w is None else w + t
                    a = at_ref[hf, er, ls]
                    gl = a / (1.0 + jnp.exp(a * (_GELU_C1 + _GELU_C2 * (a * a))))
                    wg_ref[hf, er, ls] = (w * gl).astype(BF16)

    def project(k, wg_ref):
        for hf in range(nh):
            tw = slice(hf * _PEER_TOK, (hf + 1) * _PEER_TOK)
            acc_ref[:, tw] += jnp.dot(upt_ref[k], wg_ref[hf], preferred_element_type=F32)

    scores(0, at[0])
    scores(1, at[1])
    weigh(0, at[0], wg[0])

    def body(m, carry):
        k = 2 * m + 1
        scores(k + 1, at[0])
        weigh(k, at[1], wg[1])
        project(k - 1, wg[0])
        scores(k + 2, at[1])
        weigh(k + 1, at[0], wg[0])
        project(k, wg[1])
        return carry

    lax.fori_loop(0, nb // 2 - 1, body, 0)
    weigh(nb - 1, at[1], wg[1])
    project(nb - 2, wg[0])
    project(nb - 1, wg[1])

    @pl.when(j == nj - 1)
    def _():
        o_ref[...] = h_ref[...] + acc_ref[...].T


def _peer(xn, down_b, layer, up_t3, sc, th, u1, u2, h, tt=512, te=2048):
    T, D = xn.shape
    E = down_b.shape[1]
    nj = E // te
    nh = tt // _PEER_TOK
    rt = pl.BlockSpec((PEER_HEADS, PEER_NKEYS, tt), lambda i, j: (0, 0, i))
    return pl.pallas_call(
        functools.partial(_peer_kernel, nj),
        grid=(T // tt, nj),
        in_specs=[pl.BlockSpec((tt, D), lambda i, j: (i, 0)),
                  pl.BlockSpec((None, te, D), lambda i, j: (layer, j, 0)),
                  pl.BlockSpec((te // _PEER_EB, D, _PEER_EB), lambda i, j: (j, 0, 0)),
                  pl.BlockSpec((PEER_HEADS, None, PEER_NKEYS, tt), lambda i, j: (0, 1, 0, i)),
                  rt, rt, rt,
                  pl.BlockSpec((tt, D), lambda i, j: (i, 0))],
        out_specs=pl.BlockSpec((tt, D), lambda i, j: (i, 0)),
        out_shape=jax.ShapeDtypeStruct((T, D), F32),
        scratch_shapes=[pltpu.VMEM((D, tt), F32),
                        pltpu.VMEM((nh, _PEER_EB, _PEER_TOK), F32), pltpu.VMEM((nh, _PEER_EB, _PEER_TOK), F32),
                        pltpu.VMEM((nh, _PEER_EB, _PEER_TOK), BF16), pltpu.VMEM((nh, _PEER_EB, _PEER_TOK), BF16)],
        compiler_params=_cparams(("parallel", "arbitrary"), flags=_PEER_FLAGS),
        name="peer_experts",
    )(xn, down_b, up_t3, sc, th, u1, u2, h)


def _transpose_cast_kernel(x_ref, o_ref):
    o_ref[0] = x_ref[...].T.astype(o_ref.dtype)


def _transpose_blocks_bf16(x, layer, tr=_PEER_EB):
    _, R, C = x.shape
    return pl.pallas_call(
        _transpose_cast_kernel,
        grid=(R // tr,),
        in_specs=[pl.BlockSpec((None, tr, C), lambda i: (layer, i, 0))],
        out_specs=pl.BlockSpec((1, C, tr), lambda i: (i, 0, 0)),
        out_shape=jax.ShapeDtypeStruct((R // tr, C, tr), BF16),
        compiler_params=_cparams(("parallel",)),
        name="transpose_cast",
    )(x)


def _rep(v, n):
    return jnp.repeat(v, n, axis=-1)


def _layer(x2, B, S, layer, p, bias_all):
    T, D = x2.shape
    uh, ua, us = _inproj(x2, p["attn_norm_w"], p["w_in_pad"], layer)

    mh = _hgrn(uh.reshape(B, S, HG_COLS), p["hgrn_lower_bounds"],
               jnp.tile(p["hgrn_norm_w"], HGRN_HEADS).reshape(1, HGRN_W), layer)
    qw = jnp.tile(p["q_norm_w"], ATTN_HEADS).reshape(1, ATTN_W)
    kw = jnp.tile(p["k_norm_w"], ATTN_HEADS).reshape(1, ATTN_W)
    ua3 = ua.reshape(B, S, AT_COLS)
    att = [_attn_branch(ua3, qw, kw, bias_all, br) for br in range(len(DILATED_PAIRS))]
    ms = _ssd(us.reshape(B, S, SS_COLS), p["ssm_conv_w"], p["ssm_conv_b"].reshape(1, -1),
              _rep(p["ssm_dt_bias"], HEAD_DIM).reshape(1, -1), _rep(p["ssm_a_log"], HEAD_DIM).reshape(1, -1),
              _rep(p["ssm_d"], HEAD_DIM).reshape(1, -1), p["ssm_norm_w"].reshape(1, -1))

    sk = p["peer_sub_keys"]
    zk = jnp.zeros_like(sk[0])
    sk_pad = jnp.stack([jnp.concatenate([sk[0], zk], axis=1), jnp.concatenate([zk, sk[1]], axis=1)])
    h, xn, sc = _outproj(x2, mh.reshape(T, HGRN_W), att[0], att[1], att[2], ms.reshape(T, SSM_W),
                         p["w_out"].astype(BF16), p["ffn_norm_w"], p["peer_w_query"].astype(BF16), sk_pad)
    th, u1, u2 = _route(sc)
    return _peer(xn, p["peer_down_b"], layer, _transpose_blocks_bf16(p["peer_up"], layer), sc, th, u1, u2, h)


_PER_LAYER = ("attn_norm_w", "hgrn_norm_w", "q_norm_w", "k_norm_w", "ssm_conv_w", "ssm_conv_b",
              "ssm_dt_bias", "ssm_a_log", "ssm_d", "ssm_norm_w", "w_out", "ffn_norm_w", "peer_w_query",
              "peer_sub_keys")


def kernel(x, attn_norm_w, w_in, hgrn_lower_bounds, hgrn_norm_w, q_norm_w, k_norm_w, rel_bias, ssm_conv_w, ssm_conv_b, ssm_dt_bias, ssm_a_log, ssm_d, ssm_norm_w, w_out, ffn_norm_w, peer_w_query, peer_sub_keys, peer_down, peer_up):
    stacked = dict(attn_norm_w=attn_norm_w, w_in=w_in, hgrn_norm_w=hgrn_norm_w, q_norm_w=q_norm_w,
                   k_norm_w=k_norm_w, ssm_conv_w=ssm_conv_w, ssm_conv_b=ssm_conv_b, ssm_dt_bias=ssm_dt_bias,
                   ssm_a_log=ssm_a_log, ssm_d=ssm_d, ssm_norm_w=ssm_norm_w, w_out=w_out, ffn_norm_w=ffn_norm_w,
                   peer_w_query=peer_w_query, peer_sub_keys=peer_sub_keys, peer_down=peer_down, peer_up=peer_up)
    B, S, D = x.shape
    bias_all = _bias_tables(rel_bias)
    down_b = peer_down.astype(BF16)
    lane_pad = -w_in.shape[2] % 128
    w_in_pad = jnp.pad(w_in, ((0, 0), (0, 0), (0, lane_pad))).astype(BF16)
    x2 = x.reshape(B * S, D)
    for layer in range(w_in.shape[0]):
        p = {k: stacked[k][layer] for k in _PER_LAYER}
        p["w_in_pad"] = w_in_pad
        p["hgrn_lower_bounds"] = hgrn_lower_bounds
        p["peer_down_b"] = down_b
        p["peer_up"] = peer_up
        x2 = _layer(x2, B, S, layer, p, bias_all)
    return x2.reshape(B, S, D)
```

```python
import functools
import math

import numpy as np
import jax
import jax.numpy as jnp
from jax import lax
from jax.experimental import pallas as pl
from jax.experimental.pallas import tpu as pltpu

F32 = jnp.float32
BF16 = jnp.bfloat16
HIGHEST = lax.Precision.HIGHEST
NEG = -1e30
EPS = 1e-6

HEAD_DIM = 64
CHUNK = 64
HGRN_HEADS = 4
HGRN_W = HGRN_HEADS * HEAD_DIM
ATTN_HEADS = 6
ATTN_W = ATTN_HEADS * HEAD_DIM
ATTN_BLOCK = 128
DILATED_PAIRS = ((128, 1), (512, 4), (2048, 16))
NUM_BUCKETS = 32
MAX_DISTANCE = 2048
SSM_HEADS = 6
SSM_W = SSM_HEADS * HEAD_DIM
SSM_GROUPS = 2
SSM_STATE = 128
SSM_GN = SSM_GROUPS * SSM_STATE
SSM_CONV = 4
SSM_CONV_CH = SSM_W + 2 * SSM_GN
PEER_HEADS = 8
PEER_NKEYS = 128
PEER_HALF = 64
PEER_TOPK = 16
HG_COLS = 4 * HGRN_W
AT_COLS = 3 * ATTN_W
SS_COLS = SSM_W + SSM_CONV_CH + SSM_W

VMEM_LIMIT = 56 * 1024 * 1024


def _cparams(sem, flags=None):
    return pltpu.CompilerParams(dimension_semantics=sem, vmem_limit_bytes=VMEM_LIMIT, flags=flags)


def _iota(shape, dim):
    return lax.broadcasted_iota(jnp.int32, shape, dim)


def _mm(a, b):
    return jnp.dot(a.astype(BF16), b.astype(BF16), preferred_element_type=F32)


def _mm_nt(a, b):
    return lax.dot_general(a.astype(BF16), b.astype(BF16), (((1,), (1,)), ((), ())),
                           preferred_element_type=F32)


def _mm_tn(a, b):
    return lax.dot_general(a.astype(BF16), b.astype(BF16), (((0,), (0,)), ((), ())),
                           preferred_element_type=F32)


def _mm_hi(a, b):
    return jnp.dot(a, b, precision=HIGHEST, preferred_element_type=F32)


def _sigmoid(x):
    return 1.0 / (1.0 + jnp.exp(-x))


def _silu(x):
    return x * _sigmoid(x)


def _softplus(x):
    return jnp.maximum(x, 0.0) + jnp.log(1.0 + jnp.exp(-jnp.abs(x)))


def _seg_ones(n, seg):
    r = _iota((n, n), 0)
    c = _iota((n, n), 1)
    same = None
    for s in range(n // seg):
        lo, hi = s * seg, (s + 1) * seg
        t = (r >= lo) & (r < hi) & (c >= lo) & (c < hi)
        same = t if same is None else (same | t)
    return jnp.where(same, 1.0, 0.0).astype(F32)


def _lane_seg_mask(width, seg, idx):
    l = _iota((1, width), 1)
    return jnp.where((l >= idx * seg) & (l < (idx + 1) * seg), 1.0, 0.0).astype(F32)


def _inproj_kernel(x_ref, nw_ref, w_ref, oh_ref, oa_ref, os_ref):
    x = x_ref[...]
    xn = x * lax.rsqrt(jnp.mean(x * x, axis=-1, keepdims=True) + EPS) * nw_ref[...]
    xb = xn.astype(BF16)
    c0, c1, c2 = HG_COLS, HG_COLS + AT_COLS, HG_COLS + AT_COLS + SSM_W + SSM_CONV_CH
    oh_ref[...] = jnp.dot(xb, w_ref[:, 0:c0], preferred_element_type=F32)
    oa_ref[...] = jnp.dot(xb, w_ref[:, c0:c1], preferred_element_type=F32)
    os_ref[:, 0:c2 - c1] = jnp.dot(xb, w_ref[:, c1:c2], preferred_element_type=F32)
    dt = jnp.dot(xb, w_ref[:, c2:], preferred_element_type=F32)
    lanes = w_ref.shape[1] - c2
    expand = jnp.where((_iota((lanes, SSM_W), 1) >> 6) == _iota((lanes, SSM_W), 0), 1.0, 0.0).astype(F32)
    os_ref[:, c2 - c1:] = _mm_hi(dt, expand)


def _inproj(x2, norm_w, w_pad, layer, tm=256):
    T, D = x2.shape
    ncol = w_pad.shape[2]
    return pl.pallas_call(
        _inproj_kernel,
        grid=(T // tm,),
        in_specs=[pl.BlockSpec((tm, D), lambda i: (i, 0)),
                  pl.BlockSpec((1, D), lambda i: (0, 0)),
                  pl.BlockSpec((None, D, ncol), lambda i: (layer, 0, 0))],
        out_specs=[pl.BlockSpec((tm, HG_COLS), lambda i: (i, 0)),
                   pl.BlockSpec((tm, AT_COLS), lambda i: (i, 0)),
                   pl.BlockSpec((tm, SS_COLS), lambda i: (i, 0))],
        out_shape=[jax.ShapeDtypeStruct((T, HG_COLS), F32),
                   jax.ShapeDtypeStruct((T, AT_COLS), F32),
                   jax.ShapeDtypeStruct((T, SS_COLS), F32)],
        compiler_params=_cparams(("parallel",)),
        name="inproj",
    )(x2, norm_w.reshape(1, D), w_pad)


def _hgrn_kernel(layer, nchunk, u_ref, lbs_ref, nw_ref, o_ref, st_ref):
    W = HGRN_W

    @pl.when(pl.program_id(1) == 0)
    def _():
        st_ref[...] = jnp.zeros_like(st_ref)

    lbs = lbs_ref[...]
    e = jnp.exp(lbs - jnp.max(lbs, axis=0, keepdims=True))
    sm = e / jnp.sum(e, axis=0, keepdims=True)
    lb = jnp.zeros((1, W), F32)
    for j in range(1, layer + 1):
        lb = lb + sm[j:j + 1]
    nw = nw_ref[...]

    tril = jnp.where(_iota((CHUNK, CHUNK), 0) >= _iota((CHUNK, CHUNK), 1), 1.0, 0.0).astype(F32)
    row = _iota((CHUNK, W), 0)
    hm = [_lane_seg_mask(W, HEAD_DIM, h) for h in range(HGRN_HEADS)]
    hm3 = [jnp.concatenate([m, m, m], axis=1) for m in hm]
    seg = _seg_ones(W, HEAD_DIM)
    seg_b = seg.astype(BF16)
    tt16 = _iota((16, W), 0)
    z32 = jnp.zeros((32, W), F32)
    in_q2 = (row & 16) != 0

    def chunk(c, carry):
        r0 = pl.multiple_of(c * CHUNK, CHUNK)
        u = u_ref[0, pl.ds(r0, CHUNK), :]
        q = u[:, 0:W] * (HEAD_DIM ** -0.5)
        f = u[:, W:2 * W]
        iv = u[:, 2 * W:3 * W]
        g = u[:, 3 * W:4 * W]
        forget = lb + (1.0 - lb) * _sigmoid(f)
        logf = jnp.log(forget)
        kk = 1.0 - forget
        b = _mm_hi(tril, logf)
        b16, b32, b48, bend = b[16:17], b[32:33], b[48:49], b[63:64]

        q1 = q * jnp.exp(jnp.where(row >= 32, b - b32, NEG))
        k1 = kk * jnp.exp(jnp.where(row < 32, b32 - b, NEG))
        ref2 = jnp.where(row < 32, b16, b48)
        q2 = q * jnp.exp(jnp.where(in_q2, b - ref2, NEG))
        k2 = kk * jnp.exp(jnp.where(in_q2, NEG, ref2 - b))
        qcat = jnp.concatenate([q1, jnp.concatenate([q2[:32], z32], 0), jnp.concatenate([z32, q2[32:]], 0)], 1)
        kcat = jnp.concatenate([k1, jnp.concatenate([k2[:32], z32], 0), jnp.concatenate([z32, k2[32:]], 0)], 1)
        qstack = jnp.concatenate([qcat * hm3[h] for h in range(HGRN_HEADS)], axis=0)
        a = _mm_nt(qstack, kcat)
        ofull = _mm(a, iv)
        o = ofull[0:CHUNK] * hm[0]
        for h in range(1, HGRN_HEADS):
            o = o + ofull[h * CHUNK:(h + 1) * CHUNK] * hm[h]

        st = st_ref[...]
        o = o + _mm_nt(q * jnp.exp(b), st)

        diag = []
        for j in range(4):
            bR = b[16 * j:16 * j + 16]
            qR = q[16 * j:16 * j + 16]
            ps = []
            for s in range(16):
                r = 16 * j + s
                ps.append(qR * kk[r:r + 1] * jnp.exp(jnp.where(tt16 >= s, bR - b[r:r + 1], NEG)))
            ab = jnp.dot(jnp.concatenate(ps, axis=0).astype(BF16), seg_b, preferred_element_type=F32)
            od = ab[0:16] * iv[16 * j:16 * j + 1]
            for s in range(1, 16):
                od = od + ab[16 * s:16 * s + 16] * iv[16 * j + s:16 * j + s + 1]
            diag.append(od)
        o = o + jnp.concatenate(diag, axis=0)

        kend = kk * jnp.exp(bend - b)
        st_ref[...] = st * jnp.exp(bend) + _mm_tn(iv, kend) * seg

        ss = _mm(o * o, seg_b) * (1.0 / HEAD_DIM)
        o_ref[0, pl.ds(r0, CHUNK), :] = o * lax.rsqrt(ss + EPS) * nw * _silu(g)
        return carry

    lax.fori_loop(0, nchunk, chunk, 0)


def _hgrn(uh, lbs, norm_w, layer, blk=256):
    B, S, _ = uh.shape
    return pl.pallas_call(
        functools.partial(_hgrn_kernel, layer, blk // CHUNK),
        grid=(B, S // blk),
        in_specs=[pl.BlockSpec((1, blk, HG_COLS), lambda b, s: (b, s, 0)),
                  pl.BlockSpec(lbs.shape, lambda b, s: (0, 0)),
                  pl.BlockSpec((1, HGRN_W), lambda b, s: (0, 0))],
        out_specs=pl.BlockSpec((1, blk, HGRN_W), lambda b, s: (b, s, 0)),
        out_shape=jax.ShapeDtypeStruct((B, S, HGRN_W), F32),
        scratch_shapes=[pltpu.VMEM((HGRN_W, HGRN_W), F32)],
        compiler_params=_cparams(("parallel", "arbitrary")),
        name="hgrn2",
    )(uh, lbs, norm_w)


def _ssd_kernel(nchunk, u_ref, cw_ref, cb_ref, dtb_ref, alog_ref, d_ref, nw_ref, o_ref,
                st_ref, ext_ref, xc_ref):
    W = SSM_W
    blk = nchunk * CHUNK
    first = pl.program_id(1) == 0

    @pl.when(first)
    def _():
        st_ref[...] = jnp.zeros_like(st_ref)
        ext_ref[0:8, :] = jnp.zeros((8, SSM_CONV_CH), F32)

    @pl.when(jnp.logical_not(first))
    def _():
        ext_ref[0:8, :] = ext_ref[blk:blk + 8, :]

    ext_ref[8:8 + blk, :] = u_ref[0, :, W:W + SSM_CONV_CH]
    cw = cw_ref[...]
    conv = cb_ref[...] + cw[0:1] * ext_ref[5:5 + blk, :]
    for j in range(1, SSM_CONV):
        conv = conv + cw[j:j + 1] * ext_ref[5 + j:5 + j + blk, :]
    xc_ref[...] = _silu(conv)

    dtb = dtb_ref[...]
    a_neg = -jnp.exp(alog_ref[...])
    dsk = d_ref[...]
    nw = nw_ref[...]
    tril = jnp.where(_iota((CHUNK, CHUNK), 0) >= _iota((CHUNK, CHUNK), 1), 1.0, 0.0).astype(F32)
    t_i = _iota((CHUNK, W), 0)
    s_i = _iota((CHUNK, W), 1) & (HEAD_DIM - 1)
    strict = jnp.where(t_i > s_i, 1.0, 0.0).astype(F32)
    lower = t_i >= s_i
    hm = [_lane_seg_mask(W, HEAD_DIM, h) for h in range(SSM_HEADS)]
    zB = jnp.zeros((CHUNK, SSM_STATE), F32)
    gr = _iota((SSM_GN, W), 0) >= SSM_STATE
    gc = _iota((SSM_GN, W), 1) >= (SSM_HEADS // SSM_GROUPS) * HEAD_DIM
    gmask = jnp.where(gr == gc, 1.0, 0.0).astype(F32)
    gw = W // SSM_GROUPS
    seg = _seg_ones(W, gw).astype(BF16)

    def chunk(c, carry):
        r0 = pl.multiple_of(c * CHUNK, CHUNK)
        z = u_ref[0, pl.ds(r0, CHUNK), 0:W]
        dtr = u_ref[0, pl.ds(r0, CHUNK), W + SSM_CONV_CH:W + SSM_CONV_CH + W]
        xc = xc_ref[pl.ds(r0, CHUNK), :]
        xs = xc[:, 0:W]
        bm = xc[:, W:W + SSM_GN]
        cm = xc[:, W + SSM_GN:W + 2 * SSM_GN]
        dt = _softplus(dtr + dtb)
        a = dt * a_neg
        dd = _mm_hi(tril, jnp.concatenate([a * strict, a], axis=1))
        dseg = dd[:, 0:W]
        cum = dd[:, W:2 * W]
        decay = jnp.exp(jnp.where(lower, dseg, NEG))
        b0 = jnp.concatenate([bm[:, 0:SSM_STATE], zB], axis=1)
        b1 = jnp.concatenate([zB, bm[:, SSM_STATE:]], axis=1)
        bstack = jnp.concatenate([b0, b0, b0, b1, b1, b1], axis=0)
        scores = _mm_nt(cm, bstack) * decay
        xdt = xs * dt
        xbd = jnp.concatenate([xdt * hm[h] for h in range(SSM_HEADS)], axis=0)
        y = _mm(scores, xbd)
        st = st_ref[...]
        y = y + _mm(cm, st) * jnp.exp(cum)
        cend = cum[CHUNK - 1:CHUNK]
        to_end = jnp.exp(cend - cum) * dt
        st_ref[...] = st * jnp.exp(cend) + _mm_tn(bm, to_end * xs) * gmask
        y = y + dsk * xs
        y = y * _silu(z)
        ss = _mm(y * y, seg) * (1.0 / gw)
        o_ref[0, pl.ds(r0, CHUNK), :] = y * lax.rsqrt(ss + EPS) * nw
        return carry

    lax.fori_loop(0, nchunk, chunk, 0)


def _ssd(us, conv_w, conv_b, dtb, alog, dsk, norm_w, blk=256):
    B, S, _ = us.shape
    vec = lambda n: pl.BlockSpec((1, n), lambda b, s: (0, 0))
    return pl.pallas_call(
        functools.partial(_ssd_kernel, blk // CHUNK),
        grid=(B, S // blk),
        in_specs=[pl.BlockSpec((1, blk, SS_COLS), lambda b, s: (b, s, 0)),
                  pl.BlockSpec((SSM_CONV, SSM_CONV_CH), lambda b, s: (0, 0)),
                  vec(SSM_CONV_CH), vec(SSM_W), vec(SSM_W), vec(SSM_W), vec(SSM_W)],
        out_specs=pl.BlockSpec((1, blk, SSM_W), lambda b, s: (b, s, 0)),
        out_shape=jax.ShapeDtypeStruct((B, S, SSM_W), F32),
        scratch_shapes=[pltpu.VMEM((SSM_GN, SSM_W), F32),
                        pltpu.VMEM((blk + 8, SSM_CONV_CH), F32),
                        pltpu.VMEM((blk, SSM_CONV_CH), F32)],
        compiler_params=_cparams(("parallel", "arbitrary")),
        name="ssd",
    )(us, conv_w, conv_b, dtb, alog, dsk, norm_w)


def _t5_bucket(dist):
    max_exact = NUM_BUCKETS // 2
    d = np.maximum(dist, 1).astype(np.float32)
    large = max_exact + (np.log(d / max_exact) / np.log(MAX_DISTANCE / max_exact)
                         * (NUM_BUCKETS - max_exact)).astype(np.int32)
    large = np.minimum(large, NUM_BUCKETS - 1)
    return np.where(dist < max_exact, dist, large).astype(np.int32)


def _bucket_tables():
    qi = np.arange(ATTN_BLOCK)[:, None]
    kj = np.arange(2 * ATTN_BLOCK)[None, :]
    delta = qi - kj + ATTN_BLOCK
    return np.stack([_t5_bucket(np.maximum(delta, 0) * dil) for _, dil in DILATED_PAIRS])


def _bias_kernel(rb_ref, bk_ref, o_ref):
    bk = bk_ref[0]
    for h in range(ATTN_HEADS):
        acc = jnp.zeros(bk.shape, F32)
        for b in range(NUM_BUCKETS):
            acc = jnp.where(bk == b, rb_ref[b, h], acc)
        o_ref[0, h] = acc


def _bias_tables(rel_bias):
    bk = jnp.asarray(_bucket_tables())
    nbr = len(DILATED_PAIRS)
    return pl.pallas_call(
        _bias_kernel,
        grid=(nbr,),
        in_specs=[pl.BlockSpec(memory_space=pltpu.SMEM),
                  pl.BlockSpec((1, ATTN_BLOCK, 2 * ATTN_BLOCK), lambda i: (i, 0, 0))],
        out_specs=pl.BlockSpec((1, ATTN_HEADS, ATTN_BLOCK, 2 * ATTN_BLOCK), lambda i: (i, 0, 0, 0)),
        out_shape=jax.ShapeDtypeStruct((nbr, ATTN_HEADS, ATTN_BLOCK, 2 * ATTN_BLOCK), F32),
        compiler_params=_cparams(("parallel",)),
        name="attn_bias",
    )(rel_bias, bk)


def _attn_kernel(ua_ref, qw_ref, kw_ref, bias_ref, o_ref, qn_ref, kn_ref, vv_ref, num_ref, den_ref, mx_ref):
    W = ATTN_W
    S = ua_ref.shape[1]
    blk = ATTN_BLOCK
    seg = _seg_ones(W, HEAD_DIM).astype(BF16)
    qw = qw_ref[...]
    kw = kw_ref[...]

    def nrm(x, w):
        ss = _mm(x * x, seg) * (1.0 / HEAD_DIM)
        return x * lax.rsqrt(ss + EPS) * w

    npair = ATTN_HEADS // 2
    pw = 2 * HEAD_DIM

    def norm_block(i, carry):
        rows = pl.ds(pl.multiple_of(i * blk, blk), blk)
        qn = nrm(ua_ref[0, rows, 0:W], qw) * (HEAD_DIM ** -0.5)
        kn = nrm(ua_ref[0, rows, W:2 * W], kw)
        for pp in range(npair):
            qn_ref[pp, rows, :] = qn[:, pp * pw:(pp + 1) * pw]
            kn_ref[pp, rows, :] = kn[:, pp * pw:(pp + 1) * pw]
            vv_ref[pp, rows, :] = ua_ref[0, rows, 2 * W + pp * pw:2 * W + (pp + 1) * pw]
        return carry

    lax.fori_loop(0, S // blk, norm_block, 0)

    lane = _iota((blk, pw), 1)
    low = lane < HEAD_DIM
    half = [jnp.where(low[0:1], 1.0, 0.0).astype(F32), jnp.where(low[0:1], 0.0, 1.0).astype(F32)]

    order = sorted(range(len(DILATED_PAIRS)), key=lambda i: -DILATED_PAIRS[i][1])
    for br in order:
        window, dil = DILATED_PAIRS[br]
        first = br == order[0]
        n_back = window // dil
        nb = S // dil // blk
        has_prev = nb > 1
        nk = 2 * blk if has_prev else blk
        qi = _iota((blk, nk), 0)
        kj = _iota((blk, nk), 1) + (0 if has_prev else blk)
        delta = qi - kj + blk
        band = (delta >= 0) & (delta <= n_back)

        def rows_at(start, dil=dil):
            if dil == 1:
                return pl.ds(pl.multiple_of(start, blk), blk)
            return pl.ds(start, blk, stride=dil)

        def block(idx, carry, br=br, dil=dil, nb=nb, has_prev=has_prev, kj=kj, band=band, rows_at=rows_at,
                  first=first):
            r = idx // nb
            n = idx % nb
            rq = rows_at(r + dil * blk * n)
            valid = band
            if has_prev:
                rp = rows_at(r + dil * blk * jnp.maximum(n - 1, 0))
                valid = band & (kj >= jnp.where(n > 0, 0, blk))
            for pp in range(npair):
                qp = qn_ref[pp, rq, :]
                kp = kn_ref[pp, rq, :]
                vp = vv_ref[pp, rq, :]
                if has_prev:
                    kp = jnp.concatenate([kn_ref[pp, rp, :], kp], axis=0)
                    vp = jnp.concatenate([vv_ref[pp, rp, :], vp], axis=0)
                oh, lh = [], []
                for hh in range(2):
                    bias = bias_ref[br, 2 * pp + hh]
                    if not has_prev:
                        bias = bias[:, blk:]
                    s = jnp.where(valid, _mm_nt(qp * half[hh], kp) + bias, NEG)
                    m = jnp.max(s, axis=-1, keepdims=True)
                    p = jnp.exp(s - m)
                    l = jnp.sum(p, axis=-1, keepdims=True)
                    oh.append(_mm(p, vp) / l)
                    lh.append(m + jnp.log(l))
                o = jnp.where(low, oh[0], oh[1])
                lse = jnp.where(low, lh[0], lh[1])
                if first:
                    num_ref[pp, rq, :] = o
                    den_ref[pp, rq, :] = jnp.ones_like(o)
                    mx_ref[pp, rq, :] = lse
                else:
                    m_old = mx_ref[pp, rq, :]
                    m_new = jnp.maximum(m_old, lse)
                    wa = jnp.exp(m_old - m_new)
                    wb = jnp.exp(lse - m_new)
                    num_ref[pp, rq, :] = num_ref[pp, rq, :] * wa + wb * o
                    den_ref[pp, rq, :] = den_ref[pp, rq, :] * wa + wb
                    mx_ref[pp, rq, :] = m_new
            return carry

        lax.fori_loop(0, dil * nb, block, 0)

    def finish(i, carry):
        rows = pl.ds(pl.multiple_of(i * blk, blk), blk)
        for pp in range(npair):
            o_ref[0, rows, pp * pw:(pp + 1) * pw] = num_ref[pp, rows, :] / den_ref[pp, rows, :]
        return carry

    lax.fori_loop(0, S // blk, finish, 0)


def _attn(ua3, qw, kw, bias_all):
    B, S, _ = ua3.shape
    nbr = len(DILATED_PAIRS)
    return pl.pallas_call(
        _attn_kernel,
        grid=(B,),
        in_specs=[pl.BlockSpec((1, S, AT_COLS), lambda b: (b, 0, 0)),
                  pl.BlockSpec((1, ATTN_W), lambda b: (0, 0)),
                  pl.BlockSpec((1, ATTN_W), lambda b: (0, 0)),
                  pl.BlockSpec((nbr, ATTN_HEADS, ATTN_BLOCK, 2 * ATTN_BLOCK), lambda b: (0, 0, 0, 0))],
        out_specs=pl.BlockSpec((1, S, ATTN_W), lambda b: (b, 0, 0)),
        out_shape=jax.ShapeDtypeStruct((B, S, ATTN_W), F32),
        scratch_shapes=[pltpu.VMEM((ATTN_HEADS // 2, S, 2 * HEAD_DIM), F32) for _ in range(6)],
        compiler_params=_cparams(("parallel",)),
        name="dilated_attn",
    )(ua3, qw, kw, bias_all)


def _outproj_kernel(x_ref, mh_ref, ma_ref, ms_ref, wo_ref, fw_ref, wq_ref, sk_ref,
                    h_ref, xn_ref, sc_ref):
    W = ATTN_W
    h = (x_ref[...] + _mm(mh_ref[...], wo_ref[0:HGRN_W, :])
         + _mm(ma_ref[...], wo_ref[HGRN_W:HGRN_W + W, :])
         + _mm(ms_ref[...], wo_ref[HGRN_W + W:, :]))
    h_ref[...] = h
    xn = (h * lax.rsqrt(jnp.mean(h * h, axis=-1, keepdims=True) + EPS) * fw_ref[...]).astype(BF16)
    xn_ref[...] = xn
    qp = jnp.dot(xn, wq_ref[...], preferred_element_type=F32)
    for hd in range(PEER_HEADS):
        qh = qp[:, 2 * PEER_HALF * hd:2 * PEER_HALF * (hd + 1)]
        for c in range(2):
            sc_ref[hd, c] = _mm_nt(sk_ref[c], qh)


def _outproj(x2, mh, ma, ms, w_out, ffn_w, wq, sk_pad, tm=256):
    T, D = x2.shape
    row = lambda n: pl.BlockSpec((tm, n), lambda i: (i, 0))
    full = lambda a: pl.BlockSpec(a.shape, lambda i: (0,) * a.ndim)
    return pl.pallas_call(
        _outproj_kernel,
        grid=(T // tm,),
        in_specs=[row(D), row(HGRN_W), row(ATTN_W), row(SSM_W),
                  full(w_out), pl.BlockSpec((1, D), lambda i: (0, 0)), full(wq), full(sk_pad)],
        out_specs=[row(D), row(D),
                   pl.BlockSpec((PEER_HEADS, 2, PEER_NKEYS, tm), lambda i: (0, 0, 0, i))],
        out_shape=[jax.ShapeDtypeStruct((T, D), F32), jax.ShapeDtypeStruct((T, D), BF16),
                   jax.ShapeDtypeStruct((PEER_HEADS, 2, PEER_NKEYS, T), F32)],
        compiler_params=_cparams(("parallel",)),
        name="outproj_query",
    )(x2, mh, ma, ms, w_out, ffn_w.reshape(1, D), wq, sk_pad)


_NSEL = PEER_TOPK + 1
_PAIRS = [(i, j) for i in range(_NSEL) for j in range(_NSEL) if (i + 1) * (j + 1) <= _NSEL]


def _top_values(v, k):
    out = []
    for _ in range(k):
        m = jnp.max(v, axis=0, keepdims=True)
        out.append(m)
        v = jnp.where(v == m, NEG, v)
    return out


def _route_kernel(sc_ref, th_ref, u1_ref, u2_ref):
    tt = sc_ref.shape[-1]
    row8 = _iota((8, tt), 0)

    def head(hd, carry):
        s1 = sc_ref[hd, 0]
        s2 = sc_ref[hd, 1]
        a = _top_values(s1, _NSEL)
        b = _top_values(s2, _NSEL)
        groups = []
        for g0 in range(0, len(_PAIRS), 8):
            cg = jnp.full((8, tt), NEG, F32)
            for p, (i, j) in enumerate(_PAIRS[g0:g0 + 8]):
                cg = jnp.where(row8 == p, a[i] + b[j], cg)
            groups.append(cg)
        best = _top_values(jnp.concatenate(groups, axis=0), _NSEL)
        zsum = jnp.ones((1, tt), F32)
        for r in range(1, PEER_TOPK):
            zsum = zsum + jnp.exp(best[r] - best[0])
        cut = 0.5 * (best[PEER_TOPK - 1] + best[PEER_TOPK])
        th_ref[hd] = cut - s1
        u1_ref[hd] = jnp.exp(s1 - a[0]) / zsum
        u2_ref[hd] = jnp.exp(s2 - b[0])
        return carry

    lax.fori_loop(0, PEER_HEADS, head, 0)


def _route(sc, tt=256):
    T = sc.shape[-1]
    blk = pl.BlockSpec((PEER_HEADS, PEER_NKEYS, tt), lambda i: (0, 0, i))
    shp = jax.ShapeDtypeStruct((PEER_HEADS, PEER_NKEYS, T), F32)
    return pl.pallas_call(
        _route_kernel,
        grid=(T // tt,),
        in_specs=[pl.BlockSpec((PEER_HEADS, 2, PEER_NKEYS, tt), lambda i: (0, 0, 0, i))],
        out_specs=[blk, blk, blk],
        out_shape=[shp, shp, shp],
        compiler_params=_cparams(("parallel",)),
        name="peer_route",
    )(sc)


_GELU_C1 = -2.0 * math.sqrt(2.0 / math.pi)
_GELU_C2 = _GELU_C1 * 0.044715
_PEER_EB = 256
_PEER_TOK = 256
_PEER_LANES = 128
_PEER_ROWS = 32
_PEER_FLAGS = None


def _peer_kernel(nj, xn_ref, dn_ref, upt_ref, s2_ref, th_ref, u1_ref, u2_ref, h_ref, o_ref,
                 acc_ref, at0_ref, at1_ref, wg0_ref, wg1_ref):
    j = pl.program_id(1)
    tt = xn_ref.shape[0]
    te = dn_ref.shape[0]
    nb = te // _PEER_EB
    nh = tt // _PEER_TOK
    gpb = _PEER_EB // PEER_NKEYS
    at = (at0_ref, at1_ref)
    wg = (wg0_ref, wg1_ref)

    @pl.when(j == 0)
    def _():
        acc_ref[...] = jnp.zeros_like(acc_ref)

    def scores(k, at_ref):
        rows = pl.ds(pl.multiple_of(k * _PEER_EB, _PEER_EB), _PEER_EB)
        for hf in range(nh):
            at_ref[hf] = lax.dot_general(dn_ref[rows, :], xn_ref[hf * _PEER_TOK:(hf + 1) * _PEER_TOK, :],
                                         (((1,), (1,)), ((), ())), preferred_element_type=F32)

    def weigh(k, at_ref, wg_ref):
        for g in range(gpb):
            c = (j * nb + k) * gpb + g
            er = slice(g * PEER_NKEYS, (g + 1) * PEER_NKEYS)
            for hf in range(nh):
                tw = slice(hf * _PEER_TOK, (hf + 1) * _PEER_TOK)
                ths = [th_ref[hd, pl.ds(c, 1), tw] for hd in range(PEER_HEADS)]
                u1s = [u1_ref[hd, pl.ds(c, 1), tw] for hd in range(PEER_HEADS)]
                for lt in range(_PEER_TOK // _PEER_LANES):
                    ls = slice(lt * _PEER_LANES, (lt + 1) * _PEER_LANES)
                    tl = slice(hf * _PEER_TOK + lt * _PEER_LANES, hf * _PEER_TOK + (lt + 1) * _PEER_LANES)
                    w = None
                    for hd in range(PEER_HEADS):
                        t = jnp.where(s2_ref[hd, :, tl] >= ths[hd][:, ls], u2_ref[hd, :, tl], 0.0) * u1s[hd][:, ls]
                        w = t if w is None else w + t
                    a = at_ref[hf, er, ls]
                    gl = a / (1.0 + jnp.exp(a * (_GELU_C1 + _GELU_C2 * (a * a))))
                    wg_ref[hf, er, ls] = (w * gl).astype(BF16)

    def project(k, wg_ref):
        for hf in range(nh):
            tw = slice(hf * _PEER_TOK, (hf + 1) * _PEER_TOK)
            acc_ref[:, tw] += jnp.dot(upt_ref[k], wg_ref[hf], preferred_element_type=F32)

    scores(0, at[0])
    scores(1, at[1])
    weigh(0, at[0], wg[0])

    def body(m, carry):
        k = 2 * m + 1
        scores(k + 1, at[0])
        weigh(k, at[1], wg[1])
        project(k - 1, wg[0])
        scores(k + 2, at[1])
        weigh(k + 1, at[0], wg[0])
        project(k, wg[1])
        return carry

    lax.fori_loop(0, nb // 2 - 1, body, 0)
    weigh(nb - 1, at[1], wg[1])
    project(nb - 2, wg[0])
    project(nb - 1, wg[1])

    @pl.when(j == nj - 1)
    def _():
        o_ref[...] = h_ref[...] + acc_ref[...].T


def _peer(xn, down_b, layer, up_t3, sc, th, u1, u2, h, tt=512, te=2048):
    T, D = xn.shape
    E = down_b.shape[1]
    nj = E // te
    nh = tt // _PEER_TOK
    rt = pl.BlockSpec((PEER_HEADS, PEER_NKEYS, tt), lambda i, j: (0, 0, i))
    return pl.pallas_call(
        functools.partial(_peer_kernel, nj),
        grid=(T // tt, nj),
        in_specs=[pl.BlockSpec((tt, D), lambda i, j: (i, 0)),
                  pl.BlockSpec((None, te, D), lambda i, j: (layer, j, 0)),
                  pl.BlockSpec((te // _PEER_EB, D, _PEER_EB), lambda i, j: (j, 0, 0)),
                  pl.BlockSpec((PEER_HEADS, None, PEER_NKEYS, tt), lambda i, j: (0, 1, 0, i)),
                  rt, rt, rt,
                  pl.BlockSpec((tt, D), lambda i, j: (i, 0))],
        out_specs=pl.BlockSpec((tt, D), lambda i, j: (i, 0)),
        out_shape=jax.ShapeDtypeStruct((T, D), F32),
        scratch_shapes=[pltpu.VMEM((D, tt), F32),
                        pltpu.VMEM((nh, _PEER_EB, _PEER_TOK), F32), pltpu.VMEM((nh, _PEER_EB, _PEER_TOK), F32),
                        pltpu.VMEM((nh, _PEER_EB, _PEER_TOK), BF16), pltpu.VMEM((nh, _PEER_EB, _PEER_TOK), BF16)],
        compiler_params=_cparams(("parallel", "arbitrary"), flags=_PEER_FLAGS),
        name="peer_experts",
    )(xn, down_b, up_t3, sc, th, u1, u2, h)


def _transpose_cast_kernel(x_ref, o_ref):
    o_ref[0] = x_ref[...].T.astype(o_ref.dtype)


def _transpose_blocks_bf16(x, layer, tr=_PEER_EB):
    _, R, C = x.shape
    return pl.pallas_call(
        _transpose_cast_kernel,
        grid=(R // tr,),
        in_specs=[pl.BlockSpec((None, tr, C), lambda i: (layer, i, 0))],
        out_specs=pl.BlockSpec((1, C, tr), lambda i: (i, 0, 0)),
        out_shape=jax.ShapeDtypeStruct((R // tr, C, tr), BF16),
        compiler_params=_cparams(("parallel",)),
        name="transpose_cast",
    )(x)


def _rep(v, n):
    return jnp.repeat(v, n, axis=-1)


def _layer(x2, B, S, layer, p, bias_all):
    T, D = x2.shape
    uh, ua, us = _inproj(x2, p["attn_norm_w"], p["w_in_pad"], layer)

    mh = _hgrn(uh.reshape(B, S, HG_COLS), p["hgrn_lower_bounds"],
               jnp.tile(p["hgrn_norm_w"], HGRN_HEADS).reshape(1, HGRN_W), layer)
    qw = jnp.tile(p["q_norm_w"], ATTN_HEADS).reshape(1, ATTN_W)
    kw = jnp.tile(p["k_norm_w"], ATTN_HEADS).reshape(1, ATTN_W)
    ma = _attn(ua.reshape(B, S, AT_COLS), qw, kw, bias_all)
    ms = _ssd(us.reshape(B, S, SS_COLS), p["ssm_conv_w"], p["ssm_conv_b"].reshape(1, -1),
              _rep(p["ssm_dt_bias"], HEAD_DIM).reshape(1, -1), _rep(p["ssm_a_log"], HEAD_DIM).reshape(1, -1),
              _rep(p["ssm_d"], HEAD_DIM).reshape(1, -1), p["ssm_norm_w"].reshape(1, -1))

    sk = p["peer_sub_keys"]
    zk = jnp.zeros_like(sk[0])
    sk_pad = jnp.stack([jnp.concatenate([sk[0], zk], axis=1), jnp.concatenate([zk, sk[1]], axis=1)])
    h, xn, sc = _outproj(x2, mh.reshape(T, HGRN_W), ma.reshape(T, ATTN_W), ms.reshape(T, SSM_W),
                         p["w_out"].astype(BF16), p["ffn_norm_w"], p["peer_w_query"].astype(BF16), sk_pad)
    th, u1, u2 = _route(sc)
    return _peer(xn, p["peer_down_b"], layer, _transpose_blocks_bf16(p["peer_up"], layer), sc, th, u1, u2, h)


_PER_LAYER = ("attn_norm_w", "hgrn_norm_w", "q_norm_w", "k_norm_w", "ssm_conv_w", "ssm_conv_b",
              "ssm_dt_bias", "ssm_a_log", "ssm_d", "ssm_norm_w", "w_out", "ffn_norm_w", "peer_w_query",
              "peer_sub_keys")


def kernel(x, attn_norm_w, w_in, hgrn_lower_bounds, hgrn_norm_w, q_norm_w, k_norm_w, rel_bias, ssm_conv_w, ssm_conv_b, ssm_dt_bias, ssm_a_log, ssm_d, ssm_norm_w, w_out, ffn_norm_w, peer_w_query, peer_sub_keys, peer_down, peer_up):
    stacked = dict(attn_norm_w=attn_norm_w, w_in=w_in, hgrn_norm_w=hgrn_norm_w, q_norm_w=q_norm_w,
                   k_norm_w=k_norm_w, ssm_conv_w=ssm_conv_w, ssm_conv_b=ssm_conv_b, ssm_dt_bias=ssm_dt_bias,
                   ssm_a_log=ssm_a_log, ssm_d=ssm_d, ssm_norm_w=ssm_norm_w, w_out=w_out, ffn_norm_w=ffn_norm_w,
                   peer_w_query=peer_w_query, peer_sub_keys=peer_sub_keys, peer_down=peer_down, peer_up=peer_up)
    B, S, D = x.shape
    bias_all = _bias_tables(rel_bias)
    down_b = peer_down.astype(BF16)
    lane_pad = -w_in.shape[2] % 128
    w_in_pad = jnp.pad(w_in, ((0, 0), (0, 0), (0, lane_pad))).astype(BF16)
    x2 = x.reshape(B * S, D)
    for layer in range(w_in.shape[0]):
        p = {k: stacked[k][layer] for k in _PER_LAYER}
        p["w_in_pad"] = w_in_pad
        p["hgrn_lower_bounds"] = hgrn_lower_bounds
        p["peer_down_b"] = down_b
        p["peer_up"] = peer_up
        x2 = _layer(x2, B, S, layer, p, bias_all)
    return x2.reshape(B, S, D)
```

```python
import functools
import math

import numpy as np
import jax
import jax.numpy as jnp
from jax import lax
from jax.experimental import pallas as pl
from jax.experimental.pallas import tpu as pltpu

F32 = jnp.float32
BF16 = jnp.bfloat16
HIGHEST = lax.Precision.HIGHEST
NEG = -1e30
EPS = 1e-6

HEAD_DIM = 64
CHUNK = 64
HGRN_HEADS = 4
HGRN_W = HGRN_HEADS * HEAD_DIM
ATTN_HEADS = 6
ATTN_W = ATTN_HEADS * HEAD_DIM
ATTN_BLOCK = 128
DILATED_PAIRS = ((128, 1), (512, 4), (2048, 16))
NUM_BUCKETS = 32
MAX_DISTANCE = 2048
SSM_HEADS = 6
SSM_W = SSM_HEADS * HEAD_DIM
SSM_GROUPS = 2
SSM_STATE = 128
SSM_GN = SSM_GROUPS * SSM_STATE
SSM_CONV = 4
SSM_CONV_CH = SSM_W + 2 * SSM_GN
PEER_HEADS = 8
PEER_NKEYS = 128
PEER_HALF = 64
PEER_TOPK = 16
_PEER_TOK = 256
HG_COLS = 4 * HGRN_W
AT_COLS = 3 * ATTN_W
SS_COLS = SSM_W + SSM_CONV_CH + SSM_W

VMEM_LIMIT = 56 * 1024 * 1024


def _cparams(sem, flags=None):
    return pltpu.CompilerParams(dimension_semantics=sem, vmem_limit_bytes=VMEM_LIMIT, flags=flags)


def _iota(shape, dim):
    return lax.broadcasted_iota(jnp.int32, shape, dim)


def _mm(a, b):
    return jnp.dot(a.astype(BF16), b.astype(BF16), preferred_element_type=F32)


def _mm_nt(a, b):
    return lax.dot_general(a.astype(BF16), b.astype(BF16), (((1,), (1,)), ((), ())),
                           preferred_element_type=F32)


def _mm_tn(a, b):
    return lax.dot_general(a.astype(BF16), b.astype(BF16), (((0,), (0,)), ((), ())),
                           preferred_element_type=F32)


def _mm_hi(a, b):
    return jnp.dot(a, b, precision=HIGHEST, preferred_element_type=F32)


def _sigmoid(x):
    return 1.0 / (1.0 + jnp.exp(-x))


def _silu(x):
    return x * _sigmoid(x)


def _softplus(x):
    return jnp.maximum(x, 0.0) + jnp.log(1.0 + jnp.exp(-jnp.abs(x)))


def _seg_ones(n, seg):
    r = _iota((n, n), 0)
    c = _iota((n, n), 1)
    same = None
    for s in range(n // seg):
        lo, hi = s * seg, (s + 1) * seg
        t = (r >= lo) & (r < hi) & (c >= lo) & (c < hi)
        same = t if same is None else (same | t)
    return jnp.where(same, 1.0, 0.0).astype(F32)


def _lane_seg_mask(width, seg, idx):
    l = _iota((1, width), 1)
    return jnp.where((l >= idx * seg) & (l < (idx + 1) * seg), 1.0, 0.0).astype(F32)


def _inproj_kernel(x_ref, nw_ref, w_ref, oh_ref, oa_ref, os_ref):
    x = x_ref[...]
    xn = x * lax.rsqrt(jnp.mean(x * x, axis=-1, keepdims=True) + EPS) * nw_ref[...]
    xb = xn.astype(BF16)
    c0, c1, c2 = HG_COLS, HG_COLS + AT_COLS, HG_COLS + AT_COLS + SSM_W + SSM_CONV_CH
    oh_ref[...] = jnp.dot(xb, w_ref[:, 0:c0], preferred_element_type=F32)
    oa_ref[...] = jnp.dot(xb, w_ref[:, c0:c1], preferred_element_type=F32)
    os_ref[:, 0:c2 - c1] = jnp.dot(xb, w_ref[:, c1:c2], preferred_element_type=F32)
    dt = jnp.dot(xb, w_ref[:, c2:], preferred_element_type=F32)
    lanes = w_ref.shape[1] - c2
    expand = jnp.where((_iota((lanes, SSM_W), 1) >> 6) == _iota((lanes, SSM_W), 0), 1.0, 0.0).astype(F32)
    os_ref[:, c2 - c1:] = _mm_hi(dt, expand)


def _inproj(x2, norm_w, w_pad, layer, tm=256):
    T, D = x2.shape
    ncol = w_pad.shape[2]
    return pl.pallas_call(
        _inproj_kernel,
        grid=(T // tm,),
        in_specs=[pl.BlockSpec((tm, D), lambda i: (i, 0)),
                  pl.BlockSpec((1, D), lambda i: (0, 0)),
                  pl.BlockSpec((None, D, ncol), lambda i: (layer, 0, 0))],
        out_specs=[pl.BlockSpec((tm, HG_COLS), lambda i: (i, 0)),
                   pl.BlockSpec((tm, AT_COLS), lambda i: (i, 0)),
                   pl.BlockSpec((tm, SS_COLS), lambda i: (i, 0))],
        out_shape=[jax.ShapeDtypeStruct((T, HG_COLS), F32),
                   jax.ShapeDtypeStruct((T, AT_COLS), F32),
                   jax.ShapeDtypeStruct((T, SS_COLS), F32)],
        compiler_params=_cparams(("parallel",)),
        name="inproj",
    )(x2, norm_w.reshape(1, D), w_pad)


def _hgrn_kernel(layer, nchunk, u_ref, lbs_ref, nw_ref, o_ref, st_ref):
    W = HGRN_W

    @pl.when(pl.program_id(1) == 0)
    def _():
        st_ref[...] = jnp.zeros_like(st_ref)

    lbs = lbs_ref[...]
    e = jnp.exp(lbs - jnp.max(lbs, axis=0, keepdims=True))
    sm = e / jnp.sum(e, axis=0, keepdims=True)
    lb = jnp.zeros((1, W), F32)
    for j in range(1, layer + 1):
        lb = lb + sm[j:j + 1]
    nw = nw_ref[...]

    tril = jnp.where(_iota((CHUNK, CHUNK), 0) >= _iota((CHUNK, CHUNK), 1), 1.0, 0.0).astype(F32)
    row = _iota((CHUNK, W), 0)
    hm = [_lane_seg_mask(W, HEAD_DIM, h) for h in range(HGRN_HEADS)]
    hm3 = [jnp.concatenate([m, m, m], axis=1) for m in hm]
    seg = _seg_ones(W, HEAD_DIM)
    seg_b = seg.astype(BF16)
    tt16 = _iota((16, W), 0)
    z32 = jnp.zeros((32, W), F32)
    in_q2 = (row & 16) != 0

    def chunk(c, carry):
        r0 = pl.multiple_of(c * CHUNK, CHUNK)
        u = u_ref[0, pl.ds(r0, CHUNK), :]
        q = u[:, 0:W] * (HEAD_DIM ** -0.5)
        f = u[:, W:2 * W]
        iv = u[:, 2 * W:3 * W]
        g = u[:, 3 * W:4 * W]
        forget = lb + (1.0 - lb) * _sigmoid(f)
        logf = jnp.log(forget)
        kk = 1.0 - forget
        b = _mm_hi(tril, logf)
        b16, b32, b48, bend = b[16:17], b[32:33], b[48:49], b[63:64]

        q1 = q * jnp.exp(jnp.where(row >= 32, b - b32, NEG))
        k1 = kk * jnp.exp(jnp.where(row < 32, b32 - b, NEG))
        ref2 = jnp.where(row < 32, b16, b48)
        q2 = q * jnp.exp(jnp.where(in_q2, b - ref2, NEG))
        k2 = kk * jnp.exp(jnp.where(in_q2, NEG, ref2 - b))
        qcat = jnp.concatenate([q1, jnp.concatenate([q2[:32], z32], 0), jnp.concatenate([z32, q2[32:]], 0)], 1)
        kcat = jnp.concatenate([k1, jnp.concatenate([k2[:32], z32], 0), jnp.concatenate([z32, k2[32:]], 0)], 1)
        qstack = jnp.concatenate([qcat * hm3[h] for h in range(HGRN_HEADS)], axis=0)
        a = _mm_nt(qstack, kcat)
        ofull = _mm(a, iv)
        o = ofull[0:CHUNK] * hm[0]
        for h in range(1, HGRN_HEADS):
            o = o + ofull[h * CHUNK:(h + 1) * CHUNK] * hm[h]

        st = st_ref[...]
        o = o + _mm_nt(q * jnp.exp(b), st)

        diag = []
        for j in range(4):
            bR = b[16 * j:16 * j + 16]
            qR = q[16 * j:16 * j + 16]
            ps = []
            for s in range(16):
                r = 16 * j + s
                ps.append(qR * kk[r:r + 1] * jnp.exp(jnp.where(tt16 >= s, bR - b[r:r + 1], NEG)))
            ab = jnp.dot(jnp.concatenate(ps, axis=0).astype(BF16), seg_b, preferred_element_type=F32)
            od = ab[0:16] * iv[16 * j:16 * j + 1]
            for s in range(1, 16):
                od = od + ab[16 * s:16 * s + 16] * iv[16 * j + s:16 * j + s + 1]
            diag.append(od)
        o = o + jnp.concatenate(diag, axis=0)

        kend = kk * jnp.exp(bend - b)
        st_ref[...] = st * jnp.exp(bend) + _mm_tn(iv, kend) * seg

        ss = _mm(o * o, seg_b) * (1.0 / HEAD_DIM)
        o_ref[0, pl.ds(r0, CHUNK), :] = o * lax.rsqrt(ss + EPS) * nw * _silu(g)
        return carry

    lax.fori_loop(0, nchunk, chunk, 0)


def _hgrn(uh, lbs, norm_w, layer, blk=256):
    B, S, _ = uh.shape
    return pl.pallas_call(
        functools.partial(_hgrn_kernel, layer, blk // CHUNK),
        grid=(B, S // blk),
        in_specs=[pl.BlockSpec((1, blk, HG_COLS), lambda b, s: (b, s, 0)),
                  pl.BlockSpec(lbs.shape, lambda b, s: (0, 0)),
                  pl.BlockSpec((1, HGRN_W), lambda b, s: (0, 0))],
        out_specs=pl.BlockSpec((1, blk, HGRN_W), lambda b, s: (b, s, 0)),
        out_shape=jax.ShapeDtypeStruct((B, S, HGRN_W), F32),
        scratch_shapes=[pltpu.VMEM((HGRN_W, HGRN_W), F32)],
        compiler_params=_cparams(("parallel", "arbitrary")),
        name="hgrn2",
    )(uh, lbs, norm_w)


def _ssd_kernel(nchunk, u_ref, cw_ref, cb_ref, dtb_ref, alog_ref, d_ref, nw_ref, o_ref,
                st_ref, ext_ref, xc_ref):
    W = SSM_W
    blk = nchunk * CHUNK
    first = pl.program_id(1) == 0

    @pl.when(first)
    def _():
        st_ref[...] = jnp.zeros_like(st_ref)
        ext_ref[0:8, :] = jnp.zeros((8, SSM_CONV_CH), F32)

    @pl.when(jnp.logical_not(first))
    def _():
        ext_ref[0:8, :] = ext_ref[blk:blk + 8, :]

    ext_ref[8:8 + blk, :] = u_ref[0, :, W:W + SSM_CONV_CH]
    cw = cw_ref[...]
    conv = cb_ref[...] + cw[0:1] * ext_ref[5:5 + blk, :]
    for j in range(1, SSM_CONV):
        conv = conv + cw[j:j + 1] * ext_ref[5 + j:5 + j + blk, :]
    xc_ref[...] = _silu(conv)

    dtb = dtb_ref[...]
    a_neg = -jnp.exp(alog_ref[...])
    dsk = d_ref[...]
    nw = nw_ref[...]
    tril = jnp.where(_iota((CHUNK, CHUNK), 0) >= _iota((CHUNK, CHUNK), 1), 1.0, 0.0).astype(F32)
    t_i = _iota((CHUNK, W), 0)
    s_i = _iota((CHUNK, W), 1) & (HEAD_DIM - 1)
    strict = jnp.where(t_i > s_i, 1.0, 0.0).astype(F32)
    lower = t_i >= s_i
    hm = [_lane_seg_mask(W, HEAD_DIM, h) for h in range(SSM_HEADS)]
    zB = jnp.zeros((CHUNK, SSM_STATE), F32)
    gr = _iota((SSM_GN, W), 0) >= SSM_STATE
    gc = _iota((SSM_GN, W), 1) >= (SSM_HEADS // SSM_GROUPS) * HEAD_DIM
    gmask = jnp.where(gr == gc, 1.0, 0.0).astype(F32)
    gw = W // SSM_GROUPS
    seg = _seg_ones(W, gw).astype(BF16)

    def chunk(c, carry):
        r0 = pl.multiple_of(c * CHUNK, CHUNK)
        z = u_ref[0, pl.ds(r0, CHUNK), 0:W]
        dtr = u_ref[0, pl.ds(r0, CHUNK), W + SSM_CONV_CH:W + SSM_CONV_CH + W]
        xc = xc_ref[pl.ds(r0, CHUNK), :]
        xs = xc[:, 0:W]
        bm = xc[:, W:W + SSM_GN]
        cm = xc[:, W + SSM_GN:W + 2 * SSM_GN]
        dt = _softplus(dtr + dtb)
        a = dt * a_neg
        dd = _mm_hi(tril, jnp.concatenate([a * strict, a], axis=1))
        dseg = dd[:, 0:W]
        cum = dd[:, W:2 * W]
        decay = jnp.exp(jnp.where(lower, dseg, NEG))
        b0 = jnp.concatenate([bm[:, 0:SSM_STATE], zB], axis=1)
        b1 = jnp.concatenate([zB, bm[:, SSM_STATE:]], axis=1)
        bstack = jnp.concatenate([b0, b0, b0, b1, b1, b1], axis=0)
        scores = _mm_nt(cm, bstack) * decay
        xdt = xs * dt
        xbd = jnp.concatenate([xdt * hm[h] for h in range(SSM_HEADS)], axis=0)
        y = _mm(scores, xbd)
        st = st_ref[...]
        y = y + _mm(cm, st) * jnp.exp(cum)
        cend = cum[CHUNK - 1:CHUNK]
        to_end = jnp.exp(cend - cum) * dt
        st_ref[...] = st * jnp.exp(cend) + _mm_tn(bm, to_end * xs) * gmask
        y = y + dsk * xs
        y = y * _silu(z)
        ss = _mm(y * y, seg) * (1.0 / gw)
        o_ref[0, pl.ds(r0, CHUNK), :] = y * lax.rsqrt(ss + EPS) * nw
        return carry

    lax.fori_loop(0, nchunk, chunk, 0)


def _ssd(us, conv_w, conv_b, dtb, alog, dsk, norm_w, blk=256):
    B, S, _ = us.shape
    vec = lambda n: pl.BlockSpec((1, n), lambda b, s: (0, 0))
    return pl.pallas_call(
        functools.partial(_ssd_kernel, blk // CHUNK),
        grid=(B, S // blk),
        in_specs=[pl.BlockSpec((1, blk, SS_COLS), lambda b, s: (b, s, 0)),
                  pl.BlockSpec((SSM_CONV, SSM_CONV_CH), lambda b, s: (0, 0)),
                  vec(SSM_CONV_CH), vec(SSM_W), vec(SSM_W), vec(SSM_W), vec(SSM_W)],
        out_specs=pl.BlockSpec((1, blk, SSM_W), lambda b, s: (b, s, 0)),
        out_shape=jax.ShapeDtypeStruct((B, S, SSM_W), F32),
        scratch_shapes=[pltpu.VMEM((SSM_GN, SSM_W), F32),
                        pltpu.VMEM((blk + 8, SSM_CONV_CH), F32),
                        pltpu.VMEM((blk, SSM_CONV_CH), F32)],
        compiler_params=_cparams(("parallel", "arbitrary")),
        name="ssd",
    )(us, conv_w, conv_b, dtb, alog, dsk, norm_w)


def _t5_bucket(dist):
    max_exact = NUM_BUCKETS // 2
    d = np.maximum(dist, 1).astype(np.float32)
    large = max_exact + (np.log(d / max_exact) / np.log(MAX_DISTANCE / max_exact)
                         * (NUM_BUCKETS - max_exact)).astype(np.int32)
    large = np.minimum(large, NUM_BUCKETS - 1)
    return np.where(dist < max_exact, dist, large).astype(np.int32)


def _bucket_tables():
    qi = np.arange(ATTN_BLOCK)[:, None]
    kj = np.arange(2 * ATTN_BLOCK)[None, :]
    delta = qi - kj + ATTN_BLOCK
    return np.stack([_t5_bucket(np.maximum(delta, 0) * dil) for _, dil in DILATED_PAIRS])


def _bias_kernel(rb_ref, bk_ref, o_ref):
    bk = bk_ref[0]
    for h in range(ATTN_HEADS):
        acc = jnp.zeros(bk.shape, F32)
        for b in range(NUM_BUCKETS):
            acc = jnp.where(bk == b, rb_ref[b, h], acc)
        o_ref[0, h] = acc


def _bias_tables(rel_bias):
    bk = jnp.asarray(_bucket_tables())
    nbr = len(DILATED_PAIRS)
    return pl.pallas_call(
        _bias_kernel,
        grid=(nbr,),
        in_specs=[pl.BlockSpec(memory_space=pltpu.SMEM),
                  pl.BlockSpec((1, ATTN_BLOCK, 2 * ATTN_BLOCK), lambda i: (i, 0, 0))],
        out_specs=pl.BlockSpec((1, ATTN_HEADS, ATTN_BLOCK, 2 * ATTN_BLOCK), lambda i: (i, 0, 0, 0)),
        out_shape=jax.ShapeDtypeStruct((nbr, ATTN_HEADS, ATTN_BLOCK, 2 * ATTN_BLOCK), F32),
        compiler_params=_cparams(("parallel",)),
        name="attn_bias",
    )(rel_bias, bk)


def _attn_kernel(ua_ref, qw_ref, kw_ref, bias_ref, o_ref, qn_ref, kn_ref, vv_ref, num_ref, den_ref, mx_ref):
    W = ATTN_W
    S = ua_ref.shape[1]
    blk = ATTN_BLOCK
    seg = _seg_ones(W, HEAD_DIM).astype(BF16)
    qw = qw_ref[...]
    kw = kw_ref[...]

    def nrm(x, w):
        ss = _mm(x * x, seg) * (1.0 / HEAD_DIM)
        return x * lax.rsqrt(ss + EPS) * w

    npair = ATTN_HEADS // 2
    pw = 2 * HEAD_DIM

    def norm_block(i, carry):
        rows = pl.ds(pl.multiple_of(i * blk, blk), blk)
        qn = nrm(ua_ref[0, rows, 0:W], qw) * (HEAD_DIM ** -0.5)
        kn = nrm(ua_ref[0, rows, W:2 * W], kw)
        for pp in range(npair):
            qn_ref[pp, rows, :] = qn[:, pp * pw:(pp + 1) * pw]
            kn_ref[pp, rows, :] = kn[:, pp * pw:(pp + 1) * pw]
            vv_ref[pp, rows, :] = ua_ref[0, rows, 2 * W + pp * pw:2 * W + (pp + 1) * pw]
        return carry

    lax.fori_loop(0, S // blk, norm_block, 0)

    lane = _iota((blk, pw), 1)
    low = lane < HEAD_DIM
    half = [jnp.where(low[0:1], 1.0, 0.0).astype(F32), jnp.where(low[0:1], 0.0, 1.0).astype(F32)]

    order = sorted(range(len(DILATED_PAIRS)), key=lambda i: -DILATED_PAIRS[i][1])
    for br in order:
        window, dil = DILATED_PAIRS[br]
        first = br == order[0]
        n_back = window // dil
        nb = S // dil // blk
        has_prev = nb > 1
        nk = 2 * blk if has_prev else blk
        qi = _iota((blk, nk), 0)
        kj = _iota((blk, nk), 1) + (0 if has_prev else blk)
        delta = qi - kj + blk
        band = (delta >= 0) & (delta <= n_back)

        def rows_at(start, dil=dil):
            if dil == 1:
                return pl.ds(pl.multiple_of(start, blk), blk)
            return pl.ds(start, blk, stride=dil)

        def block(idx, carry, br=br, dil=dil, nb=nb, has_prev=has_prev, kj=kj, band=band, rows_at=rows_at,
                  first=first):
            r = idx // nb
            n = idx % nb
            rq = rows_at(r + dil * blk * n)
            valid = band
            if has_prev:
                rp = rows_at(r + dil * blk * jnp.maximum(n - 1, 0))
                valid = band & (kj >= jnp.where(n > 0, 0, blk))
            for pp in range(npair):
                qp = qn_ref[pp, rq, :]
                kp = kn_ref[pp, rq, :]
                vp = vv_ref[pp, rq, :]
                if has_prev:
                    kp = jnp.concatenate([kn_ref[pp, rp, :], kp], axis=0)
                    vp = jnp.concatenate([vv_ref[pp, rp, :], vp], axis=0)
                oh, lh = [], []
                for hh in range(2):
                    bias = bias_ref[br, 2 * pp + hh]
                    if not has_prev:
                        bias = bias[:, blk:]
                    s = jnp.where(valid, _mm_nt(qp * half[hh], kp) + bias, NEG)
                    m = jnp.max(s, axis=-1, keepdims=True)
                    p = jnp.exp(s - m)
                    l = jnp.sum(p, axis=-1, keepdims=True)
                    oh.append(_mm(p, vp) / l)
                    lh.append(m + jnp.log(l))
                o = jnp.where(low, oh[0], oh[1])
                lse = jnp.where(low, lh[0], lh[1])
                if first:
                    num_ref[pp, rq, :] = o
                    den_ref[pp, rq, :] = jnp.ones_like(o)
                    mx_ref[pp, rq, :] = lse
                else:
                    m_old = mx_ref[pp, rq, :]
                    m_new = jnp.maximum(m_old, lse)
                    wa = jnp.exp(m_old - m_new)
                    wb = jnp.exp(lse - m_new)
                    num_ref[pp, rq, :] = num_ref[pp, rq, :] * wa + wb * o
                    den_ref[pp, rq, :] = den_ref[pp, rq, :] * wa + wb
                    mx_ref[pp, rq, :] = m_new
            return carry

        lax.fori_loop(0, dil * nb, block, 0)

    def finish(i, carry):
        rows = pl.ds(pl.multiple_of(i * blk, blk), blk)
        for pp in range(npair):
            o_ref[0, rows, pp * pw:(pp + 1) * pw] = num_ref[pp, rows, :] / den_ref[pp, rows, :]
        return carry

    lax.fori_loop(0, S // blk, finish, 0)


def _attn(ua3, qw, kw, bias_all):
    B, S, _ = ua3.shape
    nbr = len(DILATED_PAIRS)
    return pl.pallas_call(
        _attn_kernel,
        grid=(B,),
        in_specs=[pl.BlockSpec((1, S, AT_COLS), lambda b: (b, 0, 0)),
                  pl.BlockSpec((1, ATTN_W), lambda b: (0, 0)),
                  pl.BlockSpec((1, ATTN_W), lambda b: (0, 0)),
                  pl.BlockSpec((nbr, ATTN_HEADS, ATTN_BLOCK, 2 * ATTN_BLOCK), lambda b: (0, 0, 0, 0))],
        out_specs=pl.BlockSpec((1, S, ATTN_W), lambda b: (b, 0, 0)),
        out_shape=jax.ShapeDtypeStruct((B, S, ATTN_W), F32),
        scratch_shapes=[pltpu.VMEM((ATTN_HEADS // 2, S, 2 * HEAD_DIM), F32) for _ in range(6)],
        compiler_params=_cparams(("parallel",)),
        name="dilated_attn",
    )(ua3, qw, kw, bias_all)


def _outproj_kernel(x_ref, mh_ref, ma_ref, ms_ref, wo_ref, fw_ref, wq_ref, sk_ref,
                    h_ref, xn_ref, sc_ref):
    W = ATTN_W
    h = (x_ref[...] + _mm(mh_ref[...], wo_ref[0:HGRN_W, :])
         + _mm(ma_ref[...], wo_ref[HGRN_W:HGRN_W + W, :])
         + _mm(ms_ref[...], wo_ref[HGRN_W + W:, :]))
    h_ref[...] = h
    xn = (h * lax.rsqrt(jnp.mean(h * h, axis=-1, keepdims=True) + EPS) * fw_ref[...]).astype(BF16)
    xn_ref[...] = xn
    qp = jnp.dot(xn, wq_ref[...], preferred_element_type=F32)
    for hd in range(PEER_HEADS):
        qh = qp[:, 2 * PEER_HALF * hd:2 * PEER_HALF * (hd + 1)]
        for c in range(2):
            sc_ref[hd, c] = _mm_nt(sk_ref[c], qh)


def _outproj(x2, mh, ma, ms, w_out, ffn_w, wq, sk_pad, tm=256):
    T, D = x2.shape
    row = lambda n: pl.BlockSpec((tm, n), lambda i: (i, 0))
    full = lambda a: pl.BlockSpec(a.shape, lambda i: (0,) * a.ndim)
    return pl.pallas_call(
        _outproj_kernel,
        grid=(T // tm,),
        in_specs=[row(D), row(HGRN_W), row(ATTN_W), row(SSM_W),
                  full(w_out), pl.BlockSpec((1, D), lambda i: (0, 0)), full(wq), full(sk_pad)],
        out_specs=[row(D), row(D),
                   pl.BlockSpec((PEER_HEADS, 2, PEER_NKEYS, tm), lambda i: (0, 0, 0, i))],
        out_shape=[jax.ShapeDtypeStruct((T, D), F32), jax.ShapeDtypeStruct((T, D), BF16),
                   jax.ShapeDtypeStruct((PEER_HEADS, 2, PEER_NKEYS, T), F32)],
        compiler_params=_cparams(("parallel",)),
        name="outproj_query",
    )(x2, mh, ma, ms, w_out, ffn_w.reshape(1, D), wq, sk_pad)


_NSEL = PEER_TOPK + 1
_PAIRS = [(i, j) for i in range(_NSEL) for j in range(_NSEL) if (i + 1) * (j + 1) <= _NSEL]


def _top_values(v, k):
    out = []
    for _ in range(k):
        m = jnp.max(v, axis=0, keepdims=True)
        out.append(m)
        v = jnp.where(v == m, NEG, v)
    return out


def _route_kernel(sc_ref, th_ref, u1_ref, u2_ref):
    tt = sc_ref.shape[-1]
    row8 = _iota((8, tt), 0)

    def head(hd, carry):
        s1 = sc_ref[hd, 0]
        s2 = sc_ref[hd, 1]
        a = _top_values(s1, _NSEL)
        b = _top_values(s2, _NSEL)
        groups = []
        for g0 in range(0, len(_PAIRS), 8):
            cg = jnp.full((8, tt), NEG, F32)
            for p, (i, j) in enumerate(_PAIRS[g0:g0 + 8]):
                cg = jnp.where(row8 == p, a[i] + b[j], cg)
            groups.append(cg)
        best = _top_values(jnp.concatenate(groups, axis=0), _NSEL)
        zsum = jnp.ones((1, tt), F32)
        for r in range(1, PEER_TOPK):
            zsum = zsum + jnp.exp(best[r] - best[0])
        cut = 0.5 * (best[PEER_TOPK - 1] + best[PEER_TOPK])
        th_ref[hd] = jnp.exp(cut - s1 - b[0])
        u1_ref[hd] = jnp.exp(s1 - a[0]) / zsum
        u2_ref[hd] = jnp.exp(s2 - b[0])
        return carry

    lax.fori_loop(0, PEER_HEADS, head, 0)


def _route(sc, tt=256):
    T = sc.shape[-1]
    blk = pl.BlockSpec((PEER_HEADS, PEER_NKEYS, tt), lambda i: (0, 0, i))
    shp = jax.ShapeDtypeStruct((PEER_HEADS, PEER_NKEYS, T), F32)
    return pl.pallas_call(
        _route_kernel,
        grid=(T // tt,),
        in_specs=[pl.BlockSpec((PEER_HEADS, 2, PEER_NKEYS, tt), lambda i: (0, 0, 0, i))],
        out_specs=[blk, blk, blk],
        out_shape=[shp, shp, shp],
        compiler_params=_cparams(("parallel",)),
        name="peer_route",
    )(sc)


_GELU_C1 = -2.0 * math.sqrt(2.0 / math.pi)
_GELU_C2 = _GELU_C1 * 0.044715
_PEER_EB = 256
_PEER_LANES = 128
_PEER_ROWS = 32
_PEER_FLAGS = None


def _peer_kernel(nj, xn_ref, dn_ref, upt_ref, th_ref, u1_ref, u2_ref, h_ref, o_ref,
                 acc_ref, at0_ref, at1_ref, wg0_ref, wg1_ref):
    j = pl.program_id(1)
    tt = xn_ref.shape[0]
    te = dn_ref.shape[0]
    nb = te // _PEER_EB
    nh = tt // _PEER_TOK
    gpb = _PEER_EB // PEER_NKEYS
    at = (at0_ref, at1_ref)
    wg = (wg0_ref, wg1_ref)

    @pl.when(j == 0)
    def _():
        acc_ref[...] = jnp.zeros_like(acc_ref)

    def scores(k, at_ref):
        rows = pl.ds(pl.multiple_of(k * _PEER_EB, _PEER_EB), _PEER_EB)
        for hf in range(nh):
            at_ref[hf] = lax.dot_general(dn_ref[rows, :], xn_ref[hf * _PEER_TOK:(hf + 1) * _PEER_TOK, :],
                                         (((1,), (1,)), ((), ())), preferred_element_type=F32)

    def weigh(k, at_ref, wg_ref):
        for g in range(gpb):
            c = (j * nb + k) * gpb + g
            er = slice(g * PEER_NKEYS, (g + 1) * PEER_NKEYS)
            for hf in range(nh):
                tw = slice(hf * _PEER_TOK, (hf + 1) * _PEER_TOK)
                ths = [th_ref[hd, pl.ds(c, 1), tw] for hd in range(PEER_HEADS)]
                u1s = [u1_ref[hd, pl.ds(c, 1), tw] for hd in range(PEER_HEADS)]
                for lt in range(_PEER_TOK // _PEER_LANES):
                    ls = slice(lt * _PEER_LANES, (lt + 1) * _PEER_LANES)
                    tl = slice(hf * _PEER_TOK + lt * _PEER_LANES, hf * _PEER_TOK + (lt + 1) * _PEER_LANES)
                    w = None
                    for hd in range(PEER_HEADS):
                        u2 = u2_ref[hd, :, tl]
                        t = jnp.where(u2 >= ths[hd][:, ls], u2, 0.0) * u1s[hd][:, ls]
                        w = t if w is None else w + t
                    a = at_ref[hf, er, ls]
                    gl = a / (1.0 + jnp.exp(a * (_GELU_C1 + _GELU_C2 * (a * a))))
                    wg_ref[hf, er, ls] = (w * gl).astype(BF16)

    def project(k, wg_ref):
        for hf in range(nh):
            tw = slice(hf * _PEER_TOK, (hf + 1) * _PEER_TOK)
            acc_ref[:, tw] += jnp.dot(upt_ref[k], wg_ref[hf], preferred_element_type=F32)

    scores(0, at[0])
    scores(1, at[1])
    weigh(0, at[0], wg[0])

    def body(m, carry):
        k = 2 * m + 1
        scores(k + 1, at[0])
        weigh(k, at[1], wg[1])
        project(k - 1, wg[0])
        scores(k + 2, at[1])
        weigh(k + 1, at[0], wg[0])
        project(k, wg[1])
        return carry

    lax.fori_loop(0, nb // 2 - 1, body, 0)
    weigh(nb - 1, at[1], wg[1])
    project(nb - 2, wg[0])
    project(nb - 1, wg[1])

    @pl.when(j == nj - 1)
    def _():
        o_ref[...] = h_ref[...] + acc_ref[...].T


def _peer(xn, down_b, layer, up_t3, th, u1, u2, h, tt=512, te=2048):
    T, D = xn.shape
    E = down_b.shape[1]
    nj = E // te
    nh = tt // _PEER_TOK
    rt = pl.BlockSpec((PEER_HEADS, PEER_NKEYS, tt), lambda i, j: (0, 0, i))
    return pl.pallas_call(
        functools.partial(_peer_kernel, nj),
        grid=(T // tt, nj),
        in_specs=[pl.BlockSpec((tt, D), lambda i, j: (i, 0)),
                  pl.BlockSpec((None, te, D), lambda i, j: (layer, j, 0)),
                  pl.BlockSpec((te // _PEER_EB, D, _PEER_EB), lambda i, j: (j, 0, 0)),
                  rt, rt, rt,
                  pl.BlockSpec((tt, D), lambda i, j: (i, 0))],
        out_specs=pl.BlockSpec((tt, D), lambda i, j: (i, 0)),
        out_shape=jax.ShapeDtypeStruct((T, D), F32),
        scratch_shapes=[pltpu.VMEM((D, tt), F32),
                        pltpu.VMEM((nh, _PEER_EB, _PEER_TOK), F32), pltpu.VMEM((nh, _PEER_EB, _PEER_TOK), F32),
                        pltpu.VMEM((nh, _PEER_EB, _PEER_TOK), BF16), pltpu.VMEM((nh, _PEER_EB, _PEER_TOK), BF16)],
        compiler_params=_cparams(("parallel", "arbitrary"), flags=_PEER_FLAGS),
        name="peer_experts",
    )(xn, down_b, up_t3, th, u1, u2, h)


def _transpose_cast_kernel(x_ref, o_ref):
    o_ref[0] = x_ref[...].T.astype(o_ref.dtype)


def _transpose_blocks_bf16(x, layer, tr=_PEER_EB):
    _, R, C = x.shape
    return pl.pallas_call(
        _transpose_cast_kernel,
        grid=(R // tr,),
        in_specs=[pl.BlockSpec((None, tr, C), lambda i: (layer, i, 0))],
        out_specs=pl.BlockSpec((1, C, tr), lambda i: (i, 0, 0)),
        out_shape=jax.ShapeDtypeStruct((R // tr, C, tr), BF16),
        compiler_params=_cparams(("parallel",)),
        name="transpose_cast",
    )(x)


def _rep(v, n):
    return jnp.repeat(v, n, axis=-1)


def _layer(x2, B, S, layer, p, bias_all):
    T, D = x2.shape
    uh, ua, us = _inproj(x2, p["attn_norm_w"], p["w_in_pad"], layer)

    mh = _hgrn(uh.reshape(B, S, HG_COLS), p["hgrn_lower_bounds"],
               jnp.tile(p["hgrn_norm_w"], HGRN_HEADS).reshape(1, HGRN_W), layer)
    qw = jnp.tile(p["q_norm_w"], ATTN_HEADS).reshape(1, ATTN_W)
    kw = jnp.tile(p["k_norm_w"], ATTN_HEADS).reshape(1, ATTN_W)
    ma = _attn(ua.reshape(B, S, AT_COLS), qw, kw, bias_all)
    ms = _ssd(us.reshape(B, S, SS_COLS), p["ssm_conv_w"], p["ssm_conv_b"].reshape(1, -1),
              _rep(p["ssm_dt_bias"], HEAD_DIM).reshape(1, -1), _rep(p["ssm_a_log"], HEAD_DIM).reshape(1, -1),
              _rep(p["ssm_d"], HEAD_DIM).reshape(1, -1), p["ssm_norm_w"].reshape(1, -1))

    sk = p["peer_sub_keys"]
    zk = jnp.zeros_like(sk[0])
    sk_pad = jnp.stack([jnp.concatenate([sk[0], zk], axis=1), jnp.concatenate([zk, sk[1]], axis=1)])
    h, xn, sc = _outproj(x2, mh.reshape(T, HGRN_W), ma.reshape(T, ATTN_W), ms.reshape(T, SSM_W),
                         p["w_out"].astype(BF16), p["ffn_norm_w"], p["peer_w_query"].astype(BF16), sk_pad)
    th, u1, u2 = _route(sc)
    return _peer(xn, p["peer_down_b"], layer, _transpose_blocks_bf16(p["peer_up"], layer), th, u1, u2, h)


_PER_LAYER = ("attn_norm_w", "hgrn_norm_w", "q_norm_w", "k_norm_w", "ssm_conv_w", "ssm_conv_b",
              "ssm_dt_bias", "ssm_a_log", "ssm_d", "ssm_norm_w", "w_out", "ffn_norm_w", "peer_w_query",
              "peer_sub_keys")


def kernel(x, attn_norm_w, w_in, hgrn_lower_bounds, hgrn_norm_w, q_norm_w, k_norm_w, rel_bias, ssm_conv_w, ssm_conv_b, ssm_dt_bias, ssm_a_log, ssm_d, ssm_norm_w, w_out, ffn_norm_w, peer_w_query, peer_sub_keys, peer_down, peer_up):
    stacked = dict(attn_norm_w=attn_norm_w, w_in=w_in, hgrn_norm_w=hgrn_norm_w, q_norm_w=q_norm_w,
                   k_norm_w=k_norm_w, ssm_conv_w=ssm_conv_w, ssm_conv_b=ssm_conv_b, ssm_dt_bias=ssm_dt_bias,
                   ssm_a_log=ssm_a_log, ssm_d=ssm_d, ssm_norm_w=ssm_norm_w, w_out=w_out, ffn_norm_w=ffn_norm_w,
                   peer_w_query=peer_w_query, peer_sub_keys=peer_sub_keys, peer_down=peer_down, peer_up=peer_up)
    B, S, D = x.shape
    bias_all = _bias_tables(rel_bias)
    down_b = peer_down.astype(BF16)
    lane_pad = -w_in.shape[2] % 128
    w_in_pad = jnp.pad(w_in, ((0, 0), (0, 0), (0, lane_pad))).astype(BF16)
    x2 = x.reshape(B * S, D)
    for layer in range(w_in.shape[0]):
        p = {k: stacked[k][layer] for k in _PER_LAYER}
        p["w_in_pad"] = w_in_pad
        p["hgrn_lower_bounds"] = hgrn_lower_bounds
        p["peer_down_b"] = down_b
        p["peer_up"] = peer_up
        x2 = _layer(x2, B, S, layer, p, bias_all)
    return x2.reshape(B, S, D)
```

```python
import functools
import math

import numpy as np
import jax
import jax.numpy as jnp
from jax import lax
from jax.experimental import pallas as pl
from jax.experimental.pallas import tpu as pltpu

F32 = jnp.float32
BF16 = jnp.bfloat16
HIGHEST = lax.Precision.HIGHEST
NEG = -1e30
EPS = 1e-6

HEAD_DIM = 64
CHUNK = 64
HGRN_HEADS = 4
HGRN_W = HGRN_HEADS * HEAD_DIM
ATTN_HEADS = 6
ATTN_W = ATTN_HEADS * HEAD_DIM
ATTN_BLOCK = 128
DILATED_PAIRS = ((128, 1), (512, 4), (2048, 16))
NUM_BUCKETS = 32
MAX_DISTANCE = 2048
SSM_HEADS = 6
SSM_W = SSM_HEADS * HEAD_DIM
SSM_GROUPS = 2
SSM_STATE = 128
SSM_GN = SSM_GROUPS * SSM_STATE
SSM_CONV = 4
SSM_CONV_CH = SSM_W + 2 * SSM_GN
PEER_HEADS = 8
PEER_NKEYS = 128
PEER_HALF = 64
PEER_TOPK = 16
_PEER_TOK = 256
HG_COLS = 4 * HGRN_W
AT_COLS = 3 * ATTN_W
SS_COLS = SSM_W + SSM_CONV_CH + SSM_W

VMEM_LIMIT = 56 * 1024 * 1024


def _cparams(sem, flags=None):
    return pltpu.CompilerParams(dimension_semantics=sem, vmem_limit_bytes=VMEM_LIMIT, flags=flags)


def _iota(shape, dim):
    return lax.broadcasted_iota(jnp.int32, shape, dim)


def _mm(a, b):
    return jnp.dot(a.astype(BF16), b.astype(BF16), preferred_element_type=F32)


def _mm_nt(a, b):
    return lax.dot_general(a.astype(BF16), b.astype(BF16), (((1,), (1,)), ((), ())),
                           preferred_element_type=F32)


def _mm_tn(a, b):
    return lax.dot_general(a.astype(BF16), b.astype(BF16), (((0,), (0,)), ((), ())),
                           preferred_element_type=F32)


def _mm_hi(a, b):
    return jnp.dot(a, b, precision=HIGHEST, preferred_element_type=F32)


def _sigmoid(x):
    return 1.0 / (1.0 + jnp.exp(-x))


def _silu(x):
    return x * _sigmoid(x)


def _softplus(x):
    return jnp.maximum(x, 0.0) + jnp.log(1.0 + jnp.exp(-jnp.abs(x)))


def _seg_ones(n, seg):
    r = _iota((n, n), 0)
    c = _iota((n, n), 1)
    same = None
    for s in range(n // seg):
        lo, hi = s * seg, (s + 1) * seg
        t = (r >= lo) & (r < hi) & (c >= lo) & (c < hi)
        same = t if same is None else (same | t)
    return jnp.where(same, 1.0, 0.0).astype(F32)


def _lane_seg_mask(width, seg, idx):
    l = _iota((1, width), 1)
    return jnp.where((l >= idx * seg) & (l < (idx + 1) * seg), 1.0, 0.0).astype(F32)


def _inproj_kernel(x_ref, nw_ref, w_ref, oh_ref, oa_ref, os_ref):
    x = x_ref[...]
    xn = x * lax.rsqrt(jnp.mean(x * x, axis=-1, keepdims=True) + EPS) * nw_ref[...]
    xb = xn.astype(BF16)
    c0, c1, c2 = HG_COLS, HG_COLS + AT_COLS, HG_COLS + AT_COLS + SSM_W + SSM_CONV_CH
    oh_ref[...] = jnp.dot(xb, w_ref[:, 0:c0], preferred_element_type=F32)
    oa_ref[...] = jnp.dot(xb, w_ref[:, c0:c1], preferred_element_type=F32)
    os_ref[:, 0:c2 - c1] = jnp.dot(xb, w_ref[:, c1:c2], preferred_element_type=F32)
    dt = jnp.dot(xb, w_ref[:, c2:], preferred_element_type=F32)
    lanes = w_ref.shape[1] - c2
    expand = jnp.where((_iota((lanes, SSM_W), 1) >> 6) == _iota((lanes, SSM_W), 0), 1.0, 0.0).astype(F32)
    os_ref[:, c2 - c1:] = _mm_hi(dt, expand)


def _inproj(x2, norm_w, w_pad, layer, tm=256):
    T, D = x2.shape
    ncol = w_pad.shape[2]
    return pl.pallas_call(
        _inproj_kernel,
        grid=(T // tm,),
        in_specs=[pl.BlockSpec((tm, D), lambda i: (i, 0)),
                  pl.BlockSpec((1, D), lambda i: (0, 0)),
                  pl.BlockSpec((None, D, ncol), lambda i: (layer, 0, 0))],
        out_specs=[pl.BlockSpec((tm, HG_COLS), lambda i: (i, 0)),
                   pl.BlockSpec((tm, AT_COLS), lambda i: (i, 0)),
                   pl.BlockSpec((tm, SS_COLS), lambda i: (i, 0))],
        out_shape=[jax.ShapeDtypeStruct((T, HG_COLS), F32),
                   jax.ShapeDtypeStruct((T, AT_COLS), F32),
                   jax.ShapeDtypeStruct((T, SS_COLS), F32)],
        compiler_params=_cparams(("parallel",)),
        name="inproj",
    )(x2, norm_w.reshape(1, D), w_pad)


def _hgrn_kernel(layer, nchunk, u_ref, lbs_ref, nw_ref, o_ref, st_ref):
    W = HGRN_W

    @pl.when(pl.program_id(1) == 0)
    def _():
        st_ref[...] = jnp.zeros_like(st_ref)

    lbs = lbs_ref[...]
    e = jnp.exp(lbs - jnp.max(lbs, axis=0, keepdims=True))
    sm = e / jnp.sum(e, axis=0, keepdims=True)
    lb = jnp.zeros((1, W), F32)
    for j in range(1, layer + 1):
        lb = lb + sm[j:j + 1]
    nw = nw_ref[...]

    tril = jnp.where(_iota((CHUNK, CHUNK), 0) >= _iota((CHUNK, CHUNK), 1), 1.0, 0.0).astype(F32)
    row = _iota((CHUNK, W), 0)
    hm = [_lane_seg_mask(W, HEAD_DIM, h) for h in range(HGRN_HEADS)]
    hm3 = [jnp.concatenate([m, m, m], axis=1) for m in hm]
    seg = _seg_ones(W, HEAD_DIM)
    seg_b = seg.astype(BF16)
    tt16 = _iota((16, W), 0)
    z32 = jnp.zeros((32, W), F32)
    in_q2 = (row & 16) != 0

    def chunk(c, carry):
        r0 = pl.multiple_of(c * CHUNK, CHUNK)
        u = u_ref[0, pl.ds(r0, CHUNK), :]
        q = u[:, 0:W] * (HEAD_DIM ** -0.5)
        f = u[:, W:2 * W]
        iv = u[:, 2 * W:3 * W]
        g = u[:, 3 * W:4 * W]
        forget = lb + (1.0 - lb) * _sigmoid(f)
        logf = jnp.log(forget)
        kk = 1.0 - forget
        b = _mm_hi(tril, logf)
        b16, b32, b48, bend = b[16:17], b[32:33], b[48:49], b[63:64]

        q1 = q * jnp.exp(jnp.where(row >= 32, b - b32, NEG))
        k1 = kk * jnp.exp(jnp.where(row < 32, b32 - b, NEG))
        ref2 = jnp.where(row < 32, b16, b48)
        q2 = q * jnp.exp(jnp.where(in_q2, b - ref2, NEG))
        k2 = kk * jnp.exp(jnp.where(in_q2, NEG, ref2 - b))
        qcat = jnp.concatenate([q1, jnp.concatenate([q2[:32], z32], 0), jnp.concatenate([z32, q2[32:]], 0)], 1)
        kcat = jnp.concatenate([k1, jnp.concatenate([k2[:32], z32], 0), jnp.concatenate([z32, k2[32:]], 0)], 1)
        qstack = jnp.concatenate([qcat * hm3[h] for h in range(HGRN_HEADS)], axis=0)
        a = _mm_nt(qstack, kcat)
        ofull = _mm(a, iv)
        o = ofull[0:CHUNK] * hm[0]
        for h in range(1, HGRN_HEADS):
            o = o + ofull[h * CHUNK:(h + 1) * CHUNK] * hm[h]

        st = st_ref[...]
        o = o + _mm_nt(q * jnp.exp(b), st)

        diag = []
        for j in range(4):
            bR = b[16 * j:16 * j + 16]
            qR = q[16 * j:16 * j + 16]
            ps = []
            for s in range(16):
                r = 16 * j + s
                ps.append(qR * kk[r:r + 1] * jnp.exp(jnp.where(tt16 >= s, bR - b[r:r + 1], NEG)))
            ab = jnp.dot(jnp.concatenate(ps, axis=0).astype(BF16), seg_b, preferred_element_type=F32)
            od = ab[0:16] * iv[16 * j:16 * j + 1]
            for s in range(1, 16):
                od = od + ab[16 * s:16 * s + 16] * iv[16 * j + s:16 * j + s + 1]
            diag.append(od)
        o = o + jnp.concatenate(diag, axis=0)

        kend = kk * jnp.exp(bend - b)
        st_ref[...] = st * jnp.exp(bend) + _mm_tn(iv, kend) * seg

        ss = _mm(o * o, seg_b) * (1.0 / HEAD_DIM)
        o_ref[0, pl.ds(r0, CHUNK), :] = o * lax.rsqrt(ss + EPS) * nw * _silu(g)
        return carry

    lax.fori_loop(0, nchunk, chunk, 0)


def _hgrn(uh, lbs, norm_w, layer, blk=256):
    B, S, _ = uh.shape
    return pl.pallas_call(
        functools.partial(_hgrn_kernel, layer, blk // CHUNK),
        grid=(B, S // blk),
        in_specs=[pl.BlockSpec((1, blk, HG_COLS), lambda b, s: (b, s, 0)),
                  pl.BlockSpec(lbs.shape, lambda b, s: (0, 0)),
                  pl.BlockSpec((1, HGRN_W), lambda b, s: (0, 0))],
        out_specs=pl.BlockSpec((1, blk, HGRN_W), lambda b, s: (b, s, 0)),
        out_shape=jax.ShapeDtypeStruct((B, S, HGRN_W), F32),
        scratch_shapes=[pltpu.VMEM((HGRN_W, HGRN_W), F32)],
        compiler_params=_cparams(("parallel", "arbitrary")),
        name="hgrn2",
    )(uh, lbs, norm_w)


def _ssd_kernel(nchunk, u_ref, cw_ref, cb_ref, dtb_ref, alog_ref, d_ref, nw_ref, o_ref,
                st_ref, ext_ref, xc_ref):
    W = SSM_W
    blk = nchunk * CHUNK
    first = pl.program_id(1) == 0

    @pl.when(first)
    def _():
        st_ref[...] = jnp.zeros_like(st_ref)
        ext_ref[0:8, :] = jnp.zeros((8, SSM_CONV_CH), F32)

    @pl.when(jnp.logical_not(first))
    def _():
        ext_ref[0:8, :] = ext_ref[blk:blk + 8, :]

    ext_ref[8:8 + blk, :] = u_ref[0, :, W:W + SSM_CONV_CH]
    cw = cw_ref[...]
    conv = cb_ref[...] + cw[0:1] * ext_ref[5:5 + blk, :]
    for j in range(1, SSM_CONV):
        conv = conv + cw[j:j + 1] * ext_ref[5 + j:5 + j + blk, :]
    xc_ref[...] = _silu(conv)

    dtb = dtb_ref[...]
    a_neg = -jnp.exp(alog_ref[...])
    dsk = d_ref[...]
    nw = nw_ref[...]
    tril = jnp.where(_iota((CHUNK, CHUNK), 0) >= _iota((CHUNK, CHUNK), 1), 1.0, 0.0).astype(F32)
    t_i = _iota((CHUNK, W), 0)
    s_i = _iota((CHUNK, W), 1) & (HEAD_DIM - 1)
    strict = jnp.where(t_i > s_i, 1.0, 0.0).astype(F32)
    lower = t_i >= s_i
    hm = [_lane_seg_mask(W, HEAD_DIM, h) for h in range(SSM_HEADS)]
    zB = jnp.zeros((CHUNK, SSM_STATE), F32)
    gr = _iota((SSM_GN, W), 0) >= SSM_STATE
    gc = _iota((SSM_GN, W), 1) >= (SSM_HEADS // SSM_GROUPS) * HEAD_DIM
    gmask = jnp.where(gr == gc, 1.0, 0.0).astype(F32)
    gw = W // SSM_GROUPS
    seg = _seg_ones(W, gw).astype(BF16)

    def chunk(c, carry):
        r0 = pl.multiple_of(c * CHUNK, CHUNK)
        z = u_ref[0, pl.ds(r0, CHUNK), 0:W]
        dtr = u_ref[0, pl.ds(r0, CHUNK), W + SSM_CONV_CH:W + SSM_CONV_CH + W]
        xc = xc_ref[pl.ds(r0, CHUNK), :]
        xs = xc[:, 0:W]
        bm = xc[:, W:W + SSM_GN]
        cm = xc[:, W + SSM_GN:W + 2 * SSM_GN]
        dt = _softplus(dtr + dtb)
        a = dt * a_neg
        dd = _mm_hi(tril, jnp.concatenate([a * strict, a], axis=1))
        dseg = dd[:, 0:W]
        cum = dd[:, W:2 * W]
        decay = jnp.exp(jnp.where(lower, dseg, NEG))
        b0 = jnp.concatenate([bm[:, 0:SSM_STATE], zB], axis=1)
        b1 = jnp.concatenate([zB, bm[:, SSM_STATE:]], axis=1)
        bstack = jnp.concatenate([b0, b0, b0, b1, b1, b1], axis=0)
        scores = _mm_nt(cm, bstack) * decay
        xdt = xs * dt
        xbd = jnp.concatenate([xdt * hm[h] for h in range(SSM_HEADS)], axis=0)
        y = _mm(scores, xbd)
        st = st_ref[...]
        y = y + _mm(cm, st) * jnp.exp(cum)
        cend = cum[CHUNK - 1:CHUNK]
        to_end = jnp.exp(cend - cum) * dt
        st_ref[...] = st * jnp.exp(cend) + _mm_tn(bm, to_end * xs) * gmask
        y = y + dsk * xs
        y = y * _silu(z)
        ss = _mm(y * y, seg) * (1.0 / gw)
        o_ref[0, pl.ds(r0, CHUNK), :] = y * lax.rsqrt(ss + EPS) * nw
        return carry

    lax.fori_loop(0, nchunk, chunk, 0)


def _ssd(us, conv_w, conv_b, dtb, alog, dsk, norm_w, blk=256):
    B, S, _ = us.shape
    vec = lambda n: pl.BlockSpec((1, n), lambda b, s: (0, 0))
    return pl.pallas_call(
        functools.partial(_ssd_kernel, blk // CHUNK),
        grid=(B, S // blk),
        in_specs=[pl.BlockSpec((1, blk, SS_COLS), lambda b, s: (b, s, 0)),
                  pl.BlockSpec((SSM_CONV, SSM_CONV_CH), lambda b, s: (0, 0)),
                  vec(SSM_CONV_CH), vec(SSM_W), vec(SSM_W), vec(SSM_W), vec(SSM_W)],
        out_specs=pl.BlockSpec((1, blk, SSM_W), lambda b, s: (b, s, 0)),
        out_shape=jax.ShapeDtypeStruct((B, S, SSM_W), F32),
        scratch_shapes=[pltpu.VMEM((SSM_GN, SSM_W), F32),
                        pltpu.VMEM((blk + 8, SSM_CONV_CH), F32),
                        pltpu.VMEM((blk, SSM_CONV_CH), F32)],
        compiler_params=_cparams(("parallel", "arbitrary")),
        name="ssd",
    )(us, conv_w, conv_b, dtb, alog, dsk, norm_w)


def _t5_bucket(dist):
    max_exact = NUM_BUCKETS // 2
    d = np.maximum(dist, 1).astype(np.float32)
    large = max_exact + (np.log(d / max_exact) / np.log(MAX_DISTANCE / max_exact)
                         * (NUM_BUCKETS - max_exact)).astype(np.int32)
    large = np.minimum(large, NUM_BUCKETS - 1)
    return np.where(dist < max_exact, dist, large).astype(np.int32)


def _bucket_tables():
    qi = np.arange(ATTN_BLOCK)[:, None]
    kj = np.arange(2 * ATTN_BLOCK)[None, :]
    delta = qi - kj + ATTN_BLOCK
    return np.stack([_t5_bucket(np.maximum(delta, 0) * dil) for _, dil in DILATED_PAIRS])


def _bias_kernel(rb_ref, bk_ref, o_ref):
    bk = bk_ref[0]
    for h in range(ATTN_HEADS):
        acc = jnp.zeros(bk.shape, F32)
        for b in range(NUM_BUCKETS):
            acc = jnp.where(bk == b, rb_ref[b, h], acc)
        o_ref[0, h] = acc


def _bias_tables(rel_bias):
    bk = jnp.asarray(_bucket_tables())
    nbr = len(DILATED_PAIRS)
    return pl.pallas_call(
        _bias_kernel,
        grid=(nbr,),
        in_specs=[pl.BlockSpec(memory_space=pltpu.SMEM),
                  pl.BlockSpec((1, ATTN_BLOCK, 2 * ATTN_BLOCK), lambda i: (i, 0, 0))],
        out_specs=pl.BlockSpec((1, ATTN_HEADS, ATTN_BLOCK, 2 * ATTN_BLOCK), lambda i: (i, 0, 0, 0)),
        out_shape=jax.ShapeDtypeStruct((nbr, ATTN_HEADS, ATTN_BLOCK, 2 * ATTN_BLOCK), F32),
        compiler_params=_cparams(("parallel",)),
        name="attn_bias",
    )(rel_bias, bk)


def _attn_kernel(ua_ref, qw_ref, kw_ref, bias_ref, o_ref, qn_ref, kn_ref, vv_ref, num_ref, den_ref, mx_ref):
    W = ATTN_W
    S = ua_ref.shape[1]
    blk = ATTN_BLOCK
    seg = _seg_ones(W, HEAD_DIM).astype(BF16)
    qw = qw_ref[...]
    kw = kw_ref[...]

    def nrm(x, w):
        ss = _mm(x * x, seg) * (1.0 / HEAD_DIM)
        return x * lax.rsqrt(ss + EPS) * w

    npair = ATTN_HEADS // 2
    pw = 2 * HEAD_DIM

    def norm_block(i, carry):
        rows = pl.ds(pl.multiple_of(i * blk, blk), blk)
        qn = nrm(ua_ref[0, rows, 0:W], qw) * (HEAD_DIM ** -0.5)
        kn = nrm(ua_ref[0, rows, W:2 * W], kw)
        for pp in range(npair):
            qn_ref[pp, rows, :] = qn[:, pp * pw:(pp + 1) * pw]
            kn_ref[pp, rows, :] = kn[:, pp * pw:(pp + 1) * pw]
            vv_ref[pp, rows, :] = ua_ref[0, rows, 2 * W + pp * pw:2 * W + (pp + 1) * pw]
        return carry

    lax.fori_loop(0, S // blk, norm_block, 0)

    lane = _iota((blk, pw), 1)
    low = lane < HEAD_DIM
    half = [jnp.where(low[0:1], 1.0, 0.0).astype(F32), jnp.where(low[0:1], 0.0, 1.0).astype(F32)]

    order = sorted(range(len(DILATED_PAIRS)), key=lambda i: -DILATED_PAIRS[i][1])
    for br in order:
        window, dil = DILATED_PAIRS[br]
        first = br == order[0]
        n_back = window // dil
        nb = S // dil // blk
        has_prev = nb > 1
        nk = 2 * blk if has_prev else blk
        qi = _iota((blk, nk), 0)
        kj = _iota((blk, nk), 1) + (0 if has_prev else blk)
        delta = qi - kj + blk
        band = (delta >= 0) & (delta <= n_back)

        def rows_at(start, dil=dil):
            if dil == 1:
                return pl.ds(pl.multiple_of(start, blk), blk)
            return pl.ds(start, blk, stride=dil)

        def block(idx, carry, br=br, dil=dil, nb=nb, has_prev=has_prev, kj=kj, band=band, rows_at=rows_at,
                  first=first):
            r = idx // nb
            n = idx % nb
            rq = rows_at(r + dil * blk * n)
            valid = band
            if has_prev:
                rp = rows_at(r + dil * blk * jnp.maximum(n - 1, 0))
                valid = band & (kj >= jnp.where(n > 0, 0, blk))
            for pp in range(npair):
                qp = qn_ref[pp, rq, :]
                kp = kn_ref[pp, rq, :]
                vp = vv_ref[pp, rq, :]
                if has_prev:
                    kp = jnp.concatenate([kn_ref[pp, rp, :], kp], axis=0)
                    vp = jnp.concatenate([vv_ref[pp, rp, :], vp], axis=0)
                oh, lh = [], []
                for hh in range(2):
                    bias = bias_ref[br, 2 * pp + hh]
                    if not has_prev:
                        bias = bias[:, blk:]
                    s = jnp.where(valid, _mm_nt(qp * half[hh], kp) + bias, NEG)
                    m = jnp.max(s, axis=-1, keepdims=True)
                    p = jnp.exp(s - m)
                    l = jnp.sum(p, axis=-1, keepdims=True)
                    oh.append(_mm(p, vp) / l)
                    lh.append(m + jnp.log(l))
                o = jnp.where(low, oh[0], oh[1])
                lse = jnp.where(low, lh[0], lh[1])
                if first:
                    num_ref[pp, rq, :] = o
                    den_ref[pp, rq, :] = jnp.ones_like(o)
                    mx_ref[pp, rq, :] = lse
                else:
                    m_old = mx_ref[pp, rq, :]
                    m_new = jnp.maximum(m_old, lse)
                    wa = jnp.exp(m_old - m_new)
                    wb = jnp.exp(lse - m_new)
                    num_ref[pp, rq, :] = num_ref[pp, rq, :] * wa + wb * o
                    den_ref[pp, rq, :] = den_ref[pp, rq, :] * wa + wb
                    mx_ref[pp, rq, :] = m_new
            return carry

        lax.fori_loop(0, dil * nb, block, 0)

    def finish(i, carry):
        rows = pl.ds(pl.multiple_of(i * blk, blk), blk)
        for pp in range(npair):
            o_ref[0, rows, pp * pw:(pp + 1) * pw] = num_ref[pp, rows, :] / den_ref[pp, rows, :]
        return carry

    lax.fori_loop(0, S // blk, finish, 0)


def _attn(ua3, qw, kw, bias_all):
    B, S, _ = ua3.shape
    nbr = len(DILATED_PAIRS)
    return pl.pallas_call(
        _attn_kernel,
        grid=(B,),
        in_specs=[pl.BlockSpec((1, S, AT_COLS), lambda b: (b, 0, 0)),
                  pl.BlockSpec((1, ATTN_W), lambda b: (0, 0)),
                  pl.BlockSpec((1, ATTN_W), lambda b: (0, 0)),
                  pl.BlockSpec((nbr, ATTN_HEADS, ATTN_BLOCK, 2 * ATTN_BLOCK), lambda b: (0, 0, 0, 0))],
        out_specs=pl.BlockSpec((1, S, ATTN_W), lambda b: (b, 0, 0)),
        out_shape=jax.ShapeDtypeStruct((B, S, ATTN_W), F32),
        scratch_shapes=[pltpu.VMEM((ATTN_HEADS // 2, S, 2 * HEAD_DIM), F32) for _ in range(6)],
        compiler_params=_cparams(("parallel",)),
        name="dilated_attn",
    )(ua3, qw, kw, bias_all)


def _outproj_kernel(x_ref, mh_ref, ma_ref, ms_ref, wo_ref, fw_ref, wq_ref, sk_ref,
                    h_ref, xn_ref, sc_ref):
    W = ATTN_W
    h = (x_ref[...] + _mm(mh_ref[...], wo_ref[0:HGRN_W, :])
         + _mm(ma_ref[...], wo_ref[HGRN_W:HGRN_W + W, :])
         + _mm(ms_ref[...], wo_ref[HGRN_W + W:, :]))
    h_ref[...] = h
    xn = (h * lax.rsqrt(jnp.mean(h * h, axis=-1, keepdims=True) + EPS) * fw_ref[...]).astype(BF16)
    xn_ref[...] = xn
    qp = jnp.dot(xn, wq_ref[...], preferred_element_type=F32)
    for hd in range(PEER_HEADS):
        qh = qp[:, 2 * PEER_HALF * hd:2 * PEER_HALF * (hd + 1)]
        for c in range(2):
            sc_ref[hd, c] = _mm_nt(sk_ref[c], qh)


def _outproj(x2, mh, ma, ms, w_out, ffn_w, wq, sk_pad, tm=256):
    T, D = x2.shape
    row = lambda n: pl.BlockSpec((tm, n), lambda i: (i, 0))
    full = lambda a: pl.BlockSpec(a.shape, lambda i: (0,) * a.ndim)
    return pl.pallas_call(
        _outproj_kernel,
        grid=(T // tm,),
        in_specs=[row(D), row(HGRN_W), row(ATTN_W), row(SSM_W),
                  full(w_out), pl.BlockSpec((1, D), lambda i: (0, 0)), full(wq), full(sk_pad)],
        out_specs=[row(D), row(D),
                   pl.BlockSpec((PEER_HEADS, 2, PEER_NKEYS, tm), lambda i: (0, 0, 0, i))],
        out_shape=[jax.ShapeDtypeStruct((T, D), F32), jax.ShapeDtypeStruct((T, D), BF16),
                   jax.ShapeDtypeStruct((PEER_HEADS, 2, PEER_NKEYS, T), F32)],
        compiler_params=_cparams(("parallel",)),
        name="outproj_query",
    )(x2, mh, ma, ms, w_out, ffn_w.reshape(1, D), wq, sk_pad)


_NSEL = PEER_TOPK + 1
_PAIRS = [(i, j) for i in range(_NSEL) for j in range(_NSEL) if (i + 1) * (j + 1) <= _NSEL]


def _top_values(v, k):
    out = []
    rank = jnp.full(v.shape, float(k), F32)
    for r in range(k):
        m = jnp.max(v, axis=0, keepdims=True)
        out.append(m)
        hit = v == m
        rank = jnp.where(hit, float(r), rank)
        v = jnp.where(hit, NEG, v)
    return out, rank


def _pack_pair(lo, hi):
    lo_b = lax.bitcast_convert_type(lo.astype(BF16).astype(F32), jnp.uint32) >> 16
    hi_b = lax.bitcast_convert_type(hi.astype(BF16).astype(F32), jnp.uint32) & jnp.uint32(0xFFFF0000)
    return hi_b | lo_b


def _route_kernel(sc_ref, n1_ref, u1_ref, rk2_ref, u2_ref):
    tt = sc_ref.shape[-1]
    row8 = _iota((8, tt), 0)

    def head(hd):
        s1 = sc_ref[hd, 0]
        s2 = sc_ref[hd, 1]
        a, _ = _top_values(s1, _NSEL)
        b, rank2 = _top_values(s2, _NSEL)
        groups = []
        for g0 in range(0, len(_PAIRS), 8):
            cg = jnp.full((8, tt), NEG, F32)
            for p, (i, j) in enumerate(_PAIRS[g0:g0 + 8]):
                cg = jnp.where(row8 == p, a[i] + b[j], cg)
            groups.append(cg)
        best, _ = _top_values(jnp.concatenate(groups, axis=0), _NSEL)
        zsum = jnp.ones((1, tt), F32)
        for r in range(1, PEER_TOPK):
            zsum = zsum + jnp.exp(best[r] - best[0])
        cut = 0.5 * (best[PEER_TOPK - 1] + best[PEER_TOPK])
        need = cut - s1
        n1 = jnp.zeros_like(s1)
        for r in range(PEER_TOPK):
            n1 = n1 + jnp.where(b[r] >= need, 1.0, 0.0)
        return n1, jnp.exp(s1 - a[0]) / zsum, rank2, jnp.exp(s2 - b[0])

    def pair(hp, carry):
        lo = head(2 * hp)
        hi = head(2 * hp + 1)
        for ref, x, y in zip((n1_ref, u1_ref, rk2_ref, u2_ref), lo, hi):
            ref[hp] = _pack_pair(x, y)
        return carry

    lax.fori_loop(0, PEER_HEADS // 2, pair, 0)


def _route(sc, tt=256):
    T = sc.shape[-1]
    blk = pl.BlockSpec((PEER_HEADS // 2, PEER_NKEYS, tt), lambda i: (0, 0, i))
    shp = jax.ShapeDtypeStruct((PEER_HEADS // 2, PEER_NKEYS, T), jnp.uint32)
    return pl.pallas_call(
        _route_kernel,
        grid=(T // tt,),
        in_specs=[pl.BlockSpec((PEER_HEADS, 2, PEER_NKEYS, tt), lambda i: (0, 0, 0, i))],
        out_specs=[blk, blk, blk, blk],
        out_shape=[shp, shp, shp, shp],
        compiler_params=_cparams(("parallel",)),
        name="peer_route",
    )(sc)


_GELU_C1 = -2.0 * math.sqrt(2.0 / math.pi)
_GELU_C2 = _GELU_C1 * 0.044715
_PEER_EB = 256
_PEER_LANES = 128
_PEER_ROWS = 32
_PEER_FLAGS = None


def _as_halves(words):
    return pltpu.bitcast(words, BF16)


def _peer_kernel(nj, xn_ref, dn_ref, upt_ref, n1_ref, u1_ref, rk2_ref, u2_ref, h_ref, o_ref,
                 acc_ref, at0_ref, at1_ref, wg0_ref, wg1_ref):
    j = pl.program_id(1)
    tt = xn_ref.shape[0]
    te = dn_ref.shape[0]
    nb = te // _PEER_EB
    nh = tt // _PEER_TOK
    gpb = _PEER_EB // PEER_NKEYS
    at = (at0_ref, at1_ref)
    wg = (wg0_ref, wg1_ref)

    @pl.when(j == 0)
    def _():
        acc_ref[...] = jnp.zeros_like(acc_ref)

    def scores(k, at_ref):
        rows = pl.ds(pl.multiple_of(k * _PEER_EB, _PEER_EB), _PEER_EB)
        for hf in range(nh):
            at_ref[hf] = lax.dot_general(dn_ref[rows, :], xn_ref[hf * _PEER_TOK:(hf + 1) * _PEER_TOK, :],
                                         (((1,), (1,)), ((), ())), preferred_element_type=F32)

    def weigh(k, at_ref, wg_ref):
        for g in range(gpb):
            c = (j * nb + k) * gpb + g
            er = slice(g * PEER_NKEYS, (g + 1) * PEER_NKEYS)
            for hf in range(nh):
                tw = slice(hf * _PEER_TOK, (hf + 1) * _PEER_TOK)
                n1s = [n1_ref[hp, pl.ds(c, 1), tw] for hp in range(PEER_HEADS // 2)]
                u1s = [u1_ref[hp, pl.ds(c, 1), tw] for hp in range(PEER_HEADS // 2)]
                for lt in range(_PEER_TOK // _PEER_LANES):
                    ls = slice(lt * _PEER_LANES, (lt + 1) * _PEER_LANES)
                    tl = slice(hf * _PEER_TOK + lt * _PEER_LANES, hf * _PEER_TOK + (lt + 1) * _PEER_LANES)
                    w = None
                    for hp in range(PEER_HEADS // 2):
                        n1 = _as_halves(jnp.broadcast_to(n1s[hp][:, ls], (PEER_NKEYS, _PEER_LANES)))
                        u1 = _as_halves(jnp.broadcast_to(u1s[hp][:, ls], (PEER_NKEYS, _PEER_LANES)))
                        rk2 = _as_halves(rk2_ref[hp, :, tl])
                        u2 = _as_halves(u2_ref[hp, :, tl])
                        t = jnp.where(rk2 < n1, u2, jnp.zeros((), BF16)) * u1
                        w = t if w is None else w + t
                    wu = pltpu.bitcast(w, jnp.uint32)
                    wf = (lax.bitcast_convert_type(wu << 16, F32)
                          + lax.bitcast_convert_type(wu & jnp.uint32(0xFFFF0000), F32))
                    a = at_ref[hf, er, ls]
                    gl = a / (1.0 + jnp.exp(a * (_GELU_C1 + _GELU_C2 * (a * a))))
                    wg_ref[hf, er, ls] = (wf * gl).astype(BF16)

    def project(k, wg_ref):
        for hf in range(nh):
            tw = slice(hf * _PEER_TOK, (hf + 1) * _PEER_TOK)
            acc_ref[:, tw] += jnp.dot(upt_ref[k], wg_ref[hf], preferred_element_type=F32)

    scores(0, at[0])
    scores(1, at[1])
    weigh(0, at[0], wg[0])

    def body(m, carry):
        k = 2 * m + 1
        scores(k + 1, at[0])
        weigh(k, at[1], wg[1])
        project(k - 1, wg[0])
        scores(k + 2, at[1])
        weigh(k + 1, at[0], wg[0])
        project(k, wg[1])
        return carry

    lax.fori_loop(0, nb // 2 - 1, body, 0)
    weigh(nb - 1, at[1], wg[1])
    project(nb - 2, wg[0])
    project(nb - 1, wg[1])

    @pl.when(j == nj - 1)
    def _():
        o_ref[...] = h_ref[...] + acc_ref[...].T


def _peer(xn, down_b, layer, up_t3, n1, u1, rk2, u2, h, tt=512, te=2048):
    T, D = xn.shape
    E = down_b.shape[1]
    nj = E // te
    nh = tt // _PEER_TOK
    rt = pl.BlockSpec((PEER_HEADS // 2, PEER_NKEYS, tt), lambda i, j: (0, 0, i))
    return pl.pallas_call(
        functools.partial(_peer_kernel, nj),
        grid=(T // tt, nj),
        in_specs=[pl.BlockSpec((tt, D), lambda i, j: (i, 0)),
                  pl.BlockSpec((None, te, D), lambda i, j: (layer, j, 0)),
                  pl.BlockSpec((te // _PEER_EB, D, _PEER_EB), lambda i, j: (j, 0, 0)),
                  rt, rt, rt, rt,
                  pl.BlockSpec((tt, D), lambda i, j: (i, 0))],
        out_specs=pl.BlockSpec((tt, D), lambda i, j: (i, 0)),
        out_shape=jax.ShapeDtypeStruct((T, D), F32),
        scratch_shapes=[pltpu.VMEM((D, tt), F32),
                        pltpu.VMEM((nh, _PEER_EB, _PEER_TOK), F32), pltpu.VMEM((nh, _PEER_EB, _PEER_TOK), F32),
                        pltpu.VMEM((nh, _PEER_EB, _PEER_TOK), BF16), pltpu.VMEM((nh, _PEER_EB, _PEER_TOK), BF16)],
        compiler_params=_cparams(("parallel", "arbitrary"), flags=_PEER_FLAGS),
        name="peer_experts",
    )(xn, down_b, up_t3, n1, u1, rk2, u2, h)


def _transpose_cast_kernel(x_ref, o_ref):
    o_ref[0] = x_ref[...].T.astype(o_ref.dtype)


def _transpose_blocks_bf16(x, layer, tr=_PEER_EB):
    _, R, C = x.shape
    return pl.pallas_call(
        _transpose_cast_kernel,
        grid=(R // tr,),
        in_specs=[pl.BlockSpec((None, tr, C), lambda i: (layer, i, 0))],
        out_specs=pl.BlockSpec((1, C, tr), lambda i: (i, 0, 0)),
        out_shape=jax.ShapeDtypeStruct((R // tr, C, tr), BF16),
        compiler_params=_cparams(("parallel",)),
        name="transpose_cast",
    )(x)


def _rep(v, n):
    return jnp.repeat(v, n, axis=-1)


def _layer(x2, B, S, layer, p, bias_all):
    T, D = x2.shape
    uh, ua, us = _inproj(x2, p["attn_norm_w"], p["w_in_pad"], layer)

    mh = _hgrn(uh.reshape(B, S, HG_COLS), p["hgrn_lower_bounds"],
               jnp.tile(p["hgrn_norm_w"], HGRN_HEADS).reshape(1, HGRN_W), layer)
    qw = jnp.tile(p["q_norm_w"], ATTN_HEADS).reshape(1, ATTN_W)
    kw = jnp.tile(p["k_norm_w"], ATTN_HEADS).reshape(1, ATTN_W)
    ma = _attn(ua.reshape(B, S, AT_COLS), qw, kw, bias_all)
    ms = _ssd(us.reshape(B, S, SS_COLS), p["ssm_conv_w"], p["ssm_conv_b"].reshape(1, -1),
              _rep(p["ssm_dt_bias"], HEAD_DIM).reshape(1, -1), _rep(p["ssm_a_log"], HEAD_DIM).reshape(1, -1),
              _rep(p["ssm_d"], HEAD_DIM).reshape(1, -1), p["ssm_norm_w"].reshape(1, -1))

    sk = p["peer_sub_keys"]
    zk = jnp.zeros_like(sk[0])
    sk_pad = jnp.stack([jnp.concatenate([sk[0], zk], axis=1), jnp.concatenate([zk, sk[1]], axis=1)])
    h, xn, sc = _outproj(x2, mh.reshape(T, HGRN_W), ma.reshape(T, ATTN_W), ms.reshape(T, SSM_W),
                         p["w_out"].astype(BF16), p["ffn_norm_w"], p["peer_w_query"].astype(BF16), sk_pad)
    n1, u1, rk2, u2 = _route(sc)
    return _peer(xn, p["peer_down_b"], layer, _transpose_blocks_bf16(p["peer_up"], layer), n1, u1, rk2, u2, h)


_PER_LAYER = ("attn_norm_w", "hgrn_norm_w", "q_norm_w", "k_norm_w", "ssm_conv_w", "ssm_conv_b",
              "ssm_dt_bias", "ssm_a_log", "ssm_d", "ssm_norm_w", "w_out", "ffn_norm_w", "peer_w_query",
              "peer_sub_keys")


def kernel(x, attn_norm_w, w_in, hgrn_lower_bounds, hgrn_norm_w, q_norm_w, k_norm_w, rel_bias, ssm_conv_w, ssm_conv_b, ssm_dt_bias, ssm_a_log, ssm_d, ssm_norm_w, w_out, ffn_norm_w, peer_w_query, peer_sub_keys, peer_down, peer_up):
    stacked = dict(attn_norm_w=attn_norm_w, w_in=w_in, hgrn_norm_w=hgrn_norm_w, q_norm_w=q_norm_w,
                   k_norm_w=k_norm_w, ssm_conv_w=ssm_conv_w, ssm_conv_b=ssm_conv_b, ssm_dt_bias=ssm_dt_bias,
                   ssm_a_log=ssm_a_log, ssm_d=ssm_d, ssm_norm_w=ssm_norm_w, w_out=w_out, ffn_norm_w=ffn_norm_w,
                   peer_w_query=peer_w_query, peer_sub_keys=peer_sub_keys, peer_down=peer_down, peer_up=peer_up)
    B, S, D = x.shape
    bias_all = _bias_tables(rel_bias)
    down_b = peer_down.astype(BF16)
    lane_pad = -w_in.shape[2] % 128
    w_in_pad = jnp.pad(w_in, ((0, 0), (0, 0), (0, lane_pad))).astype(BF16)
    x2 = x.reshape(B * S, D)
    for layer in range(w_in.shape[0]):
        p = {k: stacked[k][layer] for k in _PER_LAYER}
        p["w_in_pad"] = w_in_pad
        p["hgrn_lower_bounds"] = hgrn_lower_bounds
        p["peer_down_b"] = down_b
        p["peer_up"] = peer_up
        x2 = _layer(x2, B, S, layer, p, bias_all)
    return x2.reshape(B, S, D)
```

```python
import functools
import math

import numpy as np
import jax
import jax.numpy as jnp
from jax import lax
from jax.experimental import pallas as pl
from jax.experimental.pallas import tpu as pltpu

F32 = jnp.float32
BF16 = jnp.bfloat16
HIGHEST = lax.Precision.HIGHEST
NEG = -1e30
EPS = 1e-6

HEAD_DIM = 64
CHUNK = 64
HGRN_HEADS = 4
HGRN_W = HGRN_HEADS * HEAD_DIM
ATTN_HEADS = 6
ATTN_W = ATTN_HEADS * HEAD_DIM
ATTN_BLOCK = 128
DILATED_PAIRS = ((128, 1), (512, 4), (2048, 16))
NUM_BUCKETS = 32
MAX_DISTANCE = 2048
SSM_HEADS = 6
SSM_W = SSM_HEADS * HEAD_DIM
SSM_GROUPS = 2
SSM_STATE = 128
SSM_GN = SSM_GROUPS * SSM_STATE
SSM_CONV = 4
SSM_CONV_CH = SSM_W + 2 * SSM_GN
PEER_HEADS = 8
PEER_NKEYS = 128
PEER_HALF = 64
PEER_TOPK = 16
_PEER_TOK = 256
HG_COLS = 4 * HGRN_W
AT_COLS = 3 * ATTN_W
SS_COLS = SSM_W + SSM_CONV_CH + SSM_W

VMEM_LIMIT = 56 * 1024 * 1024


def _cparams(sem, flags=None):
    return pltpu.CompilerParams(dimension_semantics=sem, vmem_limit_bytes=VMEM_LIMIT, flags=flags)


def _iota(shape, dim):
    return lax.broadcasted_iota(jnp.int32, shape, dim)


def _mm(a, b):
    return jnp.dot(a.astype(BF16), b.astype(BF16), preferred_element_type=F32)


def _mm_nt(a, b):
    return lax.dot_general(a.astype(BF16), b.astype(BF16), (((1,), (1,)), ((), ())),
                           preferred_element_type=F32)


def _mm_tn(a, b):
    return lax.dot_general(a.astype(BF16), b.astype(BF16), (((0,), (0,)), ((), ())),
                           preferred_element_type=F32)


def _mm_hi(a, b):
    return jnp.dot(a, b, precision=HIGHEST, preferred_element_type=F32)


def _sigmoid(x):
    return 1.0 / (1.0 + jnp.exp(-x))


def _silu(x):
    return x * _sigmoid(x)


def _softplus(x):
    return jnp.maximum(x, 0.0) + jnp.log(1.0 + jnp.exp(-jnp.abs(x)))


def _seg_ones(n, seg):
    r = _iota((n, n), 0)
    c = _iota((n, n), 1)
    same = None
    for s in range(n // seg):
        lo, hi = s * seg, (s + 1) * seg
        t = (r >= lo) & (r < hi) & (c >= lo) & (c < hi)
        same = t if same is None else (same | t)
    return jnp.where(same, 1.0, 0.0).astype(F32)


def _lane_seg_mask(width, seg, idx):
    l = _iota((1, width), 1)
    return jnp.where((l >= idx * seg) & (l < (idx + 1) * seg), 1.0, 0.0).astype(F32)


def _inproj_kernel(x_ref, nw_ref, w_ref, oh_ref, oa_ref, os_ref):
    x = x_ref[...]
    xn = x * lax.rsqrt(jnp.mean(x * x, axis=-1, keepdims=True) + EPS) * nw_ref[...]
    xb = xn.astype(BF16)
    c0, c1, c2 = HG_COLS, HG_COLS + AT_COLS, HG_COLS + AT_COLS + SSM_W + SSM_CONV_CH
    oh_ref[...] = jnp.dot(xb, w_ref[:, 0:c0], preferred_element_type=F32)
    oa_ref[...] = jnp.dot(xb, w_ref[:, c0:c1], preferred_element_type=F32)
    os_ref[:, 0:c2 - c1] = jnp.dot(xb, w_ref[:, c1:c2], preferred_element_type=F32)
    dt = jnp.dot(xb, w_ref[:, c2:], preferred_element_type=F32)
    lanes = w_ref.shape[1] - c2
    expand = jnp.where((_iota((lanes, SSM_W), 1) >> 6) == _iota((lanes, SSM_W), 0), 1.0, 0.0).astype(F32)
    os_ref[:, c2 - c1:] = _mm_hi(dt, expand)


def _inproj(x2, norm_w, w_pad, layer, tm=256):
    T, D = x2.shape
    ncol = w_pad.shape[2]
    return pl.pallas_call(
        _inproj_kernel,
        grid=(T // tm,),
        in_specs=[pl.BlockSpec((tm, D), lambda i: (i, 0)),
                  pl.BlockSpec((1, D), lambda i: (0, 0)),
                  pl.BlockSpec((None, D, ncol), lambda i: (layer, 0, 0))],
        out_specs=[pl.BlockSpec((tm, HG_COLS), lambda i: (i, 0)),
                   pl.BlockSpec((tm, AT_COLS), lambda i: (i, 0)),
                   pl.BlockSpec((tm, SS_COLS), lambda i: (i, 0))],
        out_shape=[jax.ShapeDtypeStruct((T, HG_COLS), F32),
                   jax.ShapeDtypeStruct((T, AT_COLS), F32),
                   jax.ShapeDtypeStruct((T, SS_COLS), F32)],
        compiler_params=_cparams(("parallel",)),
        name="inproj",
    )(x2, norm_w.reshape(1, D), w_pad)


def _hgrn_kernel(layer, nchunk, u_ref, lbs_ref, nw_ref, o_ref, st_ref):
    W = HGRN_W

    @pl.when(pl.program_id(1) == 0)
    def _():
        st_ref[...] = jnp.zeros_like(st_ref)

    lbs = lbs_ref[...]
    e = jnp.exp(lbs - jnp.max(lbs, axis=0, keepdims=True))
    sm = e / jnp.sum(e, axis=0, keepdims=True)
    lb = jnp.zeros((1, W), F32)
    for j in range(1, layer + 1):
        lb = lb + sm[j:j + 1]
    nw = nw_ref[...]

    tril = jnp.where(_iota((CHUNK, CHUNK), 0) >= _iota((CHUNK, CHUNK), 1), 1.0, 0.0).astype(F32)
    row = _iota((CHUNK, W), 0)
    hm = [_lane_seg_mask(W, HEAD_DIM, h) for h in range(HGRN_HEADS)]
    hm3 = [jnp.concatenate([m, m, m], axis=1) for m in hm]
    seg = _seg_ones(W, HEAD_DIM)
    seg_b = seg.astype(BF16)
    tt16 = _iota((16, W), 0)
    z32 = jnp.zeros((32, W), F32)
    in_q2 = (row & 16) != 0

    def chunk(c, carry):
        r0 = pl.multiple_of(c * CHUNK, CHUNK)
        u = u_ref[0, pl.ds(r0, CHUNK), :]
        q = u[:, 0:W] * (HEAD_DIM ** -0.5)
        f = u[:, W:2 * W]
        iv = u[:, 2 * W:3 * W]
        g = u[:, 3 * W:4 * W]
        forget = lb + (1.0 - lb) * _sigmoid(f)
        logf = jnp.log(forget)
        kk = 1.0 - forget
        b = _mm_hi(tril, logf)
        b16, b32, b48, bend = b[16:17], b[32:33], b[48:49], b[63:64]

        q1 = q * jnp.exp(jnp.where(row >= 32, b - b32, NEG))
        k1 = kk * jnp.exp(jnp.where(row < 32, b32 - b, NEG))
        ref2 = jnp.where(row < 32, b16, b48)
        q2 = q * jnp.exp(jnp.where(in_q2, b - ref2, NEG))
        k2 = kk * jnp.exp(jnp.where(in_q2, NEG, ref2 - b))
        qcat = jnp.concatenate([q1, jnp.concatenate([q2[:32], z32], 0), jnp.concatenate([z32, q2[32:]], 0)], 1)
        kcat = jnp.concatenate([k1, jnp.concatenate([k2[:32], z32], 0), jnp.concatenate([z32, k2[32:]], 0)], 1)
        qstack = jnp.concatenate([qcat * hm3[h] for h in range(HGRN_HEADS)], axis=0)
        a = _mm_nt(qstack, kcat)
        ofull = _mm(a, iv)
        o = ofull[0:CHUNK] * hm[0]
        for h in range(1, HGRN_HEADS):
            o = o + ofull[h * CHUNK:(h + 1) * CHUNK] * hm[h]

        st = st_ref[...]
        o = o + _mm_nt(q * jnp.exp(b), st)

        diag = []
        for j in range(4):
            bR = b[16 * j:16 * j + 16]
            qR = q[16 * j:16 * j + 16]
            ps = []
            for s in range(16):
                r = 16 * j + s
                ps.append(qR * kk[r:r + 1] * jnp.exp(jnp.where(tt16 >= s, bR - b[r:r + 1], NEG)))
            ab = jnp.dot(jnp.concatenate(ps, axis=0).astype(BF16), seg_b, preferred_element_type=F32)
            od = ab[0:16] * iv[16 * j:16 * j + 1]
            for s in range(1, 16):
                od = od + ab[16 * s:16 * s + 16] * iv[16 * j + s:16 * j + s + 1]
            diag.append(od)
        o = o + jnp.concatenate(diag, axis=0)

        kend = kk * jnp.exp(bend - b)
        st_ref[...] = st * jnp.exp(bend) + _mm_tn(iv, kend) * seg

        ss = _mm(o * o, seg_b) * (1.0 / HEAD_DIM)
        o_ref[0, pl.ds(r0, CHUNK), :] = o * lax.rsqrt(ss + EPS) * nw * _silu(g)
        return carry

    lax.fori_loop(0, nchunk, chunk, 0)


def _hgrn(uh, lbs, norm_w, layer, blk=256):
    B, S, _ = uh.shape
    return pl.pallas_call(
        functools.partial(_hgrn_kernel, layer, blk // CHUNK),
        grid=(B, S // blk),
        in_specs=[pl.BlockSpec((1, blk, HG_COLS), lambda b, s: (b, s, 0)),
                  pl.BlockSpec(lbs.shape, lambda b, s: (0, 0)),
                  pl.BlockSpec((1, HGRN_W), lambda b, s: (0, 0))],
        out_specs=pl.BlockSpec((1, blk, HGRN_W), lambda b, s: (b, s, 0)),
        out_shape=jax.ShapeDtypeStruct((B, S, HGRN_W), F32),
        scratch_shapes=[pltpu.VMEM((HGRN_W, HGRN_W), F32)],
        compiler_params=_cparams(("parallel", "arbitrary")),
        name="hgrn2",
    )(uh, lbs, norm_w)


def _ssd_kernel(nchunk, u_ref, cw_ref, cb_ref, dtb_ref, alog_ref, d_ref, nw_ref, o_ref,
                st_ref, ext_ref, xc_ref):
    W = SSM_W
    blk = nchunk * CHUNK
    first = pl.program_id(1) == 0

    @pl.when(first)
    def _():
        st_ref[...] = jnp.zeros_like(st_ref)
        ext_ref[0:8, :] = jnp.zeros((8, SSM_CONV_CH), F32)

    @pl.when(jnp.logical_not(first))
    def _():
        ext_ref[0:8, :] = ext_ref[blk:blk + 8, :]

    ext_ref[8:8 + blk, :] = u_ref[0, :, W:W + SSM_CONV_CH]
    cw = cw_ref[...]
    conv = cb_ref[...] + cw[0:1] * ext_ref[5:5 + blk, :]
    for j in range(1, SSM_CONV):
        conv = conv + cw[j:j + 1] * ext_ref[5 + j:5 + j + blk, :]
    xc_ref[...] = _silu(conv)

    dtb = dtb_ref[...]
    a_neg = -jnp.exp(alog_ref[...])
    dsk = d_ref[...]
    nw = nw_ref[...]
    tril = jnp.where(_iota((CHUNK, CHUNK), 0) >= _iota((CHUNK, CHUNK), 1), 1.0, 0.0).astype(F32)
    t_i = _iota((CHUNK, W), 0)
    s_i = _iota((CHUNK, W), 1) & (HEAD_DIM - 1)
    strict = jnp.where(t_i > s_i, 1.0, 0.0).astype(F32)
    lower = t_i >= s_i
    hm = [_lane_seg_mask(W, HEAD_DIM, h) for h in range(SSM_HEADS)]
    zB = jnp.zeros((CHUNK, SSM_STATE), F32)
    gr = _iota((SSM_GN, W), 0) >= SSM_STATE
    gc = _iota((SSM_GN, W), 1) >= (SSM_HEADS // SSM_GROUPS) * HEAD_DIM
    gmask = jnp.where(gr == gc, 1.0, 0.0).astype(F32)
    gw = W // SSM_GROUPS
    seg = _seg_ones(W, gw).astype(BF16)

    def chunk(c, carry):
        r0 = pl.multiple_of(c * CHUNK, CHUNK)
        z = u_ref[0, pl.ds(r0, CHUNK), 0:W]
        dtr = u_ref[0, pl.ds(r0, CHUNK), W + SSM_CONV_CH:W + SSM_CONV_CH + W]
        xc = xc_ref[pl.ds(r0, CHUNK), :]
        xs = xc[:, 0:W]
        bm = xc[:, W:W + SSM_GN]
        cm = xc[:, W + SSM_GN:W + 2 * SSM_GN]
        dt = _softplus(dtr + dtb)
        a = dt * a_neg
        dd = _mm_hi(tril, jnp.concatenate([a * strict, a], axis=1))
        dseg = dd[:, 0:W]
        cum = dd[:, W:2 * W]
        decay = jnp.exp(jnp.where(lower, dseg, NEG))
        b0 = jnp.concatenate([bm[:, 0:SSM_STATE], zB], axis=1)
        b1 = jnp.concatenate([zB, bm[:, SSM_STATE:]], axis=1)
        bstack = jnp.concatenate([b0, b0, b0, b1, b1, b1], axis=0)
        scores = _mm_nt(cm, bstack) * decay
        xdt = xs * dt
        xbd = jnp.concatenate([xdt * hm[h] for h in range(SSM_HEADS)], axis=0)
        y = _mm(scores, xbd)
        st = st_ref[...]
        y = y + _mm(cm, st) * jnp.exp(cum)
        cend = cum[CHUNK - 1:CHUNK]
        to_end = jnp.exp(cend - cum) * dt
        st_ref[...] = st * jnp.exp(cend) + _mm_tn(bm, to_end * xs) * gmask
        y = y + dsk * xs
        y = y * _silu(z)
        ss = _mm(y * y, seg) * (1.0 / gw)
        o_ref[0, pl.ds(r0, CHUNK), :] = y * lax.rsqrt(ss + EPS) * nw
        return carry

    lax.fori_loop(0, nchunk, chunk, 0)


def _ssd(us, conv_w, conv_b, dtb, alog, dsk, norm_w, blk=256):
    B, S, _ = us.shape
    vec = lambda n: pl.BlockSpec((1, n), lambda b, s: (0, 0))
    return pl.pallas_call(
        functools.partial(_ssd_kernel, blk // CHUNK),
        grid=(B, S // blk),
        in_specs=[pl.BlockSpec((1, blk, SS_COLS), lambda b, s: (b, s, 0)),
                  pl.BlockSpec((SSM_CONV, SSM_CONV_CH), lambda b, s: (0, 0)),
                  vec(SSM_CONV_CH), vec(SSM_W), vec(SSM_W), vec(SSM_W), vec(SSM_W)],
        out_specs=pl.BlockSpec((1, blk, SSM_W), lambda b, s: (b, s, 0)),
        out_shape=jax.ShapeDtypeStruct((B, S, SSM_W), F32),
        scratch_shapes=[pltpu.VMEM((SSM_GN, SSM_W), F32),
                        pltpu.VMEM((blk + 8, SSM_CONV_CH), F32),
                        pltpu.VMEM((blk, SSM_CONV_CH), F32)],
        compiler_params=_cparams(("parallel", "arbitrary")),
        name="ssd",
    )(us, conv_w, conv_b, dtb, alog, dsk, norm_w)


def _t5_bucket(dist):
    max_exact = NUM_BUCKETS // 2
    d = np.maximum(dist, 1).astype(np.float32)
    large = max_exact + (np.log(d / max_exact) / np.log(MAX_DISTANCE / max_exact)
                         * (NUM_BUCKETS - max_exact)).astype(np.int32)
    large = np.minimum(large, NUM_BUCKETS - 1)
    return np.where(dist < max_exact, dist, large).astype(np.int32)


def _bucket_tables():
    qi = np.arange(ATTN_BLOCK)[:, None]
    kj = np.arange(2 * ATTN_BLOCK)[None, :]
    delta = qi - kj + ATTN_BLOCK
    return np.stack([_t5_bucket(np.maximum(delta, 0) * dil) for _, dil in DILATED_PAIRS])


def _bias_kernel(rb_ref, bk_ref, o_ref):
    bk = bk_ref[0]
    for h in range(ATTN_HEADS):
        acc = jnp.zeros(bk.shape, F32)
        for b in range(NUM_BUCKETS):
            acc = jnp.where(bk == b, rb_ref[b, h], acc)
        o_ref[0, h] = acc


def _bias_tables(rel_bias):
    bk = jnp.asarray(_bucket_tables())
    nbr = len(DILATED_PAIRS)
    return pl.pallas_call(
        _bias_kernel,
        grid=(nbr,),
        in_specs=[pl.BlockSpec(memory_space=pltpu.SMEM),
                  pl.BlockSpec((1, ATTN_BLOCK, 2 * ATTN_BLOCK), lambda i: (i, 0, 0))],
        out_specs=pl.BlockSpec((1, ATTN_HEADS, ATTN_BLOCK, 2 * ATTN_BLOCK), lambda i: (i, 0, 0, 0)),
        out_shape=jax.ShapeDtypeStruct((nbr, ATTN_HEADS, ATTN_BLOCK, 2 * ATTN_BLOCK), F32),
        compiler_params=_cparams(("parallel",)),
        name="attn_bias",
    )(rel_bias, bk)


def _attn_kernel(ua_ref, qw_ref, kw_ref, bias_ref, o_ref, qn_ref, kn_ref, vv_ref, num_ref, den_ref, mx_ref):
    W = ATTN_W
    S = ua_ref.shape[1]
    blk = ATTN_BLOCK
    seg = _seg_ones(W, HEAD_DIM).astype(BF16)
    qw = qw_ref[...]
    kw = kw_ref[...]

    def nrm(x, w):
        ss = _mm(x * x, seg) * (1.0 / HEAD_DIM)
        return x * lax.rsqrt(ss + EPS) * w

    npair = ATTN_HEADS // 2
    pw = 2 * HEAD_DIM

    def norm_block(i, carry):
        rows = pl.ds(pl.multiple_of(i * blk, blk), blk)
        qn = nrm(ua_ref[0, rows, 0:W], qw) * (HEAD_DIM ** -0.5)
        kn = nrm(ua_ref[0, rows, W:2 * W], kw)
        for pp in range(npair):
            qn_ref[pp, rows, :] = qn[:, pp * pw:(pp + 1) * pw]
            kn_ref[pp, rows, :] = kn[:, pp * pw:(pp + 1) * pw]
            vv_ref[pp, rows, :] = ua_ref[0, rows, 2 * W + pp * pw:2 * W + (pp + 1) * pw]
        return carry

    lax.fori_loop(0, S // blk, norm_block, 0)

    lane = _iota((blk, pw), 1)
    low = lane < HEAD_DIM
    half = [jnp.where(low[0:1], 1.0, 0.0).astype(F32), jnp.where(low[0:1], 0.0, 1.0).astype(F32)]

    order = sorted(range(len(DILATED_PAIRS)), key=lambda i: -DILATED_PAIRS[i][1])
    for br in order:
        window, dil = DILATED_PAIRS[br]
        first = br == order[0]
        n_back = window // dil
        nb = S // dil // blk
        has_prev = nb > 1
        nk = 2 * blk if has_prev else blk
        qi = _iota((blk, nk), 0)
        kj = _iota((blk, nk), 1) + (0 if has_prev else blk)
        delta = qi - kj + blk
        band = (delta >= 0) & (delta <= n_back)

        def rows_at(start, dil=dil):
            if dil == 1:
                return pl.ds(pl.multiple_of(start, blk), blk)
            return pl.ds(start, blk, stride=dil)

        def block(idx, carry, br=br, dil=dil, nb=nb, has_prev=has_prev, kj=kj, band=band, rows_at=rows_at,
                  first=first):
            r = idx // nb
            n = idx % nb
            rq = rows_at(r + dil * blk * n)
            valid = band
            if has_prev:
                rp = rows_at(r + dil * blk * jnp.maximum(n - 1, 0))
                valid = band & (kj >= jnp.where(n > 0, 0, blk))
            for pp in range(npair):
                qp = qn_ref[pp, rq, :]
                kp = kn_ref[pp, rq, :]
                vp = vv_ref[pp, rq, :]
                if has_prev:
                    kp = jnp.concatenate([kn_ref[pp, rp, :], kp], axis=0)
                    vp = jnp.concatenate([vv_ref[pp, rp, :], vp], axis=0)
                oh, lh = [], []
                for hh in range(2):
                    bias = bias_ref[br, 2 * pp + hh]
                    if not has_prev:
                        bias = bias[:, blk:]
                    s = jnp.where(valid, _mm_nt(qp * half[hh], kp) + bias, NEG)
                    m = jnp.max(s, axis=-1, keepdims=True)
                    p = jnp.exp(s - m)
                    l = jnp.sum(p, axis=-1, keepdims=True)
                    oh.append(_mm(p, vp) / l)
                    lh.append(m + jnp.log(l))
                o = jnp.where(low, oh[0], oh[1])
                lse = jnp.where(low, lh[0], lh[1])
                if first:
                    num_ref[pp, rq, :] = o
                    den_ref[pp, rq, :] = jnp.ones_like(o)
                    mx_ref[pp, rq, :] = lse
                else:
                    m_old = mx_ref[pp, rq, :]
                    m_new = jnp.maximum(m_old, lse)
                    wa = jnp.exp(m_old - m_new)
                    wb = jnp.exp(lse - m_new)
                    num_ref[pp, rq, :] = num_ref[pp, rq, :] * wa + wb * o
                    den_ref[pp, rq, :] = den_ref[pp, rq, :] * wa + wb
                    mx_ref[pp, rq, :] = m_new
            return carry

        lax.fori_loop(0, dil * nb, block, 0)

    def finish(i, carry):
        rows = pl.ds(pl.multiple_of(i * blk, blk), blk)
        for pp in range(npair):
            o_ref[0, rows, pp * pw:(pp + 1) * pw] = num_ref[pp, rows, :] / den_ref[pp, rows, :]
        return carry

    lax.fori_loop(0, S // blk, finish, 0)


def _attn(ua3, qw, kw, bias_all):
    B, S, _ = ua3.shape
    nbr = len(DILATED_PAIRS)
    return pl.pallas_call(
        _attn_kernel,
        grid=(B,),
        in_specs=[pl.BlockSpec((1, S, AT_COLS), lambda b: (b, 0, 0)),
                  pl.BlockSpec((1, ATTN_W), lambda b: (0, 0)),
                  pl.BlockSpec((1, ATTN_W), lambda b: (0, 0)),
                  pl.BlockSpec((nbr, ATTN_HEADS, ATTN_BLOCK, 2 * ATTN_BLOCK), lambda b: (0, 0, 0, 0))],
        out_specs=pl.BlockSpec((1, S, ATTN_W), lambda b: (b, 0, 0)),
        out_shape=jax.ShapeDtypeStruct((B, S, ATTN_W), F32),
        scratch_shapes=[pltpu.VMEM((ATTN_HEADS // 2, S, 2 * HEAD_DIM), F32) for _ in range(6)],
        compiler_params=_cparams(("parallel",)),
        name="dilated_attn",
    )(ua3, qw, kw, bias_all)


def _outproj_kernel(x_ref, mh_ref, ma_ref, ms_ref, wo_ref, fw_ref, wq_ref, sk_ref,
                    h_ref, xn_ref, sc_ref):
    W = ATTN_W
    h = (x_ref[...] + _mm(mh_ref[...], wo_ref[0:HGRN_W, :])
         + _mm(ma_ref[...], wo_ref[HGRN_W:HGRN_W + W, :])
         + _mm(ms_ref[...], wo_ref[HGRN_W + W:, :]))
    h_ref[...] = h
    xn = (h * lax.rsqrt(jnp.mean(h * h, axis=-1, keepdims=True) + EPS) * fw_ref[...]).astype(BF16)
    xn_ref[...] = xn
    qp = jnp.dot(xn, wq_ref[...], preferred_element_type=F32)
    for hd in range(PEER_HEADS):
        qh = qp[:, 2 * PEER_HALF * hd:2 * PEER_HALF * (hd + 1)]
        for c in range(2):
            sc_ref[hd, c] = _mm_nt(sk_ref[c], qh)


def _outproj(x2, mh, ma, ms, w_out, ffn_w, wq, sk_pad, tm=256):
    T, D = x2.shape
    row = lambda n: pl.BlockSpec((tm, n), lambda i: (i, 0))
    full = lambda a: pl.BlockSpec(a.shape, lambda i: (0,) * a.ndim)
    return pl.pallas_call(
        _outproj_kernel,
        grid=(T // tm,),
        in_specs=[row(D), row(HGRN_W), row(ATTN_W), row(SSM_W),
                  full(w_out), pl.BlockSpec((1, D), lambda i: (0, 0)), full(wq), full(sk_pad)],
        out_specs=[row(D), row(D),
                   pl.BlockSpec((PEER_HEADS, 2, PEER_NKEYS, tm), lambda i: (0, 0, 0, i))],
        out_shape=[jax.ShapeDtypeStruct((T, D), F32), jax.ShapeDtypeStruct((T, D), BF16),
                   jax.ShapeDtypeStruct((PEER_HEADS, 2, PEER_NKEYS, T), F32)],
        compiler_params=_cparams(("parallel",)),
        name="outproj_query",
    )(x2, mh, ma, ms, w_out, ffn_w.reshape(1, D), wq, sk_pad)


_NSEL = PEER_TOPK + 1
_PAIRS = [(i, j) for i in range(_NSEL) for j in range(_NSEL) if (i + 1) * (j + 1) <= _NSEL]


def _top_values(v, k):
    out = []
    rank = jnp.full(v.shape, float(k), F32)
    for r in range(k):
        m = jnp.max(v, axis=0, keepdims=True)
        out.append(m)
        hit = v == m
        rank = jnp.where(hit, float(r), rank)
        v = jnp.where(hit, NEG, v)
    return out, rank


def _pack_pair(lo, hi):
    lo_b = lax.bitcast_convert_type(lo.astype(BF16).astype(F32), jnp.uint32) >> 16
    hi_b = lax.bitcast_convert_type(hi.astype(BF16).astype(F32), jnp.uint32) & jnp.uint32(0xFFFF0000)
    return hi_b | lo_b


def _route_kernel(sc_ref, n1_ref, u1_ref, rk2_ref, u2_ref):
    tt = sc_ref.shape[-1]
    row8 = _iota((8, tt), 0)

    def head(hd):
        s1 = sc_ref[hd, 0]
        s2 = sc_ref[hd, 1]
        a, _ = _top_values(s1, _NSEL)
        b, rank2 = _top_values(s2, _NSEL)
        groups = []
        for g0 in range(0, len(_PAIRS), 8):
            cg = jnp.full((8, tt), NEG, F32)
            for p, (i, j) in enumerate(_PAIRS[g0:g0 + 8]):
                cg = jnp.where(row8 == p, a[i] + b[j], cg)
            groups.append(cg)
        best, _ = _top_values(jnp.concatenate(groups, axis=0), _NSEL)
        zsum = jnp.ones((1, tt), F32)
        for r in range(1, PEER_TOPK):
            zsum = zsum + jnp.exp(best[r] - best[0])
        cut = 0.5 * (best[PEER_TOPK - 1] + best[PEER_TOPK])
        need = cut - s1
        n1 = jnp.zeros_like(s1)
        for r in range(PEER_TOPK):
            n1 = n1 + jnp.where(b[r] >= need, 1.0, 0.0)
        return n1, jnp.exp(s1 - a[0]) / zsum, rank2, jnp.exp(s2 - b[0])

    def pair(hp, carry):
        lo = head(2 * hp)
        hi = head(2 * hp + 1)
        for ref, x, y in zip((n1_ref, u1_ref, rk2_ref, u2_ref), lo, hi):
            ref[hp] = _pack_pair(x, y)
        return carry

    lax.fori_loop(0, PEER_HEADS // 2, pair, 0)


def _route(sc, tt=256):
    T = sc.shape[-1]
    blk = pl.BlockSpec((PEER_HEADS // 2, PEER_NKEYS, tt), lambda i: (0, 0, i))
    shp = jax.ShapeDtypeStruct((PEER_HEADS // 2, PEER_NKEYS, T), jnp.uint32)
    return pl.pallas_call(
        _route_kernel,
        grid=(T // tt,),
        in_specs=[pl.BlockSpec((PEER_HEADS, 2, PEER_NKEYS, tt), lambda i: (0, 0, 0, i))],
        out_specs=[blk, blk, blk, blk],
        out_shape=[shp, shp, shp, shp],
        compiler_params=_cparams(("parallel",)),
        name="peer_route",
    )(sc)


_GELU_C1 = -2.0 * math.sqrt(2.0 / math.pi)
_GELU_C2 = _GELU_C1 * 0.044715
_PEER_EB = 512
_PEER_LANES = 128
_PEER_ROWS = 32
_PEER_FLAGS = None


def _as_halves(words):
    return pltpu.bitcast(words, BF16)


def _peer_kernel(nj, xn_ref, dn_ref, upt_ref, n1_ref, u1_ref, rk2_ref, u2_ref, h_ref, o_ref,
                 acc_ref, at0_ref, at1_ref, wg0_ref, wg1_ref):
    j = pl.program_id(1)
    tt = xn_ref.shape[0]
    te = dn_ref.shape[0]
    nb = te // _PEER_EB
    nh = tt // _PEER_TOK
    gpb = _PEER_EB // PEER_NKEYS
    at = (at0_ref, at1_ref)
    wg = (wg0_ref, wg1_ref)

    @pl.when(j == 0)
    def _():
        acc_ref[...] = jnp.zeros_like(acc_ref)

    def scores(k, at_ref):
        rows = pl.ds(pl.multiple_of(k * _PEER_EB, _PEER_EB), _PEER_EB)
        for hf in range(nh):
            at_ref[hf] = lax.dot_general(dn_ref[rows, :], xn_ref[hf * _PEER_TOK:(hf + 1) * _PEER_TOK, :],
                                         (((1,), (1,)), ((), ())), preferred_element_type=F32)

    def weigh(k, at_ref, wg_ref):
        for g in range(gpb):
            c = (j * nb + k) * gpb + g
            er = slice(g * PEER_NKEYS, (g + 1) * PEER_NKEYS)
            for hf in range(nh):
                tw = slice(hf * _PEER_TOK, (hf + 1) * _PEER_TOK)
                n1s = [n1_ref[hp, pl.ds(c, 1), tw] for hp in range(PEER_HEADS // 2)]
                u1s = [u1_ref[hp, pl.ds(c, 1), tw] for hp in range(PEER_HEADS // 2)]
                for lt in range(_PEER_TOK // _PEER_LANES):
                    ls = slice(lt * _PEER_LANES, (lt + 1) * _PEER_LANES)
                    tl = slice(hf * _PEER_TOK + lt * _PEER_LANES, hf * _PEER_TOK + (lt + 1) * _PEER_LANES)
                    w = None
                    for hp in range(PEER_HEADS // 2):
                        n1 = _as_halves(jnp.broadcast_to(n1s[hp][:, ls], (PEER_NKEYS, _PEER_LANES)))
                        u1 = _as_halves(jnp.broadcast_to(u1s[hp][:, ls], (PEER_NKEYS, _PEER_LANES)))
                        rk2 = _as_halves(rk2_ref[hp, :, tl])
                        u2 = _as_halves(u2_ref[hp, :, tl])
                        t = jnp.where(rk2 < n1, u2, jnp.zeros((), BF16)) * u1
                        w = t if w is None else w + t
                    wu = pltpu.bitcast(w, jnp.uint32)
                    wf = (lax.bitcast_convert_type(wu << 16, F32)
                          + lax.bitcast_convert_type(wu & jnp.uint32(0xFFFF0000), F32))
                    a = at_ref[hf, er, ls]
                    gl = a / (1.0 + jnp.exp(a * (_GELU_C1 + _GELU_C2 * (a * a))))
                    wg_ref[hf, er, ls] = (wf * gl).astype(BF16)

    def project(k, wg_ref):
        for hf in range(nh):
            tw = slice(hf * _PEER_TOK, (hf + 1) * _PEER_TOK)
            acc_ref[:, tw] += jnp.dot(upt_ref[k], wg_ref[hf], preferred_element_type=F32)

    scores(0, at[0])
    scores(1, at[1])
    weigh(0, at[0], wg[0])

    def body(m, carry):
        k = 2 * m + 1
        scores(k + 1, at[0])
        weigh(k, at[1], wg[1])
        project(k - 1, wg[0])
        scores(k + 2, at[1])
        weigh(k + 1, at[0], wg[0])
        project(k, wg[1])
        return carry

    lax.fori_loop(0, nb // 2 - 1, body, 0)
    weigh(nb - 1, at[1], wg[1])
    project(nb - 2, wg[0])
    project(nb - 1, wg[1])

    @pl.when(j == nj - 1)
    def _():
        o_ref[...] = h_ref[...] + acc_ref[...].T


def _peer(xn, down_b, layer, up_t3, n1, u1, rk2, u2, h, tt=512, te=2048):
    T, D = xn.shape
    E = down_b.shape[1]
    nj = E // te
    nh = tt // _PEER_TOK
    rt = pl.BlockSpec((PEER_HEADS // 2, PEER_NKEYS, tt), lambda i, j: (0, 0, i))
    return pl.pallas_call(
        functools.partial(_peer_kernel, nj),
        grid=(T // tt, nj),
        in_specs=[pl.BlockSpec((tt, D), lambda i, j: (i, 0)),
                  pl.BlockSpec((None, te, D), lambda i, j: (layer, j, 0)),
                  pl.BlockSpec((te // _PEER_EB, D, _PEER_EB), lambda i, j: (j, 0, 0)),
                  rt, rt, rt, rt,
                  pl.BlockSpec((tt, D), lambda i, j: (i, 0))],
        out_specs=pl.BlockSpec((tt, D), lambda i, j: (i, 0)),
        out_shape=jax.ShapeDtypeStruct((T, D), F32),
        scratch_shapes=[pltpu.VMEM((D, tt), F32),
                        pltpu.VMEM((nh, _PEER_EB, _PEER_TOK), F32), pltpu.VMEM((nh, _PEER_EB, _PEER_TOK), F32),
                        pltpu.VMEM((nh, _PEER_EB, _PEER_TOK), BF16), pltpu.VMEM((nh, _PEER_EB, _PEER_TOK), BF16)],
        compiler_params=_cparams(("parallel", "arbitrary"), flags=_PEER_FLAGS),
        name="peer_experts",
    )(xn, down_b, up_t3, n1, u1, rk2, u2, h)


def _transpose_cast_kernel(x_ref, o_ref):
    o_ref[0] = x_ref[...].T.astype(o_ref.dtype)


def _transpose_blocks_bf16(x, layer, tr=_PEER_EB):
    _, R, C = x.shape
    return pl.pallas_call(
        _transpose_cast_kernel,
        grid=(R // tr,),
        in_specs=[pl.BlockSpec((None, tr, C), lambda i: (layer, i, 0))],
        out_specs=pl.BlockSpec((1, C, tr), lambda i: (i, 0, 0)),
        out_shape=jax.ShapeDtypeStruct((R // tr, C, tr), BF16),
        compiler_params=_cparams(("parallel",)),
        name="transpose_cast",
    )(x)


def _rep(v, n):
    return jnp.repeat(v, n, axis=-1)


def _layer(x2, B, S, layer, p, bias_all):
    T, D = x2.shape
    uh, ua, us = _inproj(x2, p["attn_norm_w"], p["w_in_pad"], layer)

    mh = _hgrn(uh.reshape(B, S, HG_COLS), p["hgrn_lower_bounds"],
               jnp.tile(p["hgrn_norm_w"], HGRN_HEADS).reshape(1, HGRN_W), layer)
    qw = jnp.tile(p["q_norm_w"], ATTN_HEADS).reshape(1, ATTN_W)
    kw = jnp.tile(p["k_norm_w"], ATTN_HEADS).reshape(1, ATTN_W)
    ma = _attn(ua.reshape(B, S, AT_COLS), qw, kw, bias_all)
    ms = _ssd(us.reshape(B, S, SS_COLS), p["ssm_conv_w"], p["ssm_conv_b"].reshape(1, -1),
              _rep(p["ssm_dt_bias"], HEAD_DIM).reshape(1, -1), _rep(p["ssm_a_log"], HEAD_DIM).reshape(1, -1),
              _rep(p["ssm_d"], HEAD_DIM).reshape(1, -1), p["ssm_norm_w"].reshape(1, -1))

    sk = p["peer_sub_keys"]
    zk = jnp.zeros_like(sk[0])
    sk_pad = jnp.stack([jnp.concatenate([sk[0], zk], axis=1), jnp.concatenate([zk, sk[1]], axis=1)])
    h, xn, sc = _outproj(x2, mh.reshape(T, HGRN_W), ma.reshape(T, ATTN_W), ms.reshape(T, SSM_W),
                         p["w_out"].astype(BF16), p["ffn_norm_w"], p["peer_w_query"].astype(BF16), sk_pad)
    n1, u1, rk2, u2 = _route(sc)
    return _peer(xn, p["peer_down_b"], layer, _transpose_blocks_bf16(p["peer_up"], layer), n1, u1, rk2, u2, h)


_PER_LAYER = ("attn_norm_w", "hgrn_norm_w", "q_norm_w", "k_norm_w", "ssm_conv_w", "ssm_conv_b",
              "ssm_dt_bias", "ssm_a_log", "ssm_d", "ssm_norm_w", "w_out", "ffn_norm_w", "peer_w_query",
              "peer_sub_keys")


def kernel(x, attn_norm_w, w_in, hgrn_lower_bounds, hgrn_norm_w, q_norm_w, k_norm_w, rel_bias, ssm_conv_w, ssm_conv_b, ssm_dt_bias, ssm_a_log, ssm_d, ssm_norm_w, w_out, ffn_norm_w, peer_w_query, peer_sub_keys, peer_down, peer_up):
    stacked = dict(attn_norm_w=attn_norm_w, w_in=w_in, hgrn_norm_w=hgrn_norm_w, q_norm_w=q_norm_w,
                   k_norm_w=k_norm_w, ssm_conv_w=ssm_conv_w, ssm_conv_b=ssm_conv_b, ssm_dt_bias=ssm_dt_bias,
                   ssm_a_log=ssm_a_log, ssm_d=ssm_d, ssm_norm_w=ssm_norm_w, w_out=w_out, ffn_norm_w=ffn_norm_w,
                   peer_w_query=peer_w_query, peer_sub_keys=peer_sub_keys, peer_down=peer_down, peer_up=peer_up)
    B, S, D = x.shape
    bias_all = _bias_tables(rel_bias)
    down_b = peer_down.astype(BF16)
    lane_pad = -w_in.shape[2] % 128
    w_in_pad = jnp.pad(w_in, ((0, 0), (0, 0), (0, lane_pad))).astype(BF16)
    x2 = x.reshape(B * S, D)
    for layer in range(w_in.shape[0]):
        p = {k: stacked[k][layer] for k in _PER_LAYER}
        p["w_in_pad"] = w_in_pad
        p["hgrn_lower_bounds"] = hgrn_lower_bounds
        p["peer_down_b"] = down_b
        p["peer_up"] = peer_up
        x2 = _layer(x2, B, S, layer, p, bias_all)
    return x2.reshape(B, S, D)
```

```python
import functools
import math

import numpy as np
import jax
import jax.numpy as jnp
from jax import lax
from jax.experimental import pallas as pl
from jax.experimental.pallas import tpu as pltpu

F32 = jnp.float32
BF16 = jnp.bfloat16
HIGHEST = lax.Precision.HIGHEST
NEG = -1e30
EPS = 1e-6

HEAD_DIM = 64
CHUNK = 64
HGRN_HEADS = 4
HGRN_W = HGRN_HEADS * HEAD_DIM
ATTN_HEADS = 6
ATTN_W = ATTN_HEADS * HEAD_DIM
ATTN_BLOCK = 128
DILATED_PAIRS = ((128, 1), (512, 4), (2048, 16))
NUM_BUCKETS = 32
MAX_DISTANCE = 2048
SSM_HEADS = 6
SSM_W = SSM_HEADS * HEAD_DIM
SSM_GROUPS = 2
SSM_STATE = 128
SSM_GN = SSM_GROUPS * SSM_STATE
SSM_CONV = 4
SSM_CONV_CH = SSM_W + 2 * SSM_GN
PEER_HEADS = 8
PEER_NKEYS = 128
PEER_HALF = 64
PEER_TOPK = 16
_PEER_TOK = 256
HG_COLS = 4 * HGRN_W
AT_COLS = 3 * ATTN_W
SS_COLS = SSM_W + SSM_CONV_CH + SSM_W

VMEM_LIMIT = 56 * 1024 * 1024


def _cparams(sem, flags=None):
    return pltpu.CompilerParams(dimension_semantics=sem, vmem_limit_bytes=VMEM_LIMIT, flags=flags)


def _iota(shape, dim):
    return lax.broadcasted_iota(jnp.int32, shape, dim)


def _mm(a, b):
    return jnp.dot(a.astype(BF16), b.astype(BF16), preferred_element_type=F32)


def _mm_nt(a, b):
    return lax.dot_general(a.astype(BF16), b.astype(BF16), (((1,), (1,)), ((), ())),
                           preferred_element_type=F32)


def _mm_tn(a, b):
    return lax.dot_general(a.astype(BF16), b.astype(BF16), (((0,), (0,)), ((), ())),
                           preferred_element_type=F32)


def _mm_hi(a, b):
    return jnp.dot(a, b, precision=HIGHEST, preferred_element_type=F32)


def _sigmoid(x):
    return 1.0 / (1.0 + jnp.exp(-x))


def _silu(x):
    return x * _sigmoid(x)


def _softplus(x):
    return jnp.maximum(x, 0.0) + jnp.log(1.0 + jnp.exp(-jnp.abs(x)))


def _seg_ones(n, seg):
    r = _iota((n, n), 0)
    c = _iota((n, n), 1)
    same = None
    for s in range(n // seg):
        lo, hi = s * seg, (s + 1) * seg
        t = (r >= lo) & (r < hi) & (c >= lo) & (c < hi)
        same = t if same is None else (same | t)
    return jnp.where(same, 1.0, 0.0).astype(F32)


def _lane_seg_mask(width, seg, idx):
    l = _iota((1, width), 1)
    return jnp.where((l >= idx * seg) & (l < (idx + 1) * seg), 1.0, 0.0).astype(F32)


def _inproj_kernel(x_ref, nw_ref, w_ref, oh_ref, oa_ref, os_ref):
    x = x_ref[...]
    xn = x * lax.rsqrt(jnp.mean(x * x, axis=-1, keepdims=True) + EPS) * nw_ref[...]
    xb = xn.astype(BF16)
    c0, c1, c2 = HG_COLS, HG_COLS + AT_COLS, HG_COLS + AT_COLS + SSM_W + SSM_CONV_CH
    oh_ref[...] = jnp.dot(xb, w_ref[:, 0:c0], preferred_element_type=F32)
    oa_ref[...] = jnp.dot(xb, w_ref[:, c0:c1], preferred_element_type=F32)
    os_ref[:, 0:c2 - c1] = jnp.dot(xb, w_ref[:, c1:c2], preferred_element_type=F32)
    dt = jnp.dot(xb, w_ref[:, c2:], preferred_element_type=F32)
    lanes = w_ref.shape[1] - c2
    expand = jnp.where((_iota((lanes, SSM_W), 1) >> 6) == _iota((lanes, SSM_W), 0), 1.0, 0.0).astype(F32)
    os_ref[:, c2 - c1:] = _mm_hi(dt, expand)


def _inproj(x2, norm_w, w_pad, layer, tm=512):
    T, D = x2.shape
    ncol = w_pad.shape[2]
    return pl.pallas_call(
        _inproj_kernel,
        grid=(T // tm,),
        in_specs=[pl.BlockSpec((tm, D), lambda i: (i, 0)),
                  pl.BlockSpec((1, D), lambda i: (0, 0)),
                  pl.BlockSpec((None, D, ncol), lambda i: (layer, 0, 0))],
        out_specs=[pl.BlockSpec((tm, HG_COLS), lambda i: (i, 0)),
                   pl.BlockSpec((tm, AT_COLS), lambda i: (i, 0)),
                   pl.BlockSpec((tm, SS_COLS), lambda i: (i, 0))],
        out_shape=[jax.ShapeDtypeStruct((T, HG_COLS), F32),
                   jax.ShapeDtypeStruct((T, AT_COLS), F32),
                   jax.ShapeDtypeStruct((T, SS_COLS), F32)],
        compiler_params=_cparams(("parallel",)),
        name="inproj",
    )(x2, norm_w.reshape(1, D), w_pad)


def _hgrn_kernel(layer, nchunk, u_ref, lbs_ref, nw_ref, o_ref, st_ref):
    W = HGRN_W

    @pl.when(pl.program_id(1) == 0)
    def _():
        st_ref[...] = jnp.zeros_like(st_ref)

    lbs = lbs_ref[...]
    e = jnp.exp(lbs - jnp.max(lbs, axis=0, keepdims=True))
    sm = e / jnp.sum(e, axis=0, keepdims=True)
    lb = jnp.zeros((1, W), F32)
    for j in range(1, layer + 1):
        lb = lb + sm[j:j + 1]
    nw = nw_ref[...]

    tril = jnp.where(_iota((CHUNK, CHUNK), 0) >= _iota((CHUNK, CHUNK), 1), 1.0, 0.0).astype(F32)
    row = _iota((CHUNK, W), 0)
    hm = [_lane_seg_mask(W, HEAD_DIM, h) for h in range(HGRN_HEADS)]
    hm3 = [jnp.concatenate([m, m, m], axis=1) for m in hm]
    seg = _seg_ones(W, HEAD_DIM)
    seg_b = seg.astype(BF16)
    tt16 = _iota((16, W), 0)
    z32 = jnp.zeros((32, W), F32)
    in_q2 = (row & 16) != 0

    def chunk(c, carry):
        r0 = pl.multiple_of(c * CHUNK, CHUNK)
        u = u_ref[0, pl.ds(r0, CHUNK), :]
        q = u[:, 0:W] * (HEAD_DIM ** -0.5)
        f = u[:, W:2 * W]
        iv = u[:, 2 * W:3 * W]
        g = u[:, 3 * W:4 * W]
        forget = lb + (1.0 - lb) * _sigmoid(f)
        logf = jnp.log(forget)
        kk = 1.0 - forget
        b = _mm_hi(tril, logf)
        b16, b32, b48, bend = b[16:17], b[32:33], b[48:49], b[63:64]

        q1 = q * jnp.exp(jnp.where(row >= 32, b - b32, NEG))
        k1 = kk * jnp.exp(jnp.where(row < 32, b32 - b, NEG))
        ref2 = jnp.where(row < 32, b16, b48)
        q2 = q * jnp.exp(jnp.where(in_q2, b - ref2, NEG))
        k2 = kk * jnp.exp(jnp.where(in_q2, NEG, ref2 - b))
        qcat = jnp.concatenate([q1, jnp.concatenate([q2[:32], z32], 0), jnp.concatenate([z32, q2[32:]], 0)], 1)
        kcat = jnp.concatenate([k1, jnp.concatenate([k2[:32], z32], 0), jnp.concatenate([z32, k2[32:]], 0)], 1)
        qstack = jnp.concatenate([qcat * hm3[h] for h in range(HGRN_HEADS)], axis=0)
        a = _mm_nt(qstack, kcat)
        ofull = _mm(a, iv)
        o = ofull[0:CHUNK] * hm[0]
        for h in range(1, HGRN_HEADS):
            o = o + ofull[h * CHUNK:(h + 1) * CHUNK] * hm[h]

        st = st_ref[...]
        o = o + _mm_nt(q * jnp.exp(b), st)

        diag = []
        for j in range(4):
            bR = b[16 * j:16 * j + 16]
            qR = q[16 * j:16 * j + 16]
            ps = []
            for s in range(16):
                r = 16 * j + s
                ps.append(qR * kk[r:r + 1] * jnp.exp(jnp.where(tt16 >= s, bR - b[r:r + 1], NEG)))
            ab = jnp.dot(jnp.concatenate(ps, axis=0).astype(BF16), seg_b, preferred_element_type=F32)
            od = ab[0:16] * iv[16 * j:16 * j + 1]
            for s in range(1, 16):
                od = od + ab[16 * s:16 * s + 16] * iv[16 * j + s:16 * j + s + 1]
            diag.append(od)
        o = o + jnp.concatenate(diag, axis=0)

        kend = kk * jnp.exp(bend - b)
        st_ref[...] = st * jnp.exp(bend) + _mm_tn(iv, kend) * seg

        ss = _mm(o * o, seg_b) * (1.0 / HEAD_DIM)
        o_ref[0, pl.ds(r0, CHUNK), :] = o * lax.rsqrt(ss + EPS) * nw * _silu(g)
        return carry

    lax.fori_loop(0, nchunk, chunk, 0, unroll=True)


def _hgrn(uh, lbs, norm_w, layer, blk=512):
    B, S, _ = uh.shape
    return pl.pallas_call(
        functools.partial(_hgrn_kernel, layer, blk // CHUNK),
        grid=(B, S // blk),
        in_specs=[pl.BlockSpec((1, blk, HG_COLS), lambda b, s: (b, s, 0)),
                  pl.BlockSpec(lbs.shape, lambda b, s: (0, 0)),
                  pl.BlockSpec((1, HGRN_W), lambda b, s: (0, 0))],
        out_specs=pl.BlockSpec((1, blk, HGRN_W), lambda b, s: (b, s, 0)),
        out_shape=jax.ShapeDtypeStruct((B, S, HGRN_W), F32),
        scratch_shapes=[pltpu.VMEM((HGRN_W, HGRN_W), F32)],
        compiler_params=_cparams(("parallel", "arbitrary")),
        name="hgrn2",
    )(uh, lbs, norm_w)


def _ssd_kernel(nchunk, u_ref, cw_ref, cb_ref, dtb_ref, alog_ref, d_ref, nw_ref, o_ref,
                st_ref, ext_ref, xc_ref):
    W = SSM_W
    blk = nchunk * CHUNK
    first = pl.program_id(1) == 0

    @pl.when(first)
    def _():
        st_ref[...] = jnp.zeros_like(st_ref)
        ext_ref[0:8, :] = jnp.zeros((8, SSM_CONV_CH), F32)

    @pl.when(jnp.logical_not(first))
    def _():
        ext_ref[0:8, :] = ext_ref[blk:blk + 8, :]

    ext_ref[8:8 + blk, :] = u_ref[0, :, W:W + SSM_CONV_CH]
    cw = cw_ref[...]
    conv = cb_ref[...] + cw[0:1] * ext_ref[5:5 + blk, :]
    for j in range(1, SSM_CONV):
        conv = conv + cw[j:j + 1] * ext_ref[5 + j:5 + j + blk, :]
    xc_ref[...] = _silu(conv)

    dtb = dtb_ref[...]
    a_neg = -jnp.exp(alog_ref[...])
    dsk = d_ref[...]
    nw = nw_ref[...]
    tril = jnp.where(_iota((CHUNK, CHUNK), 0) >= _iota((CHUNK, CHUNK), 1), 1.0, 0.0).astype(F32)
    t_i = _iota((CHUNK, W), 0)
    s_i = _iota((CHUNK, W), 1) & (HEAD_DIM - 1)
    strict = jnp.where(t_i > s_i, 1.0, 0.0).astype(F32)
    lower = t_i >= s_i
    hm = [_lane_seg_mask(W, HEAD_DIM, h) for h in range(SSM_HEADS)]
    zB = jnp.zeros((CHUNK, SSM_STATE), F32)
    gr = _iota((SSM_GN, W), 0) >= SSM_STATE
    gc = _iota((SSM_GN, W), 1) >= (SSM_HEADS // SSM_GROUPS) * HEAD_DIM
    gmask = jnp.where(gr == gc, 1.0, 0.0).astype(F32)
    gw = W // SSM_GROUPS
    seg = _seg_ones(W, gw).astype(BF16)

    def chunk(c, carry):
        r0 = pl.multiple_of(c * CHUNK, CHUNK)
        z = u_ref[0, pl.ds(r0, CHUNK), 0:W]
        dtr = u_ref[0, pl.ds(r0, CHUNK), W + SSM_CONV_CH:W + SSM_CONV_CH + W]
        xc = xc_ref[pl.ds(r0, CHUNK), :]
        xs = xc[:, 0:W]
        bm = xc[:, W:W + SSM_GN]
        cm = xc[:, W + SSM_GN:W + 2 * SSM_GN]
        dt = _softplus(dtr + dtb)
        a = dt * a_neg
        dd = _mm_hi(tril, jnp.concatenate([a * strict, a], axis=1))
        dseg = dd[:, 0:W]
        cum = dd[:, W:2 * W]
        decay = jnp.exp(jnp.where(lower, dseg, NEG))
        b0 = jnp.concatenate([bm[:, 0:SSM_STATE], zB], axis=1)
        b1 = jnp.concatenate([zB, bm[:, SSM_STATE:]], axis=1)
        bstack = jnp.concatenate([b0, b0, b0, b1, b1, b1], axis=0)
        scores = _mm_nt(cm, bstack) * decay
        xdt = xs * dt
        xbd = jnp.concatenate([xdt * hm[h] for h in range(SSM_HEADS)], axis=0)
        y = _mm(scores, xbd)
        st = st_ref[...]
        y = y + _mm(cm, st) * jnp.exp(cum)
        cend = cum[CHUNK - 1:CHUNK]
        to_end = jnp.exp(cend - cum) * dt
        st_ref[...] = st * jnp.exp(cend) + _mm_tn(bm, to_end * xs) * gmask
        y = y + dsk * xs
        y = y * _silu(z)
        ss = _mm(y * y, seg) * (1.0 / gw)
        o_ref[0, pl.ds(r0, CHUNK), :] = y * lax.rsqrt(ss + EPS) * nw
        return carry

    lax.fori_loop(0, nchunk, chunk, 0, unroll=True)


def _ssd(us, conv_w, conv_b, dtb, alog, dsk, norm_w, blk=512):
    B, S, _ = us.shape
    vec = lambda n: pl.BlockSpec((1, n), lambda b, s: (0, 0))
    return pl.pallas_call(
        functools.partial(_ssd_kernel, blk // CHUNK),
        grid=(B, S // blk),
        in_specs=[pl.BlockSpec((1, blk, SS_COLS), lambda b, s: (b, s, 0)),
                  pl.BlockSpec((SSM_CONV, SSM_CONV_CH), lambda b, s: (0, 0)),
                  vec(SSM_CONV_CH), vec(SSM_W), vec(SSM_W), vec(SSM_W), vec(SSM_W)],
        out_specs=pl.BlockSpec((1, blk, SSM_W), lambda b, s: (b, s, 0)),
        out_shape=jax.ShapeDtypeStruct((B, S, SSM_W), F32),
        scratch_shapes=[pltpu.VMEM((SSM_GN, SSM_W), F32),
                        pltpu.VMEM((blk + 8, SSM_CONV_CH), F32),
                        pltpu.VMEM((blk, SSM_CONV_CH), F32)],
        compiler_params=_cparams(("parallel", "arbitrary")),
        name="ssd",
    )(us, conv_w, conv_b, dtb, alog, dsk, norm_w)


def _t5_bucket(dist):
    max_exact = NUM_BUCKETS // 2
    d = np.maximum(dist, 1).astype(np.float32)
    large = max_exact + (np.log(d / max_exact) / np.log(MAX_DISTANCE / max_exact)
                         * (NUM_BUCKETS - max_exact)).astype(np.int32)
    large = np.minimum(large, NUM_BUCKETS - 1)
    return np.where(dist < max_exact, dist, large).astype(np.int32)


def _bucket_tables():
    qi = np.arange(ATTN_BLOCK)[:, None]
    kj = np.arange(2 * ATTN_BLOCK)[None, :]
    delta = qi - kj + ATTN_BLOCK
    return np.stack([_t5_bucket(np.maximum(delta, 0) * dil) for _, dil in DILATED_PAIRS])


def _bias_kernel(rb_ref, bk_ref, o_ref):
    bk = bk_ref[0]
    for h in range(ATTN_HEADS):
        acc = jnp.zeros(bk.shape, F32)
        for b in range(NUM_BUCKETS):
            acc = jnp.where(bk == b, rb_ref[b, h], acc)
        o_ref[0, h] = acc


def _bias_tables(rel_bias):
    bk = jnp.asarray(_bucket_tables())
    nbr = len(DILATED_PAIRS)
    return pl.pallas_call(
        _bias_kernel,
        grid=(nbr,),
        in_specs=[pl.BlockSpec(memory_space=pltpu.SMEM),
                  pl.BlockSpec((1, ATTN_BLOCK, 2 * ATTN_BLOCK), lambda i: (i, 0, 0))],
        out_specs=pl.BlockSpec((1, ATTN_HEADS, ATTN_BLOCK, 2 * ATTN_BLOCK), lambda i: (i, 0, 0, 0)),
        out_shape=jax.ShapeDtypeStruct((nbr, ATTN_HEADS, ATTN_BLOCK, 2 * ATTN_BLOCK), F32),
        compiler_params=_cparams(("parallel",)),
        name="attn_bias",
    )(rel_bias, bk)


def _attn_kernel(ua_ref, qw_ref, kw_ref, bias_ref, o_ref, qn_ref, kn_ref, vv_ref, num_ref, den_ref, mx_ref):
    W = ATTN_W
    S = ua_ref.shape[1]
    blk = ATTN_BLOCK
    seg = _seg_ones(W, HEAD_DIM).astype(BF16)
    qw = qw_ref[...]
    kw = kw_ref[...]

    def nrm(x, w):
        ss = _mm(x * x, seg) * (1.0 / HEAD_DIM)
        return x * lax.rsqrt(ss + EPS) * w

    npair = ATTN_HEADS // 2
    pw = 2 * HEAD_DIM

    def norm_block(i, carry):
        rows = pl.ds(pl.multiple_of(i * blk, blk), blk)
        qn = nrm(ua_ref[0, rows, 0:W], qw) * (HEAD_DIM ** -0.5)
        kn = nrm(ua_ref[0, rows, W:2 * W], kw)
        for pp in range(npair):
            qn_ref[pp, rows, :] = qn[:, pp * pw:(pp + 1) * pw]
            kn_ref[pp, rows, :] = kn[:, pp * pw:(pp + 1) * pw]
            vv_ref[pp, rows, :] = ua_ref[0, rows, 2 * W + pp * pw:2 * W + (pp + 1) * pw]
        return carry

    lax.fori_loop(0, S // blk, norm_block, 0)

    lane = _iota((blk, pw), 1)
    low = lane < HEAD_DIM
    half = [jnp.where(low[0:1], 1.0, 0.0).astype(F32), jnp.where(low[0:1], 0.0, 1.0).astype(F32)]

    order = sorted(range(len(DILATED_PAIRS)), key=lambda i: -DILATED_PAIRS[i][1])
    for br in order:
        window, dil = DILATED_PAIRS[br]
        first = br == order[0]
        n_back = window // dil
        nb = S // dil // blk
        has_prev = nb > 1
        nk = 2 * blk if has_prev else blk
        qi = _iota((blk, nk), 0)
        kj = _iota((blk, nk), 1) + (0 if has_prev else blk)
        delta = qi - kj + blk
        band = (delta >= 0) & (delta <= n_back)

        def rows_at(start, dil=dil):
            if dil == 1:
                return pl.ds(pl.multiple_of(start, blk), blk)
            return pl.ds(start, blk, stride=dil)

        def block(idx, carry, br=br, dil=dil, nb=nb, has_prev=has_prev, kj=kj, band=band, rows_at=rows_at,
                  first=first):
            r = idx // nb
            n = idx % nb
            rq = rows_at(r + dil * blk * n)
            valid = band
            if has_prev:
                rp = rows_at(r + dil * blk * jnp.maximum(n - 1, 0))
                valid = band & (kj >= jnp.where(n > 0, 0, blk))
            for pp in range(npair):
                qp = qn_ref[pp, rq, :]
                kp = kn_ref[pp, rq, :]
                vp = vv_ref[pp, rq, :]
                if has_prev:
                    kp = jnp.concatenate([kn_ref[pp, rp, :], kp], axis=0)
                    vp = jnp.concatenate([vv_ref[pp, rp, :], vp], axis=0)
                oh, lh = [], []
                for hh in range(2):
                    bias = bias_ref[br, 2 * pp + hh]
                    if not has_prev:
                        bias = bias[:, blk:]
                    s = jnp.where(valid, _mm_nt(qp * half[hh], kp) + bias, NEG)
                    m = jnp.max(s, axis=-1, keepdims=True)
                    p = jnp.exp(s - m)
                    l = jnp.sum(p, axis=-1, keepdims=True)
                    oh.append(_mm(p, vp) / l)
                    lh.append(m + jnp.log(l))
                o = jnp.where(low, oh[0], oh[1])
                lse = jnp.where(low, lh[0], lh[1])
                if first:
                    num_ref[pp, rq, :] = o
                    den_ref[pp, rq, :] = jnp.ones_like(o)
                    mx_ref[pp, rq, :] = lse
                else:
                    m_old = mx_ref[pp, rq, :]
                    m_new = jnp.maximum(m_old, lse)
                    wa = jnp.exp(m_old - m_new)
                    wb = jnp.exp(lse - m_new)
                    num_ref[pp, rq, :] = num_ref[pp, rq, :] * wa + wb * o
                    den_ref[pp, rq, :] = den_ref[pp, rq, :] * wa + wb
                    mx_ref[pp, rq, :] = m_new
            return carry

        lax.fori_loop(0, dil * nb, block, 0)

    def finish(i, carry):
        rows = pl.ds(pl.multiple_of(i * blk, blk), blk)
        for pp in range(npair):
            o_ref[0, rows, pp * pw:(pp + 1) * pw] = num_ref[pp, rows, :] / den_ref[pp, rows, :]
        return carry

    lax.fori_loop(0, S // blk, finish, 0)


def _attn(ua3, qw, kw, bias_all):
    B, S, _ = ua3.shape
    nbr = len(DILATED_PAIRS)
    return pl.pallas_call(
        _attn_kernel,
        grid=(B,),
        in_specs=[pl.BlockSpec((1, S, AT_COLS), lambda b: (b, 0, 0)),
                  pl.BlockSpec((1, ATTN_W), lambda b: (0, 0)),
                  pl.BlockSpec((1, ATTN_W), lambda b: (0, 0)),
                  pl.BlockSpec((nbr, ATTN_HEADS, ATTN_BLOCK, 2 * ATTN_BLOCK), lambda b: (0, 0, 0, 0))],
        out_specs=pl.BlockSpec((1, S, ATTN_W), lambda b: (b, 0, 0)),
        out_shape=jax.ShapeDtypeStruct((B, S, ATTN_W), F32),
        scratch_shapes=[pltpu.VMEM((ATTN_HEADS // 2, S, 2 * HEAD_DIM), F32) for _ in range(6)],
        compiler_params=_cparams(("parallel",)),
        name="dilated_attn",
    )(ua3, qw, kw, bias_all)


def _outproj_kernel(x_ref, mh_ref, ma_ref, ms_ref, wo_ref, fw_ref, wq_ref, sk_ref,
                    h_ref, xn_ref, sc_ref):
    W = ATTN_W
    h = (x_ref[...] + _mm(mh_ref[...], wo_ref[0:HGRN_W, :])
         + _mm(ma_ref[...], wo_ref[HGRN_W:HGRN_W + W, :])
         + _mm(ms_ref[...], wo_ref[HGRN_W + W:, :]))
    h_ref[...] = h
    xn = (h * lax.rsqrt(jnp.mean(h * h, axis=-1, keepdims=True) + EPS) * fw_ref[...]).astype(BF16)
    xn_ref[...] = xn
    qp = jnp.dot(xn, wq_ref[...], preferred_element_type=F32)
    for hd in range(PEER_HEADS):
        qh = qp[:, 2 * PEER_HALF * hd:2 * PEER_HALF * (hd + 1)]
        for c in range(2):
            sc_ref[hd, c] = _mm_nt(sk_ref[c], qh)


def _outproj(x2, mh, ma, ms, w_out, ffn_w, wq, sk_pad, tm=512):
    T, D = x2.shape
    row = lambda n: pl.BlockSpec((tm, n), lambda i: (i, 0))
    full = lambda a: pl.BlockSpec(a.shape, lambda i: (0,) * a.ndim)
    return pl.pallas_call(
        _outproj_kernel,
        grid=(T // tm,),
        in_specs=[row(D), row(HGRN_W), row(ATTN_W), row(SSM_W),
                  full(w_out), pl.BlockSpec((1, D), lambda i: (0, 0)), full(wq), full(sk_pad)],
        out_specs=[row(D), row(D),
                   pl.BlockSpec((PEER_HEADS, 2, PEER_NKEYS, tm), lambda i: (0, 0, 0, i))],
        out_shape=[jax.ShapeDtypeStruct((T, D), F32), jax.ShapeDtypeStruct((T, D), BF16),
                   jax.ShapeDtypeStruct((PEER_HEADS, 2, PEER_NKEYS, T), F32)],
        compiler_params=_cparams(("parallel",)),
        name="outproj_query",
    )(x2, mh, ma, ms, w_out, ffn_w.reshape(1, D), wq, sk_pad)


_NSEL = PEER_TOPK + 1
_PAIRS = [(i, j) for i in range(_NSEL) for j in range(_NSEL) if (i + 1) * (j + 1) <= _NSEL]


def _top_values(v, k):
    out = []
    rank = jnp.full(v.shape, float(k), F32)
    for r in range(k):
        m = jnp.max(v, axis=0, keepdims=True)
        out.append(m)
        hit = v == m
        rank = jnp.where(hit, float(r), rank)
        v = jnp.where(hit, NEG, v)
    return out, rank


def _pack_pair(lo, hi):
    lo_b = lax.bitcast_convert_type(lo.astype(BF16).astype(F32), jnp.uint32) >> 16
    hi_b = lax.bitcast_convert_type(hi.astype(BF16).astype(F32), jnp.uint32) & jnp.uint32(0xFFFF0000)
    return hi_b | lo_b


def _route_kernel(sc_ref, n1_ref, u1_ref, rk2_ref, u2_ref):
    tt = sc_ref.shape[-1]
    row8 = _iota((8, tt), 0)

    def head(hd):
        s1 = sc_ref[hd, 0]
        s2 = sc_ref[hd, 1]
        a, _ = _top_values(s1, _NSEL)
        b, rank2 = _top_values(s2, _NSEL)
        groups = []
        for g0 in range(0, len(_PAIRS), 8):
            cg = jnp.full((8, tt), NEG, F32)
            for p, (i, j) in enumerate(_PAIRS[g0:g0 + 8]):
                cg = jnp.where(row8 == p, a[i] + b[j], cg)
            groups.append(cg)
        best, _ = _top_values(jnp.concatenate(groups, axis=0), _NSEL)
        zsum = jnp.ones((1, tt), F32)
        for r in range(1, PEER_TOPK):
            zsum = zsum + jnp.exp(best[r] - best[0])
        cut = 0.5 * (best[PEER_TOPK - 1] + best[PEER_TOPK])
        need = cut - s1
        n1 = jnp.zeros_like(s1)
        for r in range(PEER_TOPK):
            n1 = n1 + jnp.where(b[r] >= need, 1.0, 0.0)
        return n1, jnp.exp(s1 - a[0]) / zsum, rank2, jnp.exp(s2 - b[0])

    def pair(hp, carry):
        lo = head(2 * hp)
        hi = head(2 * hp + 1)
        for ref, x, y in zip((n1_ref, u1_ref, rk2_ref, u2_ref), lo, hi):
            ref[hp] = _pack_pair(x, y)
        return carry

    lax.fori_loop(0, PEER_HEADS // 2, pair, 0)


def _route(sc, tt=256):
    T = sc.shape[-1]
    blk = pl.BlockSpec((PEER_HEADS // 2, PEER_NKEYS, tt), lambda i: (0, 0, i))
    shp = jax.ShapeDtypeStruct((PEER_HEADS // 2, PEER_NKEYS, T), jnp.uint32)
    return pl.pallas_call(
        _route_kernel,
        grid=(T // tt,),
        in_specs=[pl.BlockSpec((PEER_HEADS, 2, PEER_NKEYS, tt), lambda i: (0, 0, 0, i))],
        out_specs=[blk, blk, blk, blk],
        out_shape=[shp, shp, shp, shp],
        compiler_params=_cparams(("parallel",)),
        name="peer_route",
    )(sc)


_GELU_C1 = -2.0 * math.sqrt(2.0 / math.pi)
_GELU_C2 = _GELU_C1 * 0.044715
_PEER_EB = 512
_PEER_LANES = 128
_PEER_ROWS = 32
_PEER_FLAGS = None


def _as_halves(words):
    return pltpu.bitcast(words, BF16)


def _peer_kernel(nj, xn_ref, dn_ref, upt_ref, n1_ref, u1_ref, rk2_ref, u2_ref, h_ref, o_ref,
                 acc_ref, at0_ref, at1_ref, wg0_ref, wg1_ref):
    j = pl.program_id(1)
    tt = xn_ref.shape[0]
    te = dn_ref.shape[0]
    nb = te // _PEER_EB
    nh = tt // _PEER_TOK
    gpb = _PEER_EB // PEER_NKEYS
    at = (at0_ref, at1_ref)
    wg = (wg0_ref, wg1_ref)

    @pl.when(j == 0)
    def _():
        acc_ref[...] = jnp.zeros_like(acc_ref)

    def scores(k, at_ref):
        rows = pl.ds(pl.multiple_of(k * _PEER_EB, _PEER_EB), _PEER_EB)
        for hf in range(nh):
            at_ref[hf] = lax.dot_general(dn_ref[rows, :], xn_ref[hf * _PEER_TOK:(hf + 1) * _PEER_TOK, :],
                                         (((1,), (1,)), ((), ())), preferred_element_type=F32)

    def weigh(k, at_ref, wg_ref):
        for g in range(gpb):
            c = (j * nb + k) * gpb + g
            er = slice(g * PEER_NKEYS, (g + 1) * PEER_NKEYS)
            for hf in range(nh):
                tw = slice(hf * _PEER_TOK, (hf + 1) * _PEER_TOK)
                n1s = [n1_ref[hp, pl.ds(c, 1), tw] for hp in range(PEER_HEADS // 2)]
                u1s = [u1_ref[hp, pl.ds(c, 1), tw] for hp in range(PEER_HEADS // 2)]
                for lt in range(_PEER_TOK // _PEER_LANES):
                    ls = slice(lt * _PEER_LANES, (lt + 1) * _PEER_LANES)
                    tl = slice(hf * _PEER_TOK + lt * _PEER_LANES, hf * _PEER_TOK + (lt + 1) * _PEER_LANES)
                    w = None
                    for hp in range(PEER_HEADS // 2):
                        n1 = _as_halves(jnp.broadcast_to(n1s[hp][:, ls], (PEER_NKEYS, _PEER_LANES)))
                        u1 = _as_halves(jnp.broadcast_to(u1s[hp][:, ls], (PEER_NKEYS, _PEER_LANES)))
                        rk2 = _as_halves(rk2_ref[hp, :, tl])
                        u2 = _as_halves(u2_ref[hp, :, tl])
                        t = jnp.where(rk2 < n1, u2, jnp.zeros((), BF16)) * u1
                        w = t if w is None else w + t
                    wu = pltpu.bitcast(w, jnp.uint32)
                    wf = (lax.bitcast_convert_type(wu << 16, F32)
                          + lax.bitcast_convert_type(wu & jnp.uint32(0xFFFF0000), F32))
                    a = at_ref[hf, er, ls]
                    gl = a / (1.0 + jnp.exp(a * (_GELU_C1 + _GELU_C2 * (a * a))))
                    wg_ref[hf, er, ls] = (wf * gl).astype(BF16)

    def project(k, wg_ref):
        for hf in range(nh):
            tw = slice(hf * _PEER_TOK, (hf + 1) * _PEER_TOK)
            acc_ref[:, tw] += jnp.dot(upt_ref[k], wg_ref[hf], preferred_element_type=F32)

    scores(0, at[0])
    scores(1, at[1])
    weigh(0, at[0], wg[0])

    def body(m, carry):
        k = 2 * m + 1
        scores(k + 1, at[0])
        weigh(k, at[1], wg[1])
        project(k - 1, wg[0])
        scores(k + 2, at[1])
        weigh(k + 1, at[0], wg[0])
        project(k, wg[1])
        return carry

    lax.fori_loop(0, nb // 2 - 1, body, 0)
    weigh(nb - 1, at[1], wg[1])
    project(nb - 2, wg[0])
    project(nb - 1, wg[1])

    @pl.when(j == nj - 1)
    def _():
        o_ref[...] = h_ref[...] + acc_ref[...].T


def _peer(xn, down_b, layer, up_t3, n1, u1, rk2, u2, h, tt=512, te=2048):
    T, D = xn.shape
    E = down_b.shape[1]
    nj = E // te
    nh = tt // _PEER_TOK
    rt = pl.BlockSpec((PEER_HEADS // 2, PEER_NKEYS, tt), lambda i, j: (0, 0, i))
    return pl.pallas_call(
        functools.partial(_peer_kernel, nj),
        grid=(T // tt, nj),
        in_specs=[pl.BlockSpec((tt, D), lambda i, j: (i, 0)),
                  pl.BlockSpec((None, te, D), lambda i, j: (layer, j, 0)),
                  pl.BlockSpec((te // _PEER_EB, D, _PEER_EB), lambda i, j: (j, 0, 0)),
                  rt, rt, rt, rt,
                  pl.BlockSpec((tt, D), lambda i, j: (i, 0))],
        out_specs=pl.BlockSpec((tt, D), lambda i, j: (i, 0)),
        out_shape=jax.ShapeDtypeStruct((T, D), F32),
        scratch_shapes=[pltpu.VMEM((D, tt), F32),
                        pltpu.VMEM((nh, _PEER_EB, _PEER_TOK), F32), pltpu.VMEM((nh, _PEER_EB, _PEER_TOK), F32),
                        pltpu.VMEM((nh, _PEER_EB, _PEER_TOK), BF16), pltpu.VMEM((nh, _PEER_EB, _PEER_TOK), BF16)],
        compiler_params=_cparams(("parallel", "arbitrary"), flags=_PEER_FLAGS),
        name="peer_experts",
    )(xn, down_b, up_t3, n1, u1, rk2, u2, h)


def _transpose_cast_kernel(x_ref, o_ref):
    o_ref[0] = x_ref[...].T.astype(o_ref.dtype)


def _transpose_blocks_bf16(x, layer, tr=_PEER_EB):
    _, R, C = x.shape
    return pl.pallas_call(
        _transpose_cast_kernel,
        grid=(R // tr,),
        in_specs=[pl.BlockSpec((None, tr, C), lambda i: (layer, i, 0))],
        out_specs=pl.BlockSpec((1, C, tr), lambda i: (i, 0, 0)),
        out_shape=jax.ShapeDtypeStruct((R // tr, C, tr), BF16),
        compiler_params=_cparams(("parallel",)),
        name="transpose_cast",
    )(x)


def _rep(v, n):
    return jnp.repeat(v, n, axis=-1)


def _layer(x2, B, S, layer, p, bias_all):
    T, D = x2.shape
    uh, ua, us = _inproj(x2, p["attn_norm_w"], p["w_in_pad"], layer)

    mh = _hgrn(uh.reshape(B, S, HG_COLS), p["hgrn_lower_bounds"],
               jnp.tile(p["hgrn_norm_w"], HGRN_HEADS).reshape(1, HGRN_W), layer)
    qw = jnp.tile(p["q_norm_w"], ATTN_HEADS).reshape(1, ATTN_W)
    kw = jnp.tile(p["k_norm_w"], ATTN_HEADS).reshape(1, ATTN_W)
    ma = _attn(ua.reshape(B, S, AT_COLS), qw, kw, bias_all)
    ms = _ssd(us.reshape(B, S, SS_COLS), p["ssm_conv_w"], p["ssm_conv_b"].reshape(1, -1),
              _rep(p["ssm_dt_bias"], HEAD_DIM).reshape(1, -1), _rep(p["ssm_a_log"], HEAD_DIM).reshape(1, -1),
              _rep(p["ssm_d"], HEAD_DIM).reshape(1, -1), p["ssm_norm_w"].reshape(1, -1))

    sk = p["peer_sub_keys"]
    zk = jnp.zeros_like(sk[0])
    sk_pad = jnp.stack([jnp.concatenate([sk[0], zk], axis=1), jnp.concatenate([zk, sk[1]], axis=1)])
    h, xn, sc = _outproj(x2, mh.reshape(T, HGRN_W), ma.reshape(T, ATTN_W), ms.reshape(T, SSM_W),
                         p["w_out"].astype(BF16), p["ffn_norm_w"], p["peer_w_query"].astype(BF16), sk_pad)
    n1, u1, rk2, u2 = _route(sc)
    return _peer(xn, p["peer_down_b"], layer, _transpose_blocks_bf16(p["peer_up"], layer), n1, u1, rk2, u2, h)


_PER_LAYER = ("attn_norm_w", "hgrn_norm_w", "q_norm_w", "k_norm_w", "ssm_conv_w", "ssm_conv_b",
              "ssm_dt_bias", "ssm_a_log", "ssm_d", "ssm_norm_w", "w_out", "ffn_norm_w", "peer_w_query",
              "peer_sub_keys")


def kernel(x, attn_norm_w, w_in, hgrn_lower_bounds, hgrn_norm_w, q_norm_w, k_norm_w, rel_bias, ssm_conv_w, ssm_conv_b, ssm_dt_bias, ssm_a_log, ssm_d, ssm_norm_w, w_out, ffn_norm_w, peer_w_query, peer_sub_keys, peer_down, peer_up):
    stacked = dict(attn_norm_w=attn_norm_w, w_in=w_in, hgrn_norm_w=hgrn_norm_w, q_norm_w=q_norm_w,
                   k_norm_w=k_norm_w, ssm_conv_w=ssm_conv_w, ssm_conv_b=ssm_conv_b, ssm_dt_bias=ssm_dt_bias,
                   ssm_a_log=ssm_a_log, ssm_d=ssm_d, ssm_norm_w=ssm_norm_w, w_out=w_out, ffn_norm_w=ffn_norm_w,
                   peer_w_query=peer_w_query, peer_sub_keys=peer_sub_keys, peer_down=peer_down, peer_up=peer_up)
    B, S, D = x.shape
    bias_all = _bias_tables(rel_bias)
    down_b = peer_down.astype(BF16)
    lane_pad = -w_in.shape[2] % 128
    w_in_pad = jnp.pad(w_in, ((0, 0), (0, 0), (0, lane_pad))).astype(BF16)
    x2 = x.reshape(B * S, D)
    for layer in range(w_in.shape[0]):
        p = {k: stacked[k][layer] for k in _PER_LAYER}
        p["w_in_pad"] = w_in_pad
        p["hgrn_lower_bounds"] = hgrn_lower_bounds
        p["peer_down_b"] = down_b
        p["peer_up"] = peer_up
        x2 = _layer(x2, B, S, layer, p, bias_all)
    return x2.reshape(B, S, D)
```

```python
import functools
import math

import numpy as np
import jax
import jax.numpy as jnp
from jax import lax
from jax.experimental import pallas as pl
from jax.experimental.pallas import tpu as pltpu

F32 = jnp.float32
BF16 = jnp.bfloat16
HIGHEST = lax.Precision.HIGHEST
NEG = -1e30
EPS = 1e-6

HEAD_DIM = 64
CHUNK = 64
HGRN_HEADS = 4
HGRN_W = HGRN_HEADS * HEAD_DIM
ATTN_HEADS = 6
ATTN_W = ATTN_HEADS * HEAD_DIM
ATTN_BLOCK = 128
DILATED_PAIRS = ((128, 1), (512, 4), (2048, 16))
NUM_BUCKETS = 32
MAX_DISTANCE = 2048
SSM_HEADS = 6
SSM_W = SSM_HEADS * HEAD_DIM
SSM_GROUPS = 2
SSM_STATE = 128
SSM_GN = SSM_GROUPS * SSM_STATE
SSM_CONV = 4
SSM_CONV_CH = SSM_W + 2 * SSM_GN
PEER_HEADS = 8
PEER_NKEYS = 128
PEER_HALF = 64
PEER_TOPK = 16
_PEER_TOK = 256
HG_COLS = 4 * HGRN_W
AT_COLS = 3 * ATTN_W
SS_COLS = SSM_W + SSM_CONV_CH + SSM_W

VMEM_LIMIT = 56 * 1024 * 1024


def _cparams(sem, flags=None):
    return pltpu.CompilerParams(dimension_semantics=sem, vmem_limit_bytes=VMEM_LIMIT, flags=flags)


def _iota(shape, dim):
    return lax.broadcasted_iota(jnp.int32, shape, dim)


def _mm(a, b):
    return jnp.dot(a.astype(BF16), b.astype(BF16), preferred_element_type=F32)


def _mm_nt(a, b):
    return lax.dot_general(a.astype(BF16), b.astype(BF16), (((1,), (1,)), ((), ())),
                           preferred_element_type=F32)


def _mm_tn(a, b):
    return lax.dot_general(a.astype(BF16), b.astype(BF16), (((0,), (0,)), ((), ())),
                           preferred_element_type=F32)


def _mm_hi(a, b):
    return jnp.dot(a, b, precision=HIGHEST, preferred_element_type=F32)


def _sigmoid(x):
    return 1.0 / (1.0 + jnp.exp(-x))


def _silu(x):
    return x * _sigmoid(x)


def _softplus(x):
    return jnp.maximum(x, 0.0) + jnp.log(1.0 + jnp.exp(-jnp.abs(x)))


def _seg_ones(n, seg):
    r = _iota((n, n), 0)
    c = _iota((n, n), 1)
    same = None
    for s in range(n // seg):
        lo, hi = s * seg, (s + 1) * seg
        t = (r >= lo) & (r < hi) & (c >= lo) & (c < hi)
        same = t if same is None else (same | t)
    return jnp.where(same, 1.0, 0.0).astype(F32)


def _lane_seg_mask(width, seg, idx):
    l = _iota((1, width), 1)
    return jnp.where((l >= idx * seg) & (l < (idx + 1) * seg), 1.0, 0.0).astype(F32)


def _inproj_kernel(x_ref, nw_ref, w_ref, oh_ref, oa_ref, os_ref):
    x = x_ref[...]
    xn = x * lax.rsqrt(jnp.mean(x * x, axis=-1, keepdims=True) + EPS) * nw_ref[...]
    xb = xn.astype(BF16)
    c0, c1, c2 = HG_COLS, HG_COLS + AT_COLS, HG_COLS + AT_COLS + SSM_W + SSM_CONV_CH
    oh_ref[...] = jnp.dot(xb, w_ref[:, 0:c0], preferred_element_type=F32)
    oa_ref[...] = jnp.dot(xb, w_ref[:, c0:c1], preferred_element_type=F32)
    os_ref[:, 0:c2 - c1] = jnp.dot(xb, w_ref[:, c1:c2], preferred_element_type=F32)
    dt = jnp.dot(xb, w_ref[:, c2:], preferred_element_type=F32)
    lanes = w_ref.shape[1] - c2
    expand = jnp.where((_iota((lanes, SSM_W), 1) >> 6) == _iota((lanes, SSM_W), 0), 1.0, 0.0).astype(F32)
    os_ref[:, c2 - c1:] = _mm_hi(dt, expand)


def _inproj(x2, norm_w, w_pad, layer, tm=512):
    T, D = x2.shape
    ncol = w_pad.shape[2]
    return pl.pallas_call(
        _inproj_kernel,
        grid=(T // tm,),
        in_specs=[pl.BlockSpec((tm, D), lambda i: (i, 0)),
                  pl.BlockSpec((1, D), lambda i: (0, 0)),
                  pl.BlockSpec((None, D, ncol), lambda i: (layer, 0, 0))],
        out_specs=[pl.BlockSpec((tm, HG_COLS), lambda i: (i, 0)),
                   pl.BlockSpec((tm, AT_COLS), lambda i: (i, 0)),
                   pl.BlockSpec((tm, SS_COLS), lambda i: (i, 0))],
        out_shape=[jax.ShapeDtypeStruct((T, HG_COLS), F32),
                   jax.ShapeDtypeStruct((T, AT_COLS), F32),
                   jax.ShapeDtypeStruct((T, SS_COLS), F32)],
        compiler_params=_cparams(("parallel",)),
        name="inproj",
    )(x2, norm_w.reshape(1, D), w_pad)


def _hgrn_kernel(layer, nchunk, u_ref, lbs_ref, nw_ref, o_ref, st_ref):
    W = HGRN_W

    @pl.when(pl.program_id(1) == 0)
    def _():
        st_ref[...] = jnp.zeros_like(st_ref)

    lbs = lbs_ref[...]
    e = jnp.exp(lbs - jnp.max(lbs, axis=0, keepdims=True))
    sm = e / jnp.sum(e, axis=0, keepdims=True)
    lb = jnp.zeros((1, W), F32)
    for j in range(1, layer + 1):
        lb = lb + sm[j:j + 1]
    nw = nw_ref[...]

    tril = jnp.where(_iota((CHUNK, CHUNK), 0) >= _iota((CHUNK, CHUNK), 1), 1.0, 0.0).astype(F32)
    row = _iota((CHUNK, W), 0)
    hm = [_lane_seg_mask(W, HEAD_DIM, h) for h in range(HGRN_HEADS)]
    hm3 = [jnp.concatenate([m, m, m], axis=1) for m in hm]
    seg = _seg_ones(W, HEAD_DIM)
    seg_b = seg.astype(BF16)
    tt16 = _iota((16, W), 0)
    z32 = jnp.zeros((32, W), F32)
    in_q2 = (row & 16) != 0

    def chunk(c, carry):
        r0 = pl.multiple_of(c * CHUNK, CHUNK)
        u = u_ref[0, pl.ds(r0, CHUNK), :]
        q = u[:, 0:W] * (HEAD_DIM ** -0.5)
        f = u[:, W:2 * W]
        iv = u[:, 2 * W:3 * W]
        g = u[:, 3 * W:4 * W]
        forget = lb + (1.0 - lb) * _sigmoid(f)
        logf = jnp.log(forget)
        kk = 1.0 - forget
        b = _mm_hi(tril, logf)
        b16, b32, b48, bend = b[16:17], b[32:33], b[48:49], b[63:64]

        q1 = q * jnp.exp(jnp.where(row >= 32, b - b32, NEG))
        k1 = kk * jnp.exp(jnp.where(row < 32, b32 - b, NEG))
        ref2 = jnp.where(row < 32, b16, b48)
        q2 = q * jnp.exp(jnp.where(in_q2, b - ref2, NEG))
        k2 = kk * jnp.exp(jnp.where(in_q2, NEG, ref2 - b))
        qcat = jnp.concatenate([q1, jnp.concatenate([q2[:32], z32], 0), jnp.concatenate([z32, q2[32:]], 0)], 1)
        kcat = jnp.concatenate([k1, jnp.concatenate([k2[:32], z32], 0), jnp.concatenate([z32, k2[32:]], 0)], 1)
        qstack = jnp.concatenate([qcat * hm3[h] for h in range(HGRN_HEADS)], axis=0)
        a = _mm_nt(qstack, kcat)
        ofull = _mm(a, iv)
        o = ofull[0:CHUNK] * hm[0]
        for h in range(1, HGRN_HEADS):
            o = o + ofull[h * CHUNK:(h + 1) * CHUNK] * hm[h]

        st = st_ref[...]
        o = o + _mm_nt(q * jnp.exp(b), st)

        diag = []
        for j in range(4):
            bR = b[16 * j:16 * j + 16]
            qR = q[16 * j:16 * j + 16]
            ps = []
            for s in range(16):
                r = 16 * j + s
                ps.append(qR * kk[r:r + 1] * jnp.exp(jnp.where(tt16 >= s, bR - b[r:r + 1], NEG)))
            ab = jnp.dot(jnp.concatenate(ps, axis=0).astype(BF16), seg_b, preferred_element_type=F32)
            od = ab[0:16] * iv[16 * j:16 * j + 1]
            for s in range(1, 16):
                od = od + ab[16 * s:16 * s + 16] * iv[16 * j + s:16 * j + s + 1]
            diag.append(od)
        o = o + jnp.concatenate(diag, axis=0)

        kend = kk * jnp.exp(bend - b)
        st_ref[...] = st * jnp.exp(bend) + _mm_tn(iv, kend) * seg

        ss = _mm(o * o, seg_b) * (1.0 / HEAD_DIM)
        o_ref[0, pl.ds(r0, CHUNK), :] = o * lax.rsqrt(ss + EPS) * nw * _silu(g)
        return carry

    lax.fori_loop(0, nchunk, chunk, 0, unroll=True)


def _hgrn(uh, lbs, norm_w, layer, blk=512):
    B, S, _ = uh.shape
    return pl.pallas_call(
        functools.partial(_hgrn_kernel, layer, blk // CHUNK),
        grid=(B, S // blk),
        in_specs=[pl.BlockSpec((1, blk, HG_COLS), lambda b, s: (b, s, 0)),
                  pl.BlockSpec(lbs.shape, lambda b, s: (0, 0)),
                  pl.BlockSpec((1, HGRN_W), lambda b, s: (0, 0))],
        out_specs=pl.BlockSpec((1, blk, HGRN_W), lambda b, s: (b, s, 0)),
        out_shape=jax.ShapeDtypeStruct((B, S, HGRN_W), F32),
        scratch_shapes=[pltpu.VMEM((HGRN_W, HGRN_W), F32)],
        compiler_params=_cparams(("parallel", "arbitrary")),
        name="hgrn2",
    )(uh, lbs, norm_w)


def _ssd_kernel(nchunk, u_ref, cw_ref, cb_ref, dtb_ref, alog_ref, d_ref, nw_ref, o_ref,
                st_ref, ext_ref, xc_ref):
    W = SSM_W
    blk = nchunk * CHUNK
    first = pl.program_id(1) == 0

    @pl.when(first)
    def _():
        st_ref[...] = jnp.zeros_like(st_ref)
        ext_ref[0:8, :] = jnp.zeros((8, SSM_CONV_CH), F32)

    @pl.when(jnp.logical_not(first))
    def _():
        ext_ref[0:8, :] = ext_ref[blk:blk + 8, :]

    ext_ref[8:8 + blk, :] = u_ref[0, :, W:W + SSM_CONV_CH]
    cw = cw_ref[...]
    conv = cb_ref[...] + cw[0:1] * ext_ref[5:5 + blk, :]
    for j in range(1, SSM_CONV):
        conv = conv + cw[j:j + 1] * ext_ref[5 + j:5 + j + blk, :]
    xc_ref[...] = _silu(conv)

    dtb = dtb_ref[...]
    a_neg = -jnp.exp(alog_ref[...])
    dsk = d_ref[...]
    nw = nw_ref[...]
    tril = jnp.where(_iota((CHUNK, CHUNK), 0) >= _iota((CHUNK, CHUNK), 1), 1.0, 0.0).astype(F32)
    t_i = _iota((CHUNK, W), 0)
    s_i = _iota((CHUNK, W), 1) & (HEAD_DIM - 1)
    strict = jnp.where(t_i > s_i, 1.0, 0.0).astype(F32)
    lower = t_i >= s_i
    hm = [_lane_seg_mask(W, HEAD_DIM, h) for h in range(SSM_HEADS)]
    zB = jnp.zeros((CHUNK, SSM_STATE), F32)
    gr = _iota((SSM_GN, W), 0) >= SSM_STATE
    gc = _iota((SSM_GN, W), 1) >= (SSM_HEADS // SSM_GROUPS) * HEAD_DIM
    gmask = jnp.where(gr == gc, 1.0, 0.0).astype(F32)
    gw = W // SSM_GROUPS
    seg = _seg_ones(W, gw).astype(BF16)

    def chunk(c, carry):
        r0 = pl.multiple_of(c * CHUNK, CHUNK)
        z = u_ref[0, pl.ds(r0, CHUNK), 0:W]
        dtr = u_ref[0, pl.ds(r0, CHUNK), W + SSM_CONV_CH:W + SSM_CONV_CH + W]
        xc = xc_ref[pl.ds(r0, CHUNK), :]
        xs = xc[:, 0:W]
        bm = xc[:, W:W + SSM_GN]
        cm = xc[:, W + SSM_GN:W + 2 * SSM_GN]
        dt = _softplus(dtr + dtb)
        a = dt * a_neg
        dd = _mm_hi(tril, jnp.concatenate([a * strict, a], axis=1))
        dseg = dd[:, 0:W]
        cum = dd[:, W:2 * W]
        decay = jnp.exp(jnp.where(lower, dseg, NEG))
        b0 = jnp.concatenate([bm[:, 0:SSM_STATE], zB], axis=1)
        b1 = jnp.concatenate([zB, bm[:, SSM_STATE:]], axis=1)
        bstack = jnp.concatenate([b0, b0, b0, b1, b1, b1], axis=0)
        scores = _mm_nt(cm, bstack) * decay
        xdt = xs * dt
        xbd = jnp.concatenate([xdt * hm[h] for h in range(SSM_HEADS)], axis=0)
        y = _mm(scores, xbd)
        st = st_ref[...]
        y = y + _mm(cm, st) * jnp.exp(cum)
        cend = cum[CHUNK - 1:CHUNK]
        to_end = jnp.exp(cend - cum) * dt
        st_ref[...] = st * jnp.exp(cend) + _mm_tn(bm, to_end * xs) * gmask
        y = y + dsk * xs
        y = y * _silu(z)
        ss = _mm(y * y, seg) * (1.0 / gw)
        o_ref[0, pl.ds(r0, CHUNK), :] = y * lax.rsqrt(ss + EPS) * nw
        return carry

    lax.fori_loop(0, nchunk, chunk, 0, unroll=True)


def _ssd(us, conv_w, conv_b, dtb, alog, dsk, norm_w, blk=512):
    B, S, _ = us.shape
    vec = lambda n: pl.BlockSpec((1, n), lambda b, s: (0, 0))
    return pl.pallas_call(
        functools.partial(_ssd_kernel, blk // CHUNK),
        grid=(B, S // blk),
        in_specs=[pl.BlockSpec((1, blk, SS_COLS), lambda b, s: (b, s, 0)),
                  pl.BlockSpec((SSM_CONV, SSM_CONV_CH), lambda b, s: (0, 0)),
                  vec(SSM_CONV_CH), vec(SSM_W), vec(SSM_W), vec(SSM_W), vec(SSM_W)],
        out_specs=pl.BlockSpec((1, blk, SSM_W), lambda b, s: (b, s, 0)),
        out_shape=jax.ShapeDtypeStruct((B, S, SSM_W), F32),
        scratch_shapes=[pltpu.VMEM((SSM_GN, SSM_W), F32),
                        pltpu.VMEM((blk + 8, SSM_CONV_CH), F32),
                        pltpu.VMEM((blk, SSM_CONV_CH), F32)],
        compiler_params=_cparams(("parallel", "arbitrary")),
        name="ssd",
    )(us, conv_w, conv_b, dtb, alog, dsk, norm_w)


def _t5_bucket(dist):
    max_exact = NUM_BUCKETS // 2
    d = np.maximum(dist, 1).astype(np.float32)
    large = max_exact + (np.log(d / max_exact) / np.log(MAX_DISTANCE / max_exact)
                         * (NUM_BUCKETS - max_exact)).astype(np.int32)
    large = np.minimum(large, NUM_BUCKETS - 1)
    return np.where(dist < max_exact, dist, large).astype(np.int32)


def _bucket_tables():
    qi = np.arange(ATTN_BLOCK)[:, None]
    kj = np.arange(2 * ATTN_BLOCK)[None, :]
    delta = qi - kj + ATTN_BLOCK
    return np.stack([_t5_bucket(np.maximum(delta, 0) * dil) for _, dil in DILATED_PAIRS])


def _bias_kernel(rb_ref, bk_ref, o_ref):
    bk = bk_ref[0]
    for h in range(ATTN_HEADS):
        acc = jnp.zeros(bk.shape, F32)
        for b in range(NUM_BUCKETS):
            acc = jnp.where(bk == b, rb_ref[b, h], acc)
        o_ref[0, h] = acc


def _bias_tables(rel_bias):
    bk = jnp.asarray(_bucket_tables())
    nbr = len(DILATED_PAIRS)
    return pl.pallas_call(
        _bias_kernel,
        grid=(nbr,),
        in_specs=[pl.BlockSpec(memory_space=pltpu.SMEM),
                  pl.BlockSpec((1, ATTN_BLOCK, 2 * ATTN_BLOCK), lambda i: (i, 0, 0))],
        out_specs=pl.BlockSpec((1, ATTN_HEADS, ATTN_BLOCK, 2 * ATTN_BLOCK), lambda i: (i, 0, 0, 0)),
        out_shape=jax.ShapeDtypeStruct((nbr, ATTN_HEADS, ATTN_BLOCK, 2 * ATTN_BLOCK), F32),
        compiler_params=_cparams(("parallel",)),
        name="attn_bias",
    )(rel_bias, bk)


_ATTN_GROUP = 2


def _attn_kernel(ua_ref, qw_ref, kw_ref, bias_ref, o_ref, qn_ref, kn_ref, vv_ref, num_ref, den_ref, mx_ref):
    W = ATTN_W
    S = ua_ref.shape[1]
    blk = ATTN_BLOCK
    seg = _seg_ones(W, HEAD_DIM).astype(BF16)
    qw = qw_ref[...]
    kw = kw_ref[...]

    def nrm(x, w):
        ss = _mm(x * x, seg) * (1.0 / HEAD_DIM)
        return x * lax.rsqrt(ss + EPS) * w

    npair = ATTN_HEADS // 2
    pw = 2 * HEAD_DIM

    def norm_block(i, carry):
        rows = pl.ds(pl.multiple_of(i * blk, blk), blk)
        qn = nrm(ua_ref[0, rows, 0:W], qw) * (HEAD_DIM ** -0.5)
        kn = nrm(ua_ref[0, rows, W:2 * W], kw)
        for pp in range(npair):
            qn_ref[pp, rows, :] = qn[:, pp * pw:(pp + 1) * pw]
            kn_ref[pp, rows, :] = kn[:, pp * pw:(pp + 1) * pw]
            vv_ref[pp, rows, :] = ua_ref[0, rows, 2 * W + pp * pw:2 * W + (pp + 1) * pw]
        return carry

    lax.fori_loop(0, S // blk, norm_block, 0)

    lane = _iota((1, pw), 1)
    low = lane < HEAD_DIM
    half = [jnp.where(low, 1.0, 0.0).astype(F32), jnp.where(low, 0.0, 1.0).astype(F32)]

    order = sorted(range(len(DILATED_PAIRS)), key=lambda i: -DILATED_PAIRS[i][1])
    for br in order:
        window, dil = DILATED_PAIRS[br]
        first = br == order[0]
        n_back = window // dil
        nb = S // dil // blk
        has_prev = nb > 1
        grp = 1 if has_prev else _ATTN_GROUP
        nq = grp * blk
        nk = 2 * blk if has_prev else nq
        qi = _iota((nq, nk), 0)
        kj = _iota((nq, nk), 1) + (0 if has_prev else blk)
        if has_prev:
            delta = qi - kj + blk
            band = (delta >= 0) & (delta <= n_back)
        else:
            assert n_back >= blk - 1
            kl = kj - blk
            band = ((qi >> 7) == (kl >> 7)) & ((qi & (blk - 1)) >= (kl & (blk - 1)))

        def rows_at(start, dil=dil):
            if dil == 1:
                return pl.ds(pl.multiple_of(start, blk), blk)
            return pl.ds(start, blk, stride=dil)

        def block(idx, carry, br=br, dil=dil, nb=nb, has_prev=has_prev, kj=kj, band=band, rows_at=rows_at,
                  first=first, grp=grp):
            valid = band
            if has_prev:
                r = idx // nb
                n = idx % nb
                rqs = [rows_at(r + dil * blk * n)]
                rp = rows_at(r + dil * blk * jnp.maximum(n - 1, 0))
                valid = band & (kj >= jnp.where(n > 0, 0, blk))
            else:
                rqs = [rows_at(idx * grp + g) for g in range(grp)]
            for pp in range(npair):
                qp = jnp.concatenate([qn_ref[pp, rq, :] for rq in rqs], axis=0)
                kp = jnp.concatenate([kn_ref[pp, rq, :] for rq in rqs], axis=0)
                vp = jnp.concatenate([vv_ref[pp, rq, :] for rq in rqs], axis=0)
                if has_prev:
                    kp = jnp.concatenate([kn_ref[pp, rp, :], kp], axis=0)
                    vp = jnp.concatenate([vv_ref[pp, rp, :], vp], axis=0)
                oh, lh = [], []
                for hh in range(2):
                    bias = bias_ref[br, 2 * pp + hh]
                    if not has_prev:
                        row = jnp.concatenate([bias[:, blk:]] * grp, axis=1)
                        bias = jnp.concatenate([row] * grp, axis=0)
                    s = jnp.where(valid, _mm_nt(qp * half[hh], kp) + bias, NEG)
                    m = jnp.max(s, axis=-1, keepdims=True)
                    p = jnp.exp(s - m)
                    l = jnp.sum(p, axis=-1, keepdims=True)
                    oh.append(_mm(p, vp) / l)
                    lh.append(m + jnp.log(l))
                o_all = jnp.where(low, oh[0], oh[1])
                lse_all = jnp.where(low, lh[0], lh[1])
                for g, rq in enumerate(rqs):
                    o = o_all[g * blk:(g + 1) * blk]
                    lse = lse_all[g * blk:(g + 1) * blk]
                    if first:
                        num_ref[pp, rq, :] = o
                        den_ref[pp, rq, :] = jnp.ones_like(o)
                        mx_ref[pp, rq, :] = lse
                    else:
                        m_old = mx_ref[pp, rq, :]
                        m_new = jnp.maximum(m_old, lse)
                        wa = jnp.exp(m_old - m_new)
                        wb = jnp.exp(lse - m_new)
                        num_ref[pp, rq, :] = num_ref[pp, rq, :] * wa + wb * o
                        den_ref[pp, rq, :] = den_ref[pp, rq, :] * wa + wb
                        mx_ref[pp, rq, :] = m_new
            return carry

        lax.fori_loop(0, dil * nb // grp, block, 0, unroll=2 if has_prev else 1)

    def finish(i, carry):
        rows = pl.ds(pl.multiple_of(i * blk, blk), blk)
        for pp in range(npair):
            o_ref[0, rows, pp * pw:(pp + 1) * pw] = num_ref[pp, rows, :] / den_ref[pp, rows, :]
        return carry

    lax.fori_loop(0, S // blk, finish, 0)


def _attn(ua3, qw, kw, bias_all):
    B, S, _ = ua3.shape
    nbr = len(DILATED_PAIRS)
    return pl.pallas_call(
        _attn_kernel,
        grid=(B,),
        in_specs=[pl.BlockSpec((1, S, AT_COLS), lambda b: (b, 0, 0)),
                  pl.BlockSpec((1, ATTN_W), lambda b: (0, 0)),
                  pl.BlockSpec((1, ATTN_W), lambda b: (0, 0)),
                  pl.BlockSpec((nbr, ATTN_HEADS, ATTN_BLOCK, 2 * ATTN_BLOCK), lambda b: (0, 0, 0, 0))],
        out_specs=pl.BlockSpec((1, S, ATTN_W), lambda b: (b, 0, 0)),
        out_shape=jax.ShapeDtypeStruct((B, S, ATTN_W), F32),
        scratch_shapes=[pltpu.VMEM((ATTN_HEADS // 2, S, 2 * HEAD_DIM), F32) for _ in range(6)],
        compiler_params=_cparams(("parallel",)),
        name="dilated_attn",
    )(ua3, qw, kw, bias_all)


def _outproj_kernel(x_ref, mh_ref, ma_ref, ms_ref, wo_ref, fw_ref, wq_ref, sk_ref,
                    h_ref, xn_ref, sc_ref):
    W = ATTN_W
    h = (x_ref[...] + _mm(mh_ref[...], wo_ref[0:HGRN_W, :])
         + _mm(ma_ref[...], wo_ref[HGRN_W:HGRN_W + W, :])
         + _mm(ms_ref[...], wo_ref[HGRN_W + W:, :]))
    h_ref[...] = h
    xn = (h * lax.rsqrt(jnp.mean(h * h, axis=-1, keepdims=True) + EPS) * fw_ref[...]).astype(BF16)
    xn_ref[...] = xn
    qp = jnp.dot(xn, wq_ref[...], preferred_element_type=F32)
    for hd in range(PEER_HEADS):
        qh = qp[:, 2 * PEER_HALF * hd:2 * PEER_HALF * (hd + 1)]
        for c in range(2):
            sc_ref[hd, c] = _mm_nt(sk_ref[c], qh)


def _outproj(x2, mh, ma, ms, w_out, ffn_w, wq, sk_pad, tm=512):
    T, D = x2.shape
    row = lambda n: pl.BlockSpec((tm, n), lambda i: (i, 0))
    full = lambda a: pl.BlockSpec(a.shape, lambda i: (0,) * a.ndim)
    return pl.pallas_call(
        _outproj_kernel,
        grid=(T // tm,),
        in_specs=[row(D), row(HGRN_W), row(ATTN_W), row(SSM_W),
                  full(w_out), pl.BlockSpec((1, D), lambda i: (0, 0)), full(wq), full(sk_pad)],
        out_specs=[row(D), row(D),
                   pl.BlockSpec((PEER_HEADS, 2, PEER_NKEYS, tm), lambda i: (0, 0, 0, i))],
        out_shape=[jax.ShapeDtypeStruct((T, D), F32), jax.ShapeDtypeStruct((T, D), BF16),
                   jax.ShapeDtypeStruct((PEER_HEADS, 2, PEER_NKEYS, T), F32)],
        compiler_params=_cparams(("parallel",)),
        name="outproj_query",
    )(x2, mh, ma, ms, w_out, ffn_w.reshape(1, D), wq, sk_pad)


_NSEL = PEER_TOPK + 1
_PAIRS = [(i, j) for i in range(_NSEL) for j in range(_NSEL) if (i + 1) * (j + 1) <= _NSEL]


def _top_values(v, k):
    out = []
    rank = jnp.full(v.shape, float(k), F32)
    for r in range(k):
        m = jnp.max(v, axis=0, keepdims=True)
        out.append(m)
        hit = v == m
        rank = jnp.where(hit, float(r), rank)
        v = jnp.where(hit, NEG, v)
    return out, rank


def _pack_pair(lo, hi):
    lo_b = lax.bitcast_convert_type(lo.astype(BF16).astype(F32), jnp.uint32) >> 16
    hi_b = lax.bitcast_convert_type(hi.astype(BF16).astype(F32), jnp.uint32) & jnp.uint32(0xFFFF0000)
    return hi_b | lo_b


def _route_kernel(sc_ref, n1_ref, u1_ref, rk2_ref, u2_ref):
    tt = sc_ref.shape[-1]
    row8 = _iota((8, tt), 0)

    def head(hd):
        s1 = sc_ref[hd, 0]
        s2 = sc_ref[hd, 1]
        a, _ = _top_values(s1, _NSEL)
        b, rank2 = _top_values(s2, _NSEL)
        groups = []
        for g0 in range(0, len(_PAIRS), 8):
            cg = jnp.full((8, tt), NEG, F32)
            for p, (i, j) in enumerate(_PAIRS[g0:g0 + 8]):
                cg = jnp.where(row8 == p, a[i] + b[j], cg)
            groups.append(cg)
        best, _ = _top_values(jnp.concatenate(groups, axis=0), _NSEL)
        zsum = jnp.ones((1, tt), F32)
        for r in range(1, PEER_TOPK):
            zsum = zsum + jnp.exp(best[r] - best[0])
        cut = 0.5 * (best[PEER_TOPK - 1] + best[PEER_TOPK])
        need = cut - s1
        n1 = jnp.zeros_like(s1)
        for r in range(PEER_TOPK):
            n1 = n1 + jnp.where(b[r] >= need, 1.0, 0.0)
        return n1, jnp.exp(s1 - a[0]) / zsum, rank2, jnp.exp(s2 - b[0])

    def pair(hp, carry):
        lo = head(2 * hp)
        hi = head(2 * hp + 1)
        for ref, x, y in zip((n1_ref, u1_ref, rk2_ref, u2_ref), lo, hi):
            ref[hp] = _pack_pair(x, y)
        return carry

    lax.fori_loop(0, PEER_HEADS // 2, pair, 0)


def _route(sc, tt=256):
    T = sc.shape[-1]
    blk = pl.BlockSpec((PEER_HEADS // 2, PEER_NKEYS, tt), lambda i: (0, 0, i))
    shp = jax.ShapeDtypeStruct((PEER_HEADS // 2, PEER_NKEYS, T), jnp.uint32)
    return pl.pallas_call(
        _route_kernel,
        grid=(T // tt,),
        in_specs=[pl.BlockSpec((PEER_HEADS, 2, PEER_NKEYS, tt), lambda i: (0, 0, 0, i))],
        out_specs=[blk, blk, blk, blk],
        out_shape=[shp, shp, shp, shp],
        compiler_params=_cparams(("parallel",)),
        name="peer_route",
    )(sc)


_GELU_C1 = -2.0 * math.sqrt(2.0 / math.pi)
_GELU_C2 = _GELU_C1 * 0.044715
_PEER_EB = 512
_PEER_LANES = 128
_PEER_ROWS = 32
_PEER_FLAGS = None


def _as_halves(words):
    return pltpu.bitcast(words, BF16)


def _peer_kernel(nj, xn_ref, dn_ref, upt_ref, n1_ref, u1_ref, rk2_ref, u2_ref, h_ref, o_ref,
                 acc_ref, at0_ref, at1_ref, wg0_ref, wg1_ref):
    j = pl.program_id(1)
    tt = xn_ref.shape[0]
    te = dn_ref.shape[0]
    nb = te // _PEER_EB
    nh = tt // _PEER_TOK
    gpb = _PEER_EB // PEER_NKEYS
    at = (at0_ref, at1_ref)
    wg = (wg0_ref, wg1_ref)

    @pl.when(j == 0)
    def _():
        acc_ref[...] = jnp.zeros_like(acc_ref)

    def scores(k, at_ref):
        rows = pl.ds(pl.multiple_of(k * _PEER_EB, _PEER_EB), _PEER_EB)
        for hf in range(nh):
            at_ref[hf] = lax.dot_general(dn_ref[rows, :], xn_ref[hf * _PEER_TOK:(hf + 1) * _PEER_TOK, :],
                                         (((1,), (1,)), ((), ())), preferred_element_type=F32)

    def weigh(k, at_ref, wg_ref):
        for g in range(gpb):
            c = (j * nb + k) * gpb + g
            er = slice(g * PEER_NKEYS, (g + 1) * PEER_NKEYS)
            for hf in range(nh):
                tw = slice(hf * _PEER_TOK, (hf + 1) * _PEER_TOK)
                n1s = [n1_ref[hp, pl.ds(c, 1), tw] for hp in range(PEER_HEADS // 2)]
                u1s = [u1_ref[hp, pl.ds(c, 1), tw] for hp in range(PEER_HEADS // 2)]
                for lt in range(_PEER_TOK // _PEER_LANES):
                    ls = slice(lt * _PEER_LANES, (lt + 1) * _PEER_LANES)
                    tl = slice(hf * _PEER_TOK + lt * _PEER_LANES, hf * _PEER_TOK + (lt + 1) * _PEER_LANES)
                    w = None
                    for hp in range(PEER_HEADS // 2):
                        n1 = _as_halves(jnp.broadcast_to(n1s[hp][:, ls], (PEER_NKEYS, _PEER_LANES)))
                        u1 = _as_halves(jnp.broadcast_to(u1s[hp][:, ls], (PEER_NKEYS, _PEER_LANES)))
                        rk2 = _as_halves(rk2_ref[hp, :, tl])
                        u2 = _as_halves(u2_ref[hp, :, tl])
                        t = jnp.where(rk2 < n1, u2, jnp.zeros((), BF16)) * u1
                        w = t if w is None else w + t
                    wu = pltpu.bitcast(w, jnp.uint32)
                    wf = (lax.bitcast_convert_type(wu << 16, F32)
                          + lax.bitcast_convert_type(wu & jnp.uint32(0xFFFF0000), F32))
                    a = at_ref[hf, er, ls]
                    gl = a / (1.0 + jnp.exp(a * (_GELU_C1 + _GELU_C2 * (a * a))))
                    wg_ref[hf, er, ls] = (wf * gl).astype(BF16)

    def project(k, wg_ref):
        for hf in range(nh):
            tw = slice(hf * _PEER_TOK, (hf + 1) * _PEER_TOK)
            acc_ref[:, tw] += jnp.dot(upt_ref[k], wg_ref[hf], preferred_element_type=F32)

    scores(0, at[0])
    scores(1, at[1])
    weigh(0, at[0], wg[0])

    def body(m, carry):
        k = 2 * m + 1
        scores(k + 1, at[0])
        weigh(k, at[1], wg[1])
        project(k - 1, wg[0])
        scores(k + 2, at[1])
        weigh(k + 1, at[0], wg[0])
        project(k, wg[1])
        return carry

    lax.fori_loop(0, nb // 2 - 1, body, 0)
    weigh(nb - 1, at[1], wg[1])
    project(nb - 2, wg[0])
    project(nb - 1, wg[1])

    @pl.when(j == nj - 1)
    def _():
        o_ref[...] = h_ref[...] + acc_ref[...].T


def _peer(xn, down_b, layer, up_t3, n1, u1, rk2, u2, h, tt=512, te=2048):
    T, D = xn.shape
    E = down_b.shape[1]
    nj = E // te
    nh = tt // _PEER_TOK
    rt = pl.BlockSpec((PEER_HEADS // 2, PEER_NKEYS, tt), lambda i, j: (0, 0, i))
    return pl.pallas_call(
        functools.partial(_peer_kernel, nj),
        grid=(T // tt, nj),
        in_specs=[pl.BlockSpec((tt, D), lambda i, j: (i, 0)),
                  pl.BlockSpec((None, te, D), lambda i, j: (layer, j, 0)),
                  pl.BlockSpec((te // _PEER_EB, D, _PEER_EB), lambda i, j: (j, 0, 0)),
                  rt, rt, rt, rt,
                  pl.BlockSpec((tt, D), lambda i, j: (i, 0))],
        out_specs=pl.BlockSpec((tt, D), lambda i, j: (i, 0)),
        out_shape=jax.ShapeDtypeStruct((T, D), F32),
        scratch_shapes=[pltpu.VMEM((D, tt), F32),
                        pltpu.VMEM((nh, _PEER_EB, _PEER_TOK), F32), pltpu.VMEM((nh, _PEER_EB, _PEER_TOK), F32),
                        pltpu.VMEM((nh, _PEER_EB, _PEER_TOK), BF16), pltpu.VMEM((nh, _PEER_EB, _PEER_TOK), BF16)],
        compiler_params=_cparams(("parallel", "arbitrary"), flags=_PEER_FLAGS),
        name="peer_experts",
    )(xn, down_b, up_t3, n1, u1, rk2, u2, h)


def _transpose_cast_kernel(x_ref, o_ref):
    o_ref[0] = x_ref[...].T.astype(o_ref.dtype)


def _transpose_blocks_bf16(x, layer, tr=_PEER_EB):
    _, R, C = x.shape
    return pl.pallas_call(
        _transpose_cast_kernel,
        grid=(R // tr,),
        in_specs=[pl.BlockSpec((None, tr, C), lambda i: (layer, i, 0))],
        out_specs=pl.BlockSpec((1, C, tr), lambda i: (i, 0, 0)),
        out_shape=jax.ShapeDtypeStruct((R // tr, C, tr), BF16),
        compiler_params=_cparams(("parallel",)),
        name="transpose_cast",
    )(x)


def _rep(v, n):
    return jnp.repeat(v, n, axis=-1)


def _layer(x2, B, S, layer, p, bias_all):
    T, D = x2.shape
    uh, ua, us = _inproj(x2, p["attn_norm_w"], p["w_in_pad"], layer)

    mh = _hgrn(uh.reshape(B, S, HG_COLS), p["hgrn_lower_bounds"],
               jnp.tile(p["hgrn_norm_w"], HGRN_HEADS).reshape(1, HGRN_W), layer)
    qw = jnp.tile(p["q_norm_w"], ATTN_HEADS).reshape(1, ATTN_W)
    kw = jnp.tile(p["k_norm_w"], ATTN_HEADS).reshape(1, ATTN_W)
    ma = _attn(ua.reshape(B, S, AT_COLS), qw, kw, bias_all)
    ms = _ssd(us.reshape(B, S, SS_COLS), p["ssm_conv_w"], p["ssm_conv_b"].reshape(1, -1),
              _rep(p["ssm_dt_bias"], HEAD_DIM).reshape(1, -1), _rep(p["ssm_a_log"], HEAD_DIM).reshape(1, -1),
              _rep(p["ssm_d"], HEAD_DIM).reshape(1, -1), p["ssm_norm_w"].reshape(1, -1))

    sk = p["peer_sub_keys"]
    zk = jnp.zeros_like(sk[0])
    sk_pad = jnp.stack([jnp.concatenate([sk[0], zk], axis=1), jnp.concatenate([zk, sk[1]], axis=1)])
    h, xn, sc = _outproj(x2, mh.reshape(T, HGRN_W), ma.reshape(T, ATTN_W), ms.reshape(T, SSM_W),
                         p["w_out"].astype(BF16), p["ffn_norm_w"], p["peer_w_query"].astype(BF16), sk_pad)
    n1, u1, rk2, u2 = _route(sc)
    return _peer(xn, p["peer_down_b"], layer, _transpose_blocks_bf16(p["peer_up"], layer), n1, u1, rk2, u2, h)


_PER_LAYER = ("attn_norm_w", "hgrn_norm_w", "q_norm_w", "k_norm_w", "ssm_conv_w", "ssm_conv_b",
              "ssm_dt_bias", "ssm_a_log", "ssm_d", "ssm_norm_w", "w_out", "ffn_norm_w", "peer_w_query",
              "peer_sub_keys")


def kernel(x, attn_norm_w, w_in, hgrn_lower_bounds, hgrn_norm_w, q_norm_w, k_norm_w, rel_bias, ssm_conv_w, ssm_conv_b, ssm_dt_bias, ssm_a_log, ssm_d, ssm_norm_w, w_out, ffn_norm_w, peer_w_query, peer_sub_keys, peer_down, peer_up):
    stacked = dict(attn_norm_w=attn_norm_w, w_in=w_in, hgrn_norm_w=hgrn_norm_w, q_norm_w=q_norm_w,
                   k_norm_w=k_norm_w, ssm_conv_w=ssm_conv_w, ssm_conv_b=ssm_conv_b, ssm_dt_bias=ssm_dt_bias,
                   ssm_a_log=ssm_a_log, ssm_d=ssm_d, ssm_norm_w=ssm_norm_w, w_out=w_out, ffn_norm_w=ffn_norm_w,
                   peer_w_query=peer_w_query, peer_sub_keys=peer_sub_keys, peer_down=peer_down, peer_up=peer_up)
    B, S, D = x.shape
    bias_all = _bias_tables(rel_bias)
    down_b = peer_down.astype(BF16)
    lane_pad = -w_in.shape[2] % 128
    w_in_pad = jnp.pad(w_in, ((0, 0), (0, 0), (0, lane_pad))).astype(BF16)
    x2 = x.reshape(B * S, D)
    for layer in range(w_in.shape[0]):
        p = {k: stacked[k][layer] for k in _PER_LAYER}
        p["w_in_pad"] = w_in_pad
        p["hgrn_lower_bounds"] = hgrn_lower_bounds
        p["peer_down_b"] = down_b
        p["peer_up"] = peer_up
        x2 = _layer(x2, B, S, layer, p, bias_all)
    return x2.reshape(B, S, D)
```

```python
import functools
import math

import numpy as np
import jax
import jax.numpy as jnp
from jax import lax
from jax.experimental import pallas as pl
from jax.experimental.pallas import tpu as pltpu

F32 = jnp.float32
BF16 = jnp.bfloat16
HIGHEST = lax.Precision.HIGHEST
NEG = -1e30
EPS = 1e-6

HEAD_DIM = 64
CHUNK = 64
HGRN_HEADS = 4
HGRN_W = HGRN_HEADS * HEAD_DIM
ATTN_HEADS = 6
ATTN_W = ATTN_HEADS * HEAD_DIM
ATTN_BLOCK = 128
DILATED_PAIRS = ((128, 1), (512, 4), (2048, 16))
NUM_BUCKETS = 32
MAX_DISTANCE = 2048
SSM_HEADS = 6
SSM_W = SSM_HEADS * HEAD_DIM
SSM_GROUPS = 2
SSM_STATE = 128
SSM_GN = SSM_GROUPS * SSM_STATE
SSM_CONV = 4
SSM_CONV_CH = SSM_W + 2 * SSM_GN
PEER_HEADS = 8
PEER_NKEYS = 128
PEER_HALF = 64
PEER_TOPK = 16
_PEER_TOK = 256
HG_COLS = 4 * HGRN_W
AT_COLS = 3 * ATTN_W
SS_COLS = SSM_W + SSM_CONV_CH + SSM_W

VMEM_LIMIT = 56 * 1024 * 1024


def _cparams(sem, flags=None):
    return pltpu.CompilerParams(dimension_semantics=sem, vmem_limit_bytes=VMEM_LIMIT, flags=flags)


def _iota(shape, dim):
    return lax.broadcasted_iota(jnp.int32, shape, dim)


def _mm(a, b):
    return jnp.dot(a.astype(BF16), b.astype(BF16), preferred_element_type=F32)


def _mm_nt(a, b):
    return lax.dot_general(a.astype(BF16), b.astype(BF16), (((1,), (1,)), ((), ())),
                           preferred_element_type=F32)


def _mm_tn(a, b):
    return lax.dot_general(a.astype(BF16), b.astype(BF16), (((0,), (0,)), ((), ())),
                           preferred_element_type=F32)


def _mm_hi(a, b):
    return jnp.dot(a, b, precision=HIGHEST, preferred_element_type=F32)


def _sigmoid(x):
    return 1.0 / (1.0 + jnp.exp(-x))


def _silu(x):
    return x * _sigmoid(x)


def _softplus(x):
    return jnp.maximum(x, 0.0) + jnp.log(1.0 + jnp.exp(-jnp.abs(x)))


def _seg_ones(n, seg):
    r = _iota((n, n), 0)
    c = _iota((n, n), 1)
    same = None
    for s in range(n // seg):
        lo, hi = s * seg, (s + 1) * seg
        t = (r >= lo) & (r < hi) & (c >= lo) & (c < hi)
        same = t if same is None else (same | t)
    return jnp.where(same, 1.0, 0.0).astype(F32)


def _lane_seg_mask(width, seg, idx):
    l = _iota((1, width), 1)
    return jnp.where((l >= idx * seg) & (l < (idx + 1) * seg), 1.0, 0.0).astype(F32)


def _inproj_kernel(x_ref, nw_ref, w_ref, oh_ref, oa_ref, os_ref):
    x = x_ref[...]
    xn = x * lax.rsqrt(jnp.mean(x * x, axis=-1, keepdims=True) + EPS) * nw_ref[...]
    xb = xn.astype(BF16)
    c0, c1, c2 = HG_COLS, HG_COLS + AT_COLS, HG_COLS + AT_COLS + SSM_W + SSM_CONV_CH
    oh_ref[...] = jnp.dot(xb, w_ref[:, 0:c0], preferred_element_type=F32)
    oa_ref[...] = jnp.dot(xb, w_ref[:, c0:c1], preferred_element_type=F32)
    os_ref[:, 0:c2 - c1] = jnp.dot(xb, w_ref[:, c1:c2], preferred_element_type=F32)
    dt = jnp.dot(xb, w_ref[:, c2:], preferred_element_type=F32)
    lanes = w_ref.shape[1] - c2
    expand = jnp.where((_iota((lanes, SSM_W), 1) >> 6) == _iota((lanes, SSM_W), 0), 1.0, 0.0).astype(F32)
    os_ref[:, c2 - c1:] = _mm_hi(dt, expand)


def _inproj(x2, norm_w, w_pad, layer, tm=512):
    T, D = x2.shape
    ncol = w_pad.shape[2]
    return pl.pallas_call(
        _inproj_kernel,
        grid=(T // tm,),
        in_specs=[pl.BlockSpec((tm, D), lambda i: (i, 0)),
                  pl.BlockSpec((1, D), lambda i: (0, 0)),
                  pl.BlockSpec((None, D, ncol), lambda i: (layer, 0, 0))],
        out_specs=[pl.BlockSpec((tm, HG_COLS), lambda i: (i, 0)),
                   pl.BlockSpec((tm, AT_COLS), lambda i: (i, 0)),
                   pl.BlockSpec((tm, SS_COLS), lambda i: (i, 0))],
        out_shape=[jax.ShapeDtypeStruct((T, HG_COLS), F32),
                   jax.ShapeDtypeStruct((T, AT_COLS), F32),
                   jax.ShapeDtypeStruct((T, SS_COLS), F32)],
        compiler_params=_cparams(("parallel",)),
        name="inproj",
    )(x2, norm_w.reshape(1, D), w_pad)


def _hgrn_kernel(layer, nchunk, u_ref, lbs_ref, nw_ref, o_ref, st_ref):
    W = HGRN_W

    @pl.when(pl.program_id(1) == 0)
    def _():
        st_ref[...] = jnp.zeros_like(st_ref)

    lbs = lbs_ref[...]
    e = jnp.exp(lbs - jnp.max(lbs, axis=0, keepdims=True))
    sm = e / jnp.sum(e, axis=0, keepdims=True)
    lb = jnp.zeros((1, W), F32)
    for j in range(1, layer + 1):
        lb = lb + sm[j:j + 1]
    nw = nw_ref[...]

    tril = jnp.where(_iota((CHUNK, CHUNK), 0) >= _iota((CHUNK, CHUNK), 1), 1.0, 0.0).astype(F32)
    row = _iota((CHUNK, W), 0)
    hm = [_lane_seg_mask(W, HEAD_DIM, h) for h in range(HGRN_HEADS)]
    hm3 = [jnp.concatenate([m, m, m], axis=1) for m in hm]
    seg = _seg_ones(W, HEAD_DIM)
    seg_b = seg.astype(BF16)
    tt16 = _iota((16, W), 0)
    z32 = jnp.zeros((32, W), F32)
    in_q2 = (row & 16) != 0

    def chunk(c, carry):
        r0 = pl.multiple_of(c * CHUNK, CHUNK)
        u = u_ref[0, pl.ds(r0, CHUNK), :]
        q = u[:, 0:W] * (HEAD_DIM ** -0.5)
        f = u[:, W:2 * W]
        iv = u[:, 2 * W:3 * W]
        g = u[:, 3 * W:4 * W]
        forget = lb + (1.0 - lb) * _sigmoid(f)
        logf = jnp.log(forget)
        kk = 1.0 - forget
        b = _mm_hi(tril, logf)
        b16, b32, b48, bend = b[16:17], b[32:33], b[48:49], b[63:64]

        q1 = q * jnp.exp(jnp.where(row >= 32, b - b32, NEG))
        k1 = kk * jnp.exp(jnp.where(row < 32, b32 - b, NEG))
        ref2 = jnp.where(row < 32, b16, b48)
        q2 = q * jnp.exp(jnp.where(in_q2, b - ref2, NEG))
        k2 = kk * jnp.exp(jnp.where(in_q2, NEG, ref2 - b))
        qcat = jnp.concatenate([q1, jnp.concatenate([q2[:32], z32], 0), jnp.concatenate([z32, q2[32:]], 0)], 1)
        kcat = jnp.concatenate([k1, jnp.concatenate([k2[:32], z32], 0), jnp.concatenate([z32, k2[32:]], 0)], 1)
        qstack = jnp.concatenate([qcat * hm3[h] for h in range(HGRN_HEADS)], axis=0)
        a = _mm_nt(qstack, kcat)
        ofull = _mm(a, iv)
        o = ofull[0:CHUNK] * hm[0]
        for h in range(1, HGRN_HEADS):
            o = o + ofull[h * CHUNK:(h + 1) * CHUNK] * hm[h]

        st = st_ref[...]
        o = o + _mm_nt(q * jnp.exp(b), st)

        diag = []
        for j in range(4):
            bR = b[16 * j:16 * j + 16]
            qR = q[16 * j:16 * j + 16]
            ps = []
            for s in range(16):
                r = 16 * j + s
                ps.append(qR * kk[r:r + 1] * jnp.exp(jnp.where(tt16 >= s, bR - b[r:r + 1], NEG)))
            ab = jnp.dot(jnp.concatenate(ps, axis=0).astype(BF16), seg_b, preferred_element_type=F32)
            od = ab[0:16] * iv[16 * j:16 * j + 1]
            for s in range(1, 16):
                od = od + ab[16 * s:16 * s + 16] * iv[16 * j + s:16 * j + s + 1]
            diag.append(od)
        o = o + jnp.concatenate(diag, axis=0)

        kend = kk * jnp.exp(bend - b)
        st_ref[...] = st * jnp.exp(bend) + _mm_tn(iv, kend) * seg

        ss = _mm(o * o, seg_b) * (1.0 / HEAD_DIM)
        o_ref[0, pl.ds(r0, CHUNK), :] = o * lax.rsqrt(ss + EPS) * nw * _silu(g)
        return carry

    lax.fori_loop(0, nchunk, chunk, 0, unroll=True)


def _hgrn(uh, lbs, norm_w, layer, blk=512):
    B, S, _ = uh.shape
    return pl.pallas_call(
        functools.partial(_hgrn_kernel, layer, blk // CHUNK),
        grid=(B, S // blk),
        in_specs=[pl.BlockSpec((1, blk, HG_COLS), lambda b, s: (b, s, 0)),
                  pl.BlockSpec(lbs.shape, lambda b, s: (0, 0)),
                  pl.BlockSpec((1, HGRN_W), lambda b, s: (0, 0))],
        out_specs=pl.BlockSpec((1, blk, HGRN_W), lambda b, s: (b, s, 0)),
        out_shape=jax.ShapeDtypeStruct((B, S, HGRN_W), F32),
        scratch_shapes=[pltpu.VMEM((HGRN_W, HGRN_W), F32)],
        compiler_params=_cparams(("parallel", "arbitrary")),
        name="hgrn2",
    )(uh, lbs, norm_w)


def _ssd_kernel(nchunk, u_ref, cw_ref, cb_ref, dtb_ref, alog_ref, d_ref, nw_ref, o_ref,
                st_ref, ext_ref, xc_ref):
    W = SSM_W
    blk = nchunk * CHUNK
    first = pl.program_id(1) == 0

    @pl.when(first)
    def _():
        st_ref[...] = jnp.zeros_like(st_ref)
        ext_ref[0:8, :] = jnp.zeros((8, SSM_CONV_CH), F32)

    @pl.when(jnp.logical_not(first))
    def _():
        ext_ref[0:8, :] = ext_ref[blk:blk + 8, :]

    ext_ref[8:8 + blk, :] = u_ref[0, :, W:W + SSM_CONV_CH]
    cw = cw_ref[...]
    conv = cb_ref[...] + cw[0:1] * ext_ref[5:5 + blk, :]
    for j in range(1, SSM_CONV):
        conv = conv + cw[j:j + 1] * ext_ref[5 + j:5 + j + blk, :]
    xc_ref[...] = _silu(conv)

    dtb = dtb_ref[...]
    a_neg = -jnp.exp(alog_ref[...])
    dsk = d_ref[...]
    nw = nw_ref[...]
    tril = jnp.where(_iota((CHUNK, CHUNK), 0) >= _iota((CHUNK, CHUNK), 1), 1.0, 0.0).astype(F32)
    t_i = _iota((CHUNK, W), 0)
    s_i = _iota((CHUNK, W), 1) & (HEAD_DIM - 1)
    strict = jnp.where(t_i > s_i, 1.0, 0.0).astype(F32)
    lower = t_i >= s_i
    hm = [_lane_seg_mask(W, HEAD_DIM, h) for h in range(SSM_HEADS)]
    zB = jnp.zeros((CHUNK, SSM_STATE), F32)
    gr = _iota((SSM_GN, W), 0) >= SSM_STATE
    gc = _iota((SSM_GN, W), 1) >= (SSM_HEADS // SSM_GROUPS) * HEAD_DIM
    gmask = jnp.where(gr == gc, 1.0, 0.0).astype(F32)
    gw = W // SSM_GROUPS
    seg = _seg_ones(W, gw).astype(BF16)

    def chunk(c, carry):
        r0 = pl.multiple_of(c * CHUNK, CHUNK)
        z = u_ref[0, pl.ds(r0, CHUNK), 0:W]
        dtr = u_ref[0, pl.ds(r0, CHUNK), W + SSM_CONV_CH:W + SSM_CONV_CH + W]
        xc = xc_ref[pl.ds(r0, CHUNK), :]
        xs = xc[:, 0:W]
        bm = xc[:, W:W + SSM_GN]
        cm = xc[:, W + SSM_GN:W + 2 * SSM_GN]
        dt = _softplus(dtr + dtb)
        a = dt * a_neg
        dd = _mm_hi(tril, jnp.concatenate([a * strict, a], axis=1))
        dseg = dd[:, 0:W]
        cum = dd[:, W:2 * W]
        decay = jnp.exp(jnp.where(lower, dseg, NEG))
        b0 = jnp.concatenate([bm[:, 0:SSM_STATE], zB], axis=1)
        b1 = jnp.concatenate([zB, bm[:, SSM_STATE:]], axis=1)
        bstack = jnp.concatenate([b0, b0, b0, b1, b1, b1], axis=0)
        scores = _mm_nt(cm, bstack) * decay
        xdt = xs * dt
        xbd = jnp.concatenate([xdt * hm[h] for h in range(SSM_HEADS)], axis=0)
        y = _mm(scores, xbd)
        st = st_ref[...]
        y = y + _mm(cm, st) * jnp.exp(cum)
        cend = cum[CHUNK - 1:CHUNK]
        to_end = jnp.exp(cend - cum) * dt
        st_ref[...] = st * jnp.exp(cend) + _mm_tn(bm, to_end * xs) * gmask
        y = y + dsk * xs
        y = y * _silu(z)
        ss = _mm(y * y, seg) * (1.0 / gw)
        o_ref[0, pl.ds(r0, CHUNK), :] = y * lax.rsqrt(ss + EPS) * nw
        return carry

    lax.fori_loop(0, nchunk, chunk, 0, unroll=True)


def _ssd(us, conv_w, conv_b, dtb, alog, dsk, norm_w, blk=512):
    B, S, _ = us.shape
    vec = lambda n: pl.BlockSpec((1, n), lambda b, s: (0, 0))
    return pl.pallas_call(
        functools.partial(_ssd_kernel, blk // CHUNK),
        grid=(B, S // blk),
        in_specs=[pl.BlockSpec((1, blk, SS_COLS), lambda b, s: (b, s, 0)),
                  pl.BlockSpec((SSM_CONV, SSM_CONV_CH), lambda b, s: (0, 0)),
                  vec(SSM_CONV_CH), vec(SSM_W), vec(SSM_W), vec(SSM_W), vec(SSM_W)],
        out_specs=pl.BlockSpec((1, blk, SSM_W), lambda b, s: (b, s, 0)),
        out_shape=jax.ShapeDtypeStruct((B, S, SSM_W), F32),
        scratch_shapes=[pltpu.VMEM((SSM_GN, SSM_W), F32),
                        pltpu.VMEM((blk + 8, SSM_CONV_CH), F32),
                        pltpu.VMEM((blk, SSM_CONV_CH), F32)],
        compiler_params=_cparams(("parallel", "arbitrary")),
        name="ssd",
    )(us, conv_w, conv_b, dtb, alog, dsk, norm_w)


def _t5_bucket(dist):
    max_exact = NUM_BUCKETS // 2
    d = np.maximum(dist, 1).astype(np.float32)
    large = max_exact + (np.log(d / max_exact) / np.log(MAX_DISTANCE / max_exact)
                         * (NUM_BUCKETS - max_exact)).astype(np.int32)
    large = np.minimum(large, NUM_BUCKETS - 1)
    return np.where(dist < max_exact, dist, large).astype(np.int32)


def _bucket_tables():
    qi = np.arange(ATTN_BLOCK)[:, None]
    kj = np.arange(2 * ATTN_BLOCK)[None, :]
    delta = qi - kj + ATTN_BLOCK
    return np.stack([_t5_bucket(np.maximum(delta, 0) * dil) for _, dil in DILATED_PAIRS])


def _bias_kernel(rb_ref, bk_ref, o_ref):
    bk = bk_ref[0]
    for h in range(ATTN_HEADS):
        acc = jnp.zeros(bk.shape, F32)
        for b in range(NUM_BUCKETS):
            acc = jnp.where(bk == b, rb_ref[b, h], acc)
        o_ref[0, h] = acc


def _bias_tables(rel_bias):
    bk = jnp.asarray(_bucket_tables())
    nbr = len(DILATED_PAIRS)
    return pl.pallas_call(
        _bias_kernel,
        grid=(nbr,),
        in_specs=[pl.BlockSpec(memory_space=pltpu.SMEM),
                  pl.BlockSpec((1, ATTN_BLOCK, 2 * ATTN_BLOCK), lambda i: (i, 0, 0))],
        out_specs=pl.BlockSpec((1, ATTN_HEADS, ATTN_BLOCK, 2 * ATTN_BLOCK), lambda i: (i, 0, 0, 0)),
        out_shape=jax.ShapeDtypeStruct((nbr, ATTN_HEADS, ATTN_BLOCK, 2 * ATTN_BLOCK), F32),
        compiler_params=_cparams(("parallel",)),
        name="attn_bias",
    )(rel_bias, bk)


_ATTN_GROUP = 2


def _attn_kernel(ua_ref, qw_ref, kw_ref, bias_ref, o_ref, qn_ref, kn_ref, vv_ref, num_ref, den_ref, mx_ref):
    W = ATTN_W
    S = ua_ref.shape[1]
    blk = ATTN_BLOCK
    seg = _seg_ones(W, HEAD_DIM).astype(BF16)
    qw = qw_ref[...]
    kw = kw_ref[...]

    def nrm(x, w):
        ss = _mm(x * x, seg) * (1.0 / HEAD_DIM)
        return x * lax.rsqrt(ss + EPS) * w

    npair = ATTN_HEADS // 2
    pw = 2 * HEAD_DIM

    def norm_block(i, carry):
        rows = pl.ds(pl.multiple_of(i * blk, blk), blk)
        qn = nrm(ua_ref[0, rows, 0:W], qw) * (HEAD_DIM ** -0.5)
        kn = nrm(ua_ref[0, rows, W:2 * W], kw)
        for pp in range(npair):
            qn_ref[pp, rows, :] = qn[:, pp * pw:(pp + 1) * pw]
            kn_ref[pp, rows, :] = kn[:, pp * pw:(pp + 1) * pw]
            vv_ref[pp, rows, :] = ua_ref[0, rows, 2 * W + pp * pw:2 * W + (pp + 1) * pw]
        return carry

    lax.fori_loop(0, S // blk, norm_block, 0)

    lane = _iota((1, pw), 1)
    low = lane < HEAD_DIM
    half = [jnp.where(low, 1.0, 0.0).astype(F32), jnp.where(low, 0.0, 1.0).astype(F32)]

    order = sorted(range(len(DILATED_PAIRS)), key=lambda i: -DILATED_PAIRS[i][1])
    for br in order:
        window, dil = DILATED_PAIRS[br]
        first = br == order[0]
        n_back = window // dil
        nb = S // dil // blk
        has_prev = nb > 1
        grp = 1 if has_prev else _ATTN_GROUP
        nq = grp * blk
        nk = 2 * blk if has_prev else nq
        qi = _iota((nq, nk), 0)
        kj = _iota((nq, nk), 1) + (0 if has_prev else blk)
        if has_prev:
            delta = qi - kj + blk
            band = (delta >= 0) & (delta <= n_back)
        else:
            assert n_back >= blk - 1
            kl = kj - blk
            band = ((qi >> 7) == (kl >> 7)) & ((qi & (blk - 1)) >= (kl & (blk - 1)))

        def rows_at(start, dil=dil):
            if dil == 1:
                return pl.ds(pl.multiple_of(start, blk), blk)
            return pl.ds(start, blk, stride=dil)

        def block(idx, carry, br=br, dil=dil, nb=nb, has_prev=has_prev, kj=kj, band=band, rows_at=rows_at,
                  first=first, grp=grp):
            valid = band
            if has_prev:
                r = idx // nb
                n = idx % nb
                rqs = [rows_at(r + dil * blk * n)]
                rp = rows_at(r + dil * blk * jnp.maximum(n - 1, 0))
                valid = band & (kj >= jnp.where(n > 0, 0, blk))
            else:
                rqs = [rows_at(idx * grp + g) for g in range(grp)]
            for pp in range(npair):
                qp = jnp.concatenate([qn_ref[pp, rq, :] for rq in rqs], axis=0)
                kp = jnp.concatenate([kn_ref[pp, rq, :] for rq in rqs], axis=0)
                vp = jnp.concatenate([vv_ref[pp, rq, :] for rq in rqs], axis=0)
                if has_prev:
                    kp = jnp.concatenate([kn_ref[pp, rp, :], kp], axis=0)
                    vp = jnp.concatenate([vv_ref[pp, rp, :], vp], axis=0)
                oh, lh = [], []
                for hh in range(2):
                    bias = bias_ref[br, 2 * pp + hh]
                    if not has_prev:
                        row = jnp.concatenate([bias[:, blk:]] * grp, axis=1)
                        bias = jnp.concatenate([row] * grp, axis=0)
                    s = jnp.where(valid, _mm_nt(qp * half[hh], kp) + bias, NEG)
                    m = jnp.max(s, axis=-1, keepdims=True)
                    p = jnp.exp(s - m)
                    l = jnp.sum(p, axis=-1, keepdims=True)
                    oh.append(_mm(p, vp) / l)
                    lh.append(m + jnp.log(l))
                o_all = jnp.where(low, oh[0], oh[1])
                lse_all = jnp.where(low, lh[0], lh[1])
                for g, rq in enumerate(rqs):
                    o = o_all[g * blk:(g + 1) * blk]
                    lse = lse_all[g * blk:(g + 1) * blk]
                    if first:
                        num_ref[pp, rq, :] = o
                        den_ref[pp, rq, :] = jnp.ones_like(o)
                        mx_ref[pp, rq, :] = lse
                    else:
                        m_old = mx_ref[pp, rq, :]
                        m_new = jnp.maximum(m_old, lse)
                        wa = jnp.exp(m_old - m_new)
                        wb = jnp.exp(lse - m_new)
                        num_ref[pp, rq, :] = num_ref[pp, rq, :] * wa + wb * o
                        den_ref[pp, rq, :] = den_ref[pp, rq, :] * wa + wb
                        mx_ref[pp, rq, :] = m_new
            return carry

        lax.fori_loop(0, dil * nb // grp, block, 0, unroll=2 if has_prev else 1)

    def finish(i, carry):
        rows = pl.ds(pl.multiple_of(i * blk, blk), blk)
        for pp in range(npair):
            o_ref[0, rows, pp * pw:(pp + 1) * pw] = num_ref[pp, rows, :] / den_ref[pp, rows, :]
        return carry

    lax.fori_loop(0, S // blk, finish, 0)


def _attn(ua3, qw, kw, bias_all):
    B, S, _ = ua3.shape
    nbr = len(DILATED_PAIRS)
    return pl.pallas_call(
        _attn_kernel,
        grid=(B,),
        in_specs=[pl.BlockSpec((1, S, AT_COLS), lambda b: (b, 0, 0)),
                  pl.BlockSpec((1, ATTN_W), lambda b: (0, 0)),
                  pl.BlockSpec((1, ATTN_W), lambda b: (0, 0)),
                  pl.BlockSpec((nbr, ATTN_HEADS, ATTN_BLOCK, 2 * ATTN_BLOCK), lambda b: (0, 0, 0, 0))],
        out_specs=pl.BlockSpec((1, S, ATTN_W), lambda b: (b, 0, 0)),
        out_shape=jax.ShapeDtypeStruct((B, S, ATTN_W), F32),
        scratch_shapes=[pltpu.VMEM((ATTN_HEADS // 2, S, 2 * HEAD_DIM), F32) for _ in range(6)],
        compiler_params=_cparams(("parallel",)),
        name="dilated_attn",
    )(ua3, qw, kw, bias_all)


def _outproj_kernel(x_ref, mh_ref, ma_ref, ms_ref, wo_ref, fw_ref, wq_ref, sk_ref,
                    h_ref, xn_ref, sc_ref):
    W = ATTN_W
    h = (x_ref[...] + _mm(mh_ref[...], wo_ref[0:HGRN_W, :])
         + _mm(ma_ref[...], wo_ref[HGRN_W:HGRN_W + W, :])
         + _mm(ms_ref[...], wo_ref[HGRN_W + W:, :]))
    h_ref[...] = h
    xn = (h * lax.rsqrt(jnp.mean(h * h, axis=-1, keepdims=True) + EPS) * fw_ref[...]).astype(BF16)
    xn_ref[...] = xn
    qp = jnp.dot(xn, wq_ref[...], preferred_element_type=F32)
    for hd in range(PEER_HEADS):
        qh = qp[:, 2 * PEER_HALF * hd:2 * PEER_HALF * (hd + 1)]
        for c in range(2):
            sc_ref[hd, c] = _mm_nt(sk_ref[c], qh)


def _outproj(x2, mh, ma, ms, w_out, ffn_w, wq, sk_pad, tm=512):
    T, D = x2.shape
    row = lambda n: pl.BlockSpec((tm, n), lambda i: (i, 0))
    full = lambda a: pl.BlockSpec(a.shape, lambda i: (0,) * a.ndim)
    return pl.pallas_call(
        _outproj_kernel,
        grid=(T // tm,),
        in_specs=[row(D), row(HGRN_W), row(ATTN_W), row(SSM_W),
                  full(w_out), pl.BlockSpec((1, D), lambda i: (0, 0)), full(wq), full(sk_pad)],
        out_specs=[row(D), row(D),
                   pl.BlockSpec((PEER_HEADS, 2, PEER_NKEYS, tm), lambda i: (0, 0, 0, i))],
        out_shape=[jax.ShapeDtypeStruct((T, D), F32), jax.ShapeDtypeStruct((T, D), BF16),
                   jax.ShapeDtypeStruct((PEER_HEADS, 2, PEER_NKEYS, T), F32)],
        compiler_params=_cparams(("parallel",)),
        name="outproj_query",
    )(x2, mh, ma, ms, w_out, ffn_w.reshape(1, D), wq, sk_pad)


_NSEL = PEER_TOPK + 1
_PAIRS = [(i, j) for i in range(_NSEL) for j in range(_NSEL) if (i + 1) * (j + 1) <= _NSEL]


def _top_values(v, k):
    out = []
    rank = jnp.full(v.shape, float(k), F32)
    for r in range(k):
        m = jnp.max(v, axis=0, keepdims=True)
        out.append(m)
        hit = v == m
        rank = jnp.where(hit, float(r), rank)
        v = jnp.where(hit, NEG, v)
    return out, rank


def _pack_pair(lo, hi):
    lo_b = lax.bitcast_convert_type(lo.astype(BF16).astype(F32), jnp.uint32) >> 16
    hi_b = lax.bitcast_convert_type(hi.astype(BF16).astype(F32), jnp.uint32) & jnp.uint32(0xFFFF0000)
    return hi_b | lo_b


def _route_kernel(sc_ref, n1_ref, u1_ref, rk2_ref, u2_ref):
    tt = sc_ref.shape[-1]
    row8 = _iota((8, tt), 0)

    def head(hd):
        s1 = sc_ref[hd, 0]
        s2 = sc_ref[hd, 1]
        a, _ = _top_values(s1, _NSEL)
        b, rank2 = _top_values(s2, _NSEL)
        groups = []
        for g0 in range(0, len(_PAIRS), 8):
            cg = jnp.full((8, tt), NEG, F32)
            for p, (i, j) in enumerate(_PAIRS[g0:g0 + 8]):
                cg = jnp.where(row8 == p, a[i] + b[j], cg)
            groups.append(cg)
        best, _ = _top_values(jnp.concatenate(groups, axis=0), _NSEL)
        zsum = jnp.ones((1, tt), F32)
        for r in range(1, PEER_TOPK):
            zsum = zsum + jnp.exp(best[r] - best[0])
        cut = 0.5 * (best[PEER_TOPK - 1] + best[PEER_TOPK])
        need = cut - s1
        n1 = jnp.zeros_like(s1)
        for r in range(PEER_TOPK):
            n1 = jnp.where(b[r] >= need, float(r + 1), n1)
        return n1, jnp.exp(s1 - a[0]) / zsum, rank2, jnp.exp(s2 - b[0])

    def pair(hp, carry):
        lo = head(2 * hp)
        hi = head(2 * hp + 1)
        for ref, x, y in zip((n1_ref, u1_ref, rk2_ref, u2_ref), lo, hi):
            ref[hp] = _pack_pair(x, y)
        return carry

    lax.fori_loop(0, PEER_HEADS // 2, pair, 0)


def _route(sc, tt=256):
    T = sc.shape[-1]
    blk = pl.BlockSpec((PEER_HEADS // 2, PEER_NKEYS, tt), lambda i: (0, 0, i))
    shp = jax.ShapeDtypeStruct((PEER_HEADS // 2, PEER_NKEYS, T), jnp.uint32)
    return pl.pallas_call(
        _route_kernel,
        grid=(T // tt,),
        in_specs=[pl.BlockSpec((PEER_HEADS, 2, PEER_NKEYS, tt), lambda i: (0, 0, 0, i))],
        out_specs=[blk, blk, blk, blk],
        out_shape=[shp, shp, shp, shp],
        compiler_params=_cparams(("parallel",)),
        name="peer_route",
    )(sc)


_GELU_C1 = -2.0 * math.sqrt(2.0 / math.pi)
_GELU_C2 = _GELU_C1 * 0.044715
_PEER_EB = 512
_PEER_LANES = 128
_PEER_ROWS = 32
_PEER_FLAGS = None


def _as_halves(words):
    return pltpu.bitcast(words, BF16)


def _peer_kernel(nj, xn_ref, dn_ref, upt_ref, n1_ref, u1_ref, rk2_ref, u2_ref, h_ref, o_ref,
                 acc_ref, at0_ref, at1_ref, wg0_ref, wg1_ref):
    j = pl.program_id(1)
    tt = xn_ref.shape[0]
    te = dn_ref.shape[0]
    nb = te // _PEER_EB
    nh = tt // _PEER_TOK
    gpb = _PEER_EB // PEER_NKEYS
    at = (at0_ref, at1_ref)
    wg = (wg0_ref, wg1_ref)

    @pl.when(j == 0)
    def _():
        acc_ref[...] = jnp.zeros_like(acc_ref)

    def scores(k, at_ref):
        rows = pl.ds(pl.multiple_of(k * _PEER_EB, _PEER_EB), _PEER_EB)
        for hf in range(nh):
            at_ref[hf] = lax.dot_general(dn_ref[rows, :], xn_ref[hf * _PEER_TOK:(hf + 1) * _PEER_TOK, :],
                                         (((1,), (1,)), ((), ())), preferred_element_type=F32)

    def weigh(k, at_ref, wg_ref):
        for g in range(gpb):
            c = (j * nb + k) * gpb + g
            er = slice(g * PEER_NKEYS, (g + 1) * PEER_NKEYS)
            for hf in range(nh):
                tw = slice(hf * _PEER_TOK, (hf + 1) * _PEER_TOK)
                n1s = [n1_ref[hp, pl.ds(c, 1), tw] for hp in range(PEER_HEADS // 2)]
                u1s = [u1_ref[hp, pl.ds(c, 1), tw] for hp in range(PEER_HEADS // 2)]
                for lt in range(_PEER_TOK // _PEER_LANES):
                    ls = slice(lt * _PEER_LANES, (lt + 1) * _PEER_LANES)
                    tl = slice(hf * _PEER_TOK + lt * _PEER_LANES, hf * _PEER_TOK + (lt + 1) * _PEER_LANES)
                    w = None
                    for hp in range(PEER_HEADS // 2):
                        n1 = _as_halves(jnp.broadcast_to(n1s[hp][:, ls], (PEER_NKEYS, _PEER_LANES)))
                        u1 = _as_halves(jnp.broadcast_to(u1s[hp][:, ls], (PEER_NKEYS, _PEER_LANES)))
                        rk2 = _as_halves(rk2_ref[hp, :, tl])
                        u2 = _as_halves(u2_ref[hp, :, tl])
                        t = jnp.where(rk2 < n1, u2, jnp.zeros((), BF16)) * u1
                        w = t if w is None else w + t
                    wu = pltpu.bitcast(w, jnp.uint32)
                    wf = (lax.bitcast_convert_type(wu << 16, F32)
                          + lax.bitcast_convert_type(wu & jnp.uint32(0xFFFF0000), F32))
                    a = at_ref[hf, er, ls]
                    gl = a / (1.0 + jnp.exp(a * (_GELU_C1 + _GELU_C2 * (a * a))))
                    wg_ref[hf, er, ls] = (wf * gl).astype(BF16)

    def project(k, wg_ref):
        for hf in range(nh):
            tw = slice(hf * _PEER_TOK, (hf + 1) * _PEER_TOK)
            acc_ref[:, tw] += jnp.dot(upt_ref[k], wg_ref[hf], preferred_element_type=F32)

    scores(0, at[0])
    scores(1, at[1])
    weigh(0, at[0], wg[0])

    def body(m, carry):
        k = 2 * m + 1
        scores(k + 1, at[0])
        weigh(k, at[1], wg[1])
        project(k - 1, wg[0])
        scores(k + 2, at[1])
        weigh(k + 1, at[0], wg[0])
        project(k, wg[1])
        return carry

    lax.fori_loop(0, nb // 2 - 1, body, 0)
    weigh(nb - 1, at[1], wg[1])
    project(nb - 2, wg[0])
    project(nb - 1, wg[1])

    @pl.when(j == nj - 1)
    def _():
        o_ref[...] = h_ref[...] + acc_ref[...].T


def _peer(xn, down_b, layer, up_t3, n1, u1, rk2, u2, h, tt=512, te=2048):
    T, D = xn.shape
    E = down_b.shape[1]
    nj = E // te
    nh = tt // _PEER_TOK
    rt = pl.BlockSpec((PEER_HEADS // 2, PEER_NKEYS, tt), lambda i, j: (0, 0, i))
    return pl.pallas_call(
        functools.partial(_peer_kernel, nj),
        grid=(T // tt, nj),
        in_specs=[pl.BlockSpec((tt, D), lambda i, j: (i, 0)),
                  pl.BlockSpec((None, te, D), lambda i, j: (layer, j, 0)),
                  pl.BlockSpec((te // _PEER_EB, D, _PEER_EB), lambda i, j: (j, 0, 0)),
                  rt, rt, rt, rt,
                  pl.BlockSpec((tt, D), lambda i, j: (i, 0))],
        out_specs=pl.BlockSpec((tt, D), lambda i, j: (i, 0)),
        out_shape=jax.ShapeDtypeStruct((T, D), F32),
        scratch_shapes=[pltpu.VMEM((D, tt), F32),
                        pltpu.VMEM((nh, _PEER_EB, _PEER_TOK), F32), pltpu.VMEM((nh, _PEER_EB, _PEER_TOK), F32),
                        pltpu.VMEM((nh, _PEER_EB, _PEER_TOK), BF16), pltpu.VMEM((nh, _PEER_EB, _PEER_TOK), BF16)],
        compiler_params=_cparams(("parallel", "arbitrary"), flags=_PEER_FLAGS),
        name="peer_experts",
    )(xn, down_b, up_t3, n1, u1, rk2, u2, h)


def _transpose_cast_kernel(x_ref, o_ref):
    o_ref[0] = x_ref[...].T.astype(o_ref.dtype)


def _transpose_blocks_bf16(x, layer, tr=_PEER_EB):
    _, R, C = x.shape
    return pl.pallas_call(
        _transpose_cast_kernel,
        grid=(R // tr,),
        in_specs=[pl.BlockSpec((None, tr, C), lambda i: (layer, i, 0))],
        out_specs=pl.BlockSpec((1, C, tr), lambda i: (i, 0, 0)),
        out_shape=jax.ShapeDtypeStruct((R // tr, C, tr), BF16),
        compiler_params=_cparams(("parallel",)),
        name="transpose_cast",
    )(x)


def _rep(v, n):
    return jnp.repeat(v, n, axis=-1)


def _layer(x2, B, S, layer, p, bias_all):
    T, D = x2.shape
    uh, ua, us = _inproj(x2, p["attn_norm_w"], p["w_in_pad"], layer)

    mh = _hgrn(uh.reshape(B, S, HG_COLS), p["hgrn_lower_bounds"],
               jnp.tile(p["hgrn_norm_w"], HGRN_HEADS).reshape(1, HGRN_W), layer)
    qw = jnp.tile(p["q_norm_w"], ATTN_HEADS).reshape(1, ATTN_W)
    kw = jnp.tile(p["k_norm_w"], ATTN_HEADS).reshape(1, ATTN_W)
    ma = _attn(ua.reshape(B, S, AT_COLS), qw, kw, bias_all)
    ms = _ssd(us.reshape(B, S, SS_COLS), p["ssm_conv_w"], p["ssm_conv_b"].reshape(1, -1),
              _rep(p["ssm_dt_bias"], HEAD_DIM).reshape(1, -1), _rep(p["ssm_a_log"], HEAD_DIM).reshape(1, -1),
              _rep(p["ssm_d"], HEAD_DIM).reshape(1, -1), p["ssm_norm_w"].reshape(1, -1))

    sk = p["peer_sub_keys"]
    zk = jnp.zeros_like(sk[0])
    sk_pad = jnp.stack([jnp.concatenate([sk[0], zk], axis=1), jnp.concatenate([zk, sk[1]], axis=1)])
    h, xn, sc = _outproj(x2, mh.reshape(T, HGRN_W), ma.reshape(T, ATTN_W), ms.reshape(T, SSM_W),
                         p["w_out"].astype(BF16), p["ffn_norm_w"], p["peer_w_query"].astype(BF16), sk_pad)
    n1, u1, rk2, u2 = _route(sc)
    return _peer(xn, p["peer_down_b"], layer, _transpose_blocks_bf16(p["peer_up"], layer), n1, u1, rk2, u2, h)


_PER_LAYER = ("attn_norm_w", "hgrn_norm_w", "q_norm_w", "k_norm_w", "ssm_conv_w", "ssm_conv_b",
              "ssm_dt_bias", "ssm_a_log", "ssm_d", "ssm_norm_w", "w_out", "ffn_norm_w", "peer_w_query",
              "peer_sub_keys")


def kernel(x, attn_norm_w, w_in, hgrn_lower_bounds, hgrn_norm_w, q_norm_w, k_norm_w, rel_bias, ssm_conv_w, ssm_conv_b, ssm_dt_bias, ssm_a_log, ssm_d, ssm_norm_w, w_out, ffn_norm_w, peer_w_query, peer_sub_keys, peer_down, peer_up):
    stacked = dict(attn_norm_w=attn_norm_w, w_in=w_in, hgrn_norm_w=hgrn_norm_w, q_norm_w=q_norm_w,
                   k_norm_w=k_norm_w, ssm_conv_w=ssm_conv_w, ssm_conv_b=ssm_conv_b, ssm_dt_bias=ssm_dt_bias,
                   ssm_a_log=ssm_a_log, ssm_d=ssm_d, ssm_norm_w=ssm_norm_w, w_out=w_out, ffn_norm_w=ffn_norm_w,
                   peer_w_query=peer_w_query, peer_sub_keys=peer_sub_keys, peer_down=peer_down, peer_up=peer_up)
    B, S, D = x.shape
    bias_all = _bias_tables(rel_bias)
    down_b = peer_down.astype(BF16)
    lane_pad = -w_in.shape[2] % 128
    w_in_pad = jnp.pad(w_in, ((0, 0), (0, 0), (0, lane_pad))).astype(BF16)
    x2 = x.reshape(B * S, D)
    for layer in range(w_in.shape[0]):
        p = {k: stacked[k][layer] for k in _PER_LAYER}
        p["w_in_pad"] = w_in_pad
        p["hgrn_lower_bounds"] = hgrn_lower_bounds
        p["peer_down_b"] = down_b
        p["peer_up"] = peer_up
        x2 = _layer(x2, B, S, layer, p, bias_all)
    return x2.reshape(B, S, D)
```

```python
import functools
import math

import numpy as np
import jax
import jax.numpy as jnp
from jax import lax
from jax.experimental import pallas as pl
from jax.experimental.pallas import tpu as pltpu

F32 = jnp.float32
BF16 = jnp.bfloat16
HIGHEST = lax.Precision.HIGHEST
NEG = -1e30
EPS = 1e-6

HEAD_DIM = 64
CHUNK = 64
HGRN_HEADS = 4
HGRN_W = HGRN_HEADS * HEAD_DIM
ATTN_HEADS = 6
ATTN_W = ATTN_HEADS * HEAD_DIM
ATTN_BLOCK = 128
DILATED_PAIRS = ((128, 1), (512, 4), (2048, 16))
NUM_BUCKETS = 32
MAX_DISTANCE = 2048
SSM_HEADS = 6
SSM_W = SSM_HEADS * HEAD_DIM
SSM_GROUPS = 2
SSM_STATE = 128
SSM_GN = SSM_GROUPS * SSM_STATE
SSM_CONV = 4
SSM_CONV_CH = SSM_W + 2 * SSM_GN
PEER_HEADS = 8
PEER_NKEYS = 128
PEER_HALF = 64
PEER_TOPK = 16
_PEER_TOK = 256
HG_COLS = 4 * HGRN_W
AT_COLS = 3 * ATTN_W
SS_COLS = SSM_W + SSM_CONV_CH + SSM_W

VMEM_LIMIT = 56 * 1024 * 1024


def _cparams(sem, flags=None):
    return pltpu.CompilerParams(dimension_semantics=sem, vmem_limit_bytes=VMEM_LIMIT, flags=flags)


def _iota(shape, dim):
    return lax.broadcasted_iota(jnp.int32, shape, dim)


def _mm(a, b):
    return jnp.dot(a.astype(BF16), b.astype(BF16), preferred_element_type=F32)


def _mm_nt(a, b):
    return lax.dot_general(a.astype(BF16), b.astype(BF16), (((1,), (1,)), ((), ())),
                           preferred_element_type=F32)


def _mm_tn(a, b):
    return lax.dot_general(a.astype(BF16), b.astype(BF16), (((0,), (0,)), ((), ())),
                           preferred_element_type=F32)


def _mm_hi(a, b):
    return jnp.dot(a, b, precision=HIGHEST, preferred_element_type=F32)


def _sigmoid(x):
    return 1.0 / (1.0 + jnp.exp(-x))


def _silu(x):
    return x * _sigmoid(x)


def _softplus(x):
    return jnp.maximum(x, 0.0) + jnp.log(1.0 + jnp.exp(-jnp.abs(x)))


def _seg_ones(n, seg):
    r = _iota((n, n), 0)
    c = _iota((n, n), 1)
    same = None
    for s in range(n // seg):
        lo, hi = s * seg, (s + 1) * seg
        t = (r >= lo) & (r < hi) & (c >= lo) & (c < hi)
        same = t if same is None else (same | t)
    return jnp.where(same, 1.0, 0.0).astype(F32)


def _lane_seg_mask(width, seg, idx):
    l = _iota((1, width), 1)
    return jnp.where((l >= idx * seg) & (l < (idx + 1) * seg), 1.0, 0.0).astype(F32)


def _inproj_kernel(x_ref, nw_ref, w_ref, oh_ref, oa_ref, os_ref):
    x = x_ref[...]
    xn = x * lax.rsqrt(jnp.mean(x * x, axis=-1, keepdims=True) + EPS) * nw_ref[...]
    xb = xn.astype(BF16)
    c0, c1, c2 = HG_COLS, HG_COLS + AT_COLS, HG_COLS + AT_COLS + SSM_W + SSM_CONV_CH
    oh_ref[...] = jnp.dot(xb, w_ref[:, 0:c0], preferred_element_type=F32)
    oa_ref[...] = jnp.dot(xb, w_ref[:, c0:c1], preferred_element_type=F32)
    os_ref[:, 0:c2 - c1] = jnp.dot(xb, w_ref[:, c1:c2], preferred_element_type=F32)
    dt = jnp.dot(xb, w_ref[:, c2:], preferred_element_type=F32)
    lanes = w_ref.shape[1] - c2
    expand = jnp.where((_iota((lanes, SSM_W), 1) >> 6) == _iota((lanes, SSM_W), 0), 1.0, 0.0).astype(F32)
    os_ref[:, c2 - c1:] = _mm_hi(dt, expand)


def _inproj(x2, norm_w, w_pad, layer, tm=512):
    T, D = x2.shape
    ncol = w_pad.shape[2]
    return pl.pallas_call(
        _inproj_kernel,
        grid=(T // tm,),
        in_specs=[pl.BlockSpec((tm, D), lambda i: (i, 0)),
                  pl.BlockSpec((1, D), lambda i: (0, 0)),
                  pl.BlockSpec((None, D, ncol), lambda i: (layer, 0, 0))],
        out_specs=[pl.BlockSpec((tm, HG_COLS), lambda i: (i, 0)),
                   pl.BlockSpec((tm, AT_COLS), lambda i: (i, 0)),
                   pl.BlockSpec((tm, SS_COLS), lambda i: (i, 0))],
        out_shape=[jax.ShapeDtypeStruct((T, HG_COLS), F32),
                   jax.ShapeDtypeStruct((T, AT_COLS), F32),
                   jax.ShapeDtypeStruct((T, SS_COLS), F32)],
        compiler_params=_cparams(("parallel",)),
        name="inproj",
    )(x2, norm_w.reshape(1, D), w_pad)


def _hgrn_kernel(layer, nchunk, u_ref, lbs_ref, nw_ref, o_ref, st_ref):
    W = HGRN_W

    @pl.when(pl.program_id(1) == 0)
    def _():
        st_ref[...] = jnp.zeros_like(st_ref)

    lbs = lbs_ref[...]
    e = jnp.exp(lbs - jnp.max(lbs, axis=0, keepdims=True))
    sm = e / jnp.sum(e, axis=0, keepdims=True)
    lb = jnp.zeros((1, W), F32)
    for j in range(1, layer + 1):
        lb = lb + sm[j:j + 1]
    nw = nw_ref[...]

    tril = jnp.where(_iota((CHUNK, CHUNK), 0) >= _iota((CHUNK, CHUNK), 1), 1.0, 0.0).astype(F32)
    row = _iota((CHUNK, W), 0)
    hm = [_lane_seg_mask(W, HEAD_DIM, h) for h in range(HGRN_HEADS)]
    hm3 = [jnp.concatenate([m, m, m], axis=1) for m in hm]
    seg = _seg_ones(W, HEAD_DIM)
    seg_b = seg.astype(BF16)
    tt16 = _iota((16, W), 0)
    z32 = jnp.zeros((32, W), F32)
    in_q2 = (row & 16) != 0

    def chunk(c, carry):
        r0 = pl.multiple_of(c * CHUNK, CHUNK)
        u = u_ref[0, pl.ds(r0, CHUNK), :]
        q = u[:, 0:W] * (HEAD_DIM ** -0.5)
        f = u[:, W:2 * W]
        iv = u[:, 2 * W:3 * W]
        g = u[:, 3 * W:4 * W]
        forget = lb + (1.0 - lb) * _sigmoid(f)
        logf = jnp.log(forget)
        kk = 1.0 - forget
        b = _mm_hi(tril, logf)
        b16, b32, b48, bend = b[16:17], b[32:33], b[48:49], b[63:64]

        q1 = q * jnp.exp(jnp.where(row >= 32, b - b32, NEG))
        k1 = kk * jnp.exp(jnp.where(row < 32, b32 - b, NEG))
        ref2 = jnp.where(row < 32, b16, b48)
        q2 = q * jnp.exp(jnp.where(in_q2, b - ref2, NEG))
        k2 = kk * jnp.exp(jnp.where(in_q2, NEG, ref2 - b))
        qcat = jnp.concatenate([q1, jnp.concatenate([q2[:32], z32], 0), jnp.concatenate([z32, q2[32:]], 0)], 1)
        kcat = jnp.concatenate([k1, jnp.concatenate([k2[:32], z32], 0), jnp.concatenate([z32, k2[32:]], 0)], 1)
        qstack = jnp.concatenate([qcat * hm3[h] for h in range(HGRN_HEADS)], axis=0)
        a = _mm_nt(qstack, kcat)
        ofull = _mm(a, iv)
        o = ofull[0:CHUNK] * hm[0]
        for h in range(1, HGRN_HEADS):
            o = o + ofull[h * CHUNK:(h + 1) * CHUNK] * hm[h]

        st = st_ref[...]
        o = o + _mm_nt(q * jnp.exp(b), st)

        diag = []
        for j in range(4):
            bR = b[16 * j:16 * j + 16]
            qR = q[16 * j:16 * j + 16]
            ps = []
            for s in range(16):
                r = 16 * j + s
                ps.append(qR * kk[r:r + 1] * jnp.exp(jnp.where(tt16 >= s, bR - b[r:r + 1], NEG)))
            ab = jnp.dot(jnp.concatenate(ps, axis=0).astype(BF16), seg_b, preferred_element_type=F32)
            od = ab[0:16] * iv[16 * j:16 * j + 1]
            for s in range(1, 16):
                od = od + ab[16 * s:16 * s + 16] * iv[16 * j + s:16 * j + s + 1]
            diag.append(od)
        o = o + jnp.concatenate(diag, axis=0)

        kend = kk * jnp.exp(bend - b)
        st_ref[...] = st * jnp.exp(bend) + _mm_tn(iv, kend) * seg

        ss = _mm(o * o, seg_b) * (1.0 / HEAD_DIM)
        o_ref[0, pl.ds(r0, CHUNK), :] = o * lax.rsqrt(ss + EPS) * nw * _silu(g)
        return carry

    lax.fori_loop(0, nchunk, chunk, 0, unroll=True)


def _hgrn(uh, lbs, norm_w, layer, blk=512):
    B, S, _ = uh.shape
    return pl.pallas_call(
        functools.partial(_hgrn_kernel, layer, blk // CHUNK),
        grid=(B, S // blk),
        in_specs=[pl.BlockSpec((1, blk, HG_COLS), lambda b, s: (b, s, 0)),
                  pl.BlockSpec(lbs.shape, lambda b, s: (0, 0)),
                  pl.BlockSpec((1, HGRN_W), lambda b, s: (0, 0))],
        out_specs=pl.BlockSpec((1, blk, HGRN_W), lambda b, s: (b, s, 0)),
        out_shape=jax.ShapeDtypeStruct((B, S, HGRN_W), F32),
        scratch_shapes=[pltpu.VMEM((HGRN_W, HGRN_W), F32)],
        compiler_params=_cparams(("parallel", "arbitrary")),
        name="hgrn2",
    )(uh, lbs, norm_w)


def _ssd_kernel(nchunk, u_ref, cw_ref, cb_ref, dtb_ref, alog_ref, d_ref, nw_ref, o_ref,
                st_ref, ext_ref, xc_ref):
    W = SSM_W
    blk = nchunk * CHUNK
    first = pl.program_id(1) == 0

    @pl.when(first)
    def _():
        st_ref[...] = jnp.zeros_like(st_ref)
        ext_ref[0:8, :] = jnp.zeros((8, SSM_CONV_CH), F32)

    @pl.when(jnp.logical_not(first))
    def _():
        ext_ref[0:8, :] = ext_ref[blk:blk + 8, :]

    ext_ref[8:8 + blk, :] = u_ref[0, :, W:W + SSM_CONV_CH]
    cw = cw_ref[...]
    conv = cb_ref[...] + cw[0:1] * ext_ref[5:5 + blk, :]
    for j in range(1, SSM_CONV):
        conv = conv + cw[j:j + 1] * ext_ref[5 + j:5 + j + blk, :]
    xc_ref[...] = _silu(conv)

    dtb = dtb_ref[...]
    a_neg = -jnp.exp(alog_ref[...])
    dsk = d_ref[...]
    nw = nw_ref[...]
    tril = jnp.where(_iota((CHUNK, CHUNK), 0) >= _iota((CHUNK, CHUNK), 1), 1.0, 0.0).astype(F32)
    t_i = _iota((CHUNK, W), 0)
    s_i = _iota((CHUNK, W), 1) & (HEAD_DIM - 1)
    strict = jnp.where(t_i > s_i, 1.0, 0.0).astype(F32)
    lower = t_i >= s_i
    hm = [_lane_seg_mask(W, HEAD_DIM, h) for h in range(SSM_HEADS)]
    zB = jnp.zeros((CHUNK, SSM_STATE), F32)
    gr = _iota((SSM_GN, W), 0) >= SSM_STATE
    gc = _iota((SSM_GN, W), 1) >= (SSM_HEADS // SSM_GROUPS) * HEAD_DIM
    gmask = jnp.where(gr == gc, 1.0, 0.0).astype(F32)
    gw = W // SSM_GROUPS
    seg = _seg_ones(W, gw).astype(BF16)

    def chunk(c, carry):
        r0 = pl.multiple_of(c * CHUNK, CHUNK)
        z = u_ref[0, pl.ds(r0, CHUNK), 0:W]
        dtr = u_ref[0, pl.ds(r0, CHUNK), W + SSM_CONV_CH:W + SSM_CONV_CH + W]
        xc = xc_ref[pl.ds(r0, CHUNK), :]
        xs = xc[:, 0:W]
        bm = xc[:, W:W + SSM_GN]
        cm = xc[:, W + SSM_GN:W + 2 * SSM_GN]
        dt = _softplus(dtr + dtb)
        a = dt * a_neg
        dd = _mm_hi(tril, jnp.concatenate([a * strict, a], axis=1))
        dseg = dd[:, 0:W]
        cum = dd[:, W:2 * W]
        decay = jnp.exp(jnp.where(lower, dseg, NEG))
        b0 = jnp.concatenate([bm[:, 0:SSM_STATE], zB], axis=1)
        b1 = jnp.concatenate([zB, bm[:, SSM_STATE:]], axis=1)
        bstack = jnp.concatenate([b0, b0, b0, b1, b1, b1], axis=0)
        scores = _mm_nt(cm, bstack) * decay
        xdt = xs * dt
        xbd = jnp.concatenate([xdt * hm[h] for h in range(SSM_HEADS)], axis=0)
        y = _mm(scores, xbd)
        st = st_ref[...]
        y = y + _mm(cm, st) * jnp.exp(cum)
        cend = cum[CHUNK - 1:CHUNK]
        to_end = jnp.exp(cend - cum) * dt
        st_ref[...] = st * jnp.exp(cend) + _mm_tn(bm, to_end * xs) * gmask
        y = y + dsk * xs
        y = y * _silu(z)
        ss = _mm(y * y, seg) * (1.0 / gw)
        o_ref[0, pl.ds(r0, CHUNK), :] = y * lax.rsqrt(ss + EPS) * nw
        return carry

    lax.fori_loop(0, nchunk, chunk, 0, unroll=True)


def _ssd(us, conv_w, conv_b, dtb, alog, dsk, norm_w, blk=512):
    B, S, _ = us.shape
    vec = lambda n: pl.BlockSpec((1, n), lambda b, s: (0, 0))
    return pl.pallas_call(
        functools.partial(_ssd_kernel, blk // CHUNK),
        grid=(B, S // blk),
        in_specs=[pl.BlockSpec((1, blk, SS_COLS), lambda b, s: (b, s, 0)),
                  pl.BlockSpec((SSM_CONV, SSM_CONV_CH), lambda b, s: (0, 0)),
                  vec(SSM_CONV_CH), vec(SSM_W), vec(SSM_W), vec(SSM_W), vec(SSM_W)],
        out_specs=pl.BlockSpec((1, blk, SSM_W), lambda b, s: (b, s, 0)),
        out_shape=jax.ShapeDtypeStruct((B, S, SSM_W), F32),
        scratch_shapes=[pltpu.VMEM((SSM_GN, SSM_W), F32),
                        pltpu.VMEM((blk + 8, SSM_CONV_CH), F32),
                        pltpu.VMEM((blk, SSM_CONV_CH), F32)],
        compiler_params=_cparams(("parallel", "arbitrary")),
        name="ssd",
    )(us, conv_w, conv_b, dtb, alog, dsk, norm_w)


def _t5_bucket(dist):
    max_exact = NUM_BUCKETS // 2
    d = np.maximum(dist, 1).astype(np.float32)
    large = max_exact + (np.log(d / max_exact) / np.log(MAX_DISTANCE / max_exact)
                         * (NUM_BUCKETS - max_exact)).astype(np.int32)
    large = np.minimum(large, NUM_BUCKETS - 1)
    return np.where(dist < max_exact, dist, large).astype(np.int32)


def _bucket_tables():
    qi = np.arange(ATTN_BLOCK)[:, None]
    kj = np.arange(2 * ATTN_BLOCK)[None, :]
    delta = qi - kj + ATTN_BLOCK
    return np.stack([_t5_bucket(np.maximum(delta, 0) * dil) for _, dil in DILATED_PAIRS])


def _bias_kernel(rb_ref, bk_ref, o_ref):
    bk = bk_ref[0]
    for h in range(ATTN_HEADS):
        acc = jnp.zeros(bk.shape, F32)
        for b in range(NUM_BUCKETS):
            acc = jnp.where(bk == b, rb_ref[b, h], acc)
        o_ref[0, h] = acc


def _bias_tables(rel_bias):
    bk = jnp.asarray(_bucket_tables())
    nbr = len(DILATED_PAIRS)
    return pl.pallas_call(
        _bias_kernel,
        grid=(nbr,),
        in_specs=[pl.BlockSpec(memory_space=pltpu.SMEM),
                  pl.BlockSpec((1, ATTN_BLOCK, 2 * ATTN_BLOCK), lambda i: (i, 0, 0))],
        out_specs=pl.BlockSpec((1, ATTN_HEADS, ATTN_BLOCK, 2 * ATTN_BLOCK), lambda i: (i, 0, 0, 0)),
        out_shape=jax.ShapeDtypeStruct((nbr, ATTN_HEADS, ATTN_BLOCK, 2 * ATTN_BLOCK), F32),
        compiler_params=_cparams(("parallel",)),
        name="attn_bias",
    )(rel_bias, bk)


_ATTN_GROUP = 2


def _attn_kernel(ua_ref, qw_ref, kw_ref, bias_ref, o_ref, qn_ref, kn_ref, vv_ref, num_ref, den_ref, mx_ref):
    W = ATTN_W
    S = ua_ref.shape[1]
    blk = ATTN_BLOCK
    seg = _seg_ones(W, HEAD_DIM).astype(BF16)
    qw = qw_ref[...]
    kw = kw_ref[...]

    def nrm(x, w):
        ss = _mm(x * x, seg) * (1.0 / HEAD_DIM)
        return x * lax.rsqrt(ss + EPS) * w

    npair = ATTN_HEADS // 2
    pw = 2 * HEAD_DIM

    def norm_block(i, carry):
        rows = pl.ds(pl.multiple_of(i * blk, blk), blk)
        qn = nrm(ua_ref[0, rows, 0:W], qw) * (HEAD_DIM ** -0.5)
        kn = nrm(ua_ref[0, rows, W:2 * W], kw)
        for pp in range(npair):
            qn_ref[pp, rows, :] = qn[:, pp * pw:(pp + 1) * pw]
            kn_ref[pp, rows, :] = kn[:, pp * pw:(pp + 1) * pw]
            vv_ref[pp, rows, :] = ua_ref[0, rows, 2 * W + pp * pw:2 * W + (pp + 1) * pw]
        return carry

    lax.fori_loop(0, S // blk, norm_block, 0)

    lane = _iota((1, pw), 1)
    low = lane < HEAD_DIM
    half = [jnp.where(low, 1.0, 0.0).astype(F32), jnp.where(low, 0.0, 1.0).astype(F32)]

    order = sorted(range(len(DILATED_PAIRS)), key=lambda i: -DILATED_PAIRS[i][1])
    for br in order:
        window, dil = DILATED_PAIRS[br]
        first = br == order[0]
        n_back = window // dil
        nb = S // dil // blk
        has_prev = nb > 1
        grp = 1 if has_prev else _ATTN_GROUP
        nq = grp * blk
        nk = 2 * blk if has_prev else nq
        qi = _iota((nq, nk), 0)
        kj = _iota((nq, nk), 1) + (0 if has_prev else blk)
        if has_prev:
            delta = qi - kj + blk
            band = (delta >= 0) & (delta <= n_back)
        else:
            assert n_back >= blk - 1
            kl = kj - blk
            band = ((qi >> 7) == (kl >> 7)) & ((qi & (blk - 1)) >= (kl & (blk - 1)))

        def rows_at(start, dil=dil):
            if dil == 1:
                return pl.ds(pl.multiple_of(start, blk), blk)
            return pl.ds(start, blk, stride=dil)

        def block(idx, carry, br=br, dil=dil, nb=nb, has_prev=has_prev, kj=kj, band=band, rows_at=rows_at,
                  first=first, grp=grp):
            valid = band
            if has_prev:
                r = idx // nb
                n = idx % nb
                rqs = [rows_at(r + dil * blk * n)]
                rp = rows_at(r + dil * blk * jnp.maximum(n - 1, 0))
                valid = band & (kj >= jnp.where(n > 0, 0, blk))
            else:
                rqs = [rows_at(idx * grp + g) for g in range(grp)]
            for pp in range(npair):
                qp = jnp.concatenate([qn_ref[pp, rq, :] for rq in rqs], axis=0)
                kp = jnp.concatenate([kn_ref[pp, rq, :] for rq in rqs], axis=0)
                vp = jnp.concatenate([vv_ref[pp, rq, :] for rq in rqs], axis=0)
                if has_prev:
                    kp = jnp.concatenate([kn_ref[pp, rp, :], kp], axis=0)
                    vp = jnp.concatenate([vv_ref[pp, rp, :], vp], axis=0)
                oh, lh = [], []
                for hh in range(2):
                    bias = bias_ref[br, 2 * pp + hh]
                    if not has_prev:
                        row = jnp.concatenate([bias[:, blk:]] * grp, axis=1)
                        bias = jnp.concatenate([row] * grp, axis=0)
                    s = jnp.where(valid, _mm_nt(qp * half[hh], kp) + bias, NEG)
                    m = jnp.max(s, axis=-1, keepdims=True)
                    p = jnp.exp(s - m)
                    l = jnp.sum(p, axis=-1, keepdims=True)
                    oh.append(_mm(p, vp) / l)
                    lh.append(m + jnp.log(l))
                o_all = jnp.where(low, oh[0], oh[1])
                lse_all = jnp.where(low, lh[0], lh[1])
                for g, rq in enumerate(rqs):
                    o = o_all[g * blk:(g + 1) * blk]
                    lse = lse_all[g * blk:(g + 1) * blk]
                    if first:
                        num_ref[pp, rq, :] = o
                        den_ref[pp, rq, :] = jnp.ones_like(o)
                        mx_ref[pp, rq, :] = lse
                    else:
                        m_old = mx_ref[pp, rq, :]
                        m_new = jnp.maximum(m_old, lse)
                        wa = jnp.exp(m_old - m_new)
                        wb = jnp.exp(lse - m_new)
                        num_ref[pp, rq, :] = num_ref[pp, rq, :] * wa + wb * o
                        den_ref[pp, rq, :] = den_ref[pp, rq, :] * wa + wb
                        mx_ref[pp, rq, :] = m_new
            return carry

        lax.fori_loop(0, dil * nb // grp, block, 0, unroll=2 if has_prev else 1)

    def finish(i, carry):
        rows = pl.ds(pl.multiple_of(i * blk, blk), blk)
        for pp in range(npair):
            o_ref[0, rows, pp * pw:(pp + 1) * pw] = num_ref[pp, rows, :] / den_ref[pp, rows, :]
        return carry

    lax.fori_loop(0, S // blk, finish, 0)


def _attn(ua3, qw, kw, bias_all):
    B, S, _ = ua3.shape
    nbr = len(DILATED_PAIRS)
    return pl.pallas_call(
        _attn_kernel,
        grid=(B,),
        in_specs=[pl.BlockSpec((1, S, AT_COLS), lambda b: (b, 0, 0)),
                  pl.BlockSpec((1, ATTN_W), lambda b: (0, 0)),
                  pl.BlockSpec((1, ATTN_W), lambda b: (0, 0)),
                  pl.BlockSpec((nbr, ATTN_HEADS, ATTN_BLOCK, 2 * ATTN_BLOCK), lambda b: (0, 0, 0, 0))],
        out_specs=pl.BlockSpec((1, S, ATTN_W), lambda b: (b, 0, 0)),
        out_shape=jax.ShapeDtypeStruct((B, S, ATTN_W), F32),
        scratch_shapes=[pltpu.VMEM((ATTN_HEADS // 2, S, 2 * HEAD_DIM), F32) for _ in range(6)],
        compiler_params=_cparams(("parallel",)),
        name="dilated_attn",
    )(ua3, qw, kw, bias_all)


def _outproj_kernel(x_ref, mh_ref, ma_ref, ms_ref, wo_ref, fw_ref, wq_ref, sk_ref,
                    h_ref, xn_ref, sc_ref):
    W = ATTN_W
    h = (x_ref[...] + _mm(mh_ref[...], wo_ref[0:HGRN_W, :])
         + _mm(ma_ref[...], wo_ref[HGRN_W:HGRN_W + W, :])
         + _mm(ms_ref[...], wo_ref[HGRN_W + W:, :]))
    h_ref[...] = h
    xn = (h * lax.rsqrt(jnp.mean(h * h, axis=-1, keepdims=True) + EPS) * fw_ref[...]).astype(BF16)
    xn_ref[...] = xn
    qp = jnp.dot(xn, wq_ref[...], preferred_element_type=F32)
    for hd in range(PEER_HEADS):
        qh = qp[:, 2 * PEER_HALF * hd:2 * PEER_HALF * (hd + 1)]
        for c in range(2):
            sc_ref[hd, c] = _mm_nt(sk_ref[c], qh)


def _outproj(x2, mh, ma, ms, w_out, ffn_w, wq, sk_pad, tm=512):
    T, D = x2.shape
    row = lambda n: pl.BlockSpec((tm, n), lambda i: (i, 0))
    full = lambda a: pl.BlockSpec(a.shape, lambda i: (0,) * a.ndim)
    return pl.pallas_call(
        _outproj_kernel,
        grid=(T // tm,),
        in_specs=[row(D), row(HGRN_W), row(ATTN_W), row(SSM_W),
                  full(w_out), pl.BlockSpec((1, D), lambda i: (0, 0)), full(wq), full(sk_pad)],
        out_specs=[row(D), row(D),
                   pl.BlockSpec((PEER_HEADS, 2, PEER_NKEYS, tm), lambda i: (0, 0, 0, i))],
        out_shape=[jax.ShapeDtypeStruct((T, D), F32), jax.ShapeDtypeStruct((T, D), BF16),
                   jax.ShapeDtypeStruct((PEER_HEADS, 2, PEER_NKEYS, T), F32)],
        compiler_params=_cparams(("parallel",)),
        name="outproj_query",
    )(x2, mh, ma, ms, w_out, ffn_w.reshape(1, D), wq, sk_pad)


_NSEL = PEER_TOPK + 1
_PAIRS = [(i, j) for i in range(_NSEL) for j in range(_NSEL) if (i + 1) * (j + 1) <= _NSEL]


def _top_values(v, k):
    out = []
    rank = jnp.full(v.shape, float(k), F32)
    for r in range(k):
        m = jnp.max(v, axis=0, keepdims=True)
        out.append(m)
        hit = v == m
        rank = jnp.where(hit, float(r), rank)
        v = jnp.where(hit, NEG, v)
    return out, rank


def _pack_pair(lo, hi):
    lo_b = lax.bitcast_convert_type(lo.astype(BF16).astype(F32), jnp.uint32) >> 16
    hi_b = lax.bitcast_convert_type(hi.astype(BF16).astype(F32), jnp.uint32) & jnp.uint32(0xFFFF0000)
    return hi_b | lo_b


def _route_kernel(sc_ref, n1_ref, u1_ref, rk2_ref, u2_ref):
    tt = sc_ref.shape[-1]
    row8 = _iota((8, tt), 0)

    def head(hd):
        s1 = sc_ref[hd, 0]
        s2 = sc_ref[hd, 1]
        a, _ = _top_values(s1, _NSEL)
        b, rank2 = _top_values(s2, _NSEL)
        groups = []
        for g0 in range(0, len(_PAIRS), 8):
            cg = jnp.full((8, tt), NEG, F32)
            for p, (i, j) in enumerate(_PAIRS[g0:g0 + 8]):
                cg = jnp.where(row8 == p, a[i] + b[j], cg)
            groups.append(cg)
        best, _ = _top_values(jnp.concatenate(groups, axis=0), _NSEL)
        zsum = jnp.ones((1, tt), F32)
        for r in range(1, PEER_TOPK):
            zsum = zsum + jnp.exp(best[r] - best[0])
        cut = 0.5 * (best[PEER_TOPK - 1] + best[PEER_TOPK])
        need = cut - s1
        n1 = jnp.zeros_like(s1)
        for r in range(PEER_TOPK):
            n1 = jnp.where(b[r] >= need, float(r + 1), n1)
        return n1, jnp.exp(s1 - a[0]) / zsum, rank2, jnp.exp(s2 - b[0])

    def pair(hp, carry):
        lo = head(2 * hp)
        hi = head(2 * hp + 1)
        for ref, x, y in zip((n1_ref, u1_ref, rk2_ref, u2_ref), lo, hi):
            ref[hp] = _pack_pair(x, y)
        return carry

    lax.fori_loop(0, PEER_HEADS // 2, pair, 0)


def _route(sc, tt=256):
    T = sc.shape[-1]
    blk = pl.BlockSpec((PEER_HEADS // 2, PEER_NKEYS, tt), lambda i: (0, 0, i))
    shp = jax.ShapeDtypeStruct((PEER_HEADS // 2, PEER_NKEYS, T), jnp.uint32)
    return pl.pallas_call(
        _route_kernel,
        grid=(T // tt,),
        in_specs=[pl.BlockSpec((PEER_HEADS, 2, PEER_NKEYS, tt), lambda i: (0, 0, 0, i))],
        out_specs=[blk, blk, blk, blk],
        out_shape=[shp, shp, shp, shp],
        compiler_params=_cparams(("parallel",)),
        name="peer_route",
    )(sc)


_GELU_C1 = -2.0 * math.sqrt(2.0 / math.pi)
_GELU_C2 = _GELU_C1 * 0.044715
_PEER_TE = 2048
_PEER_EB = 512
_PEER_LANES = 128
_PEER_ROWS = 32
_PEER_FLAGS = None


def _as_halves(words):
    return pltpu.bitcast(words, BF16)


def _peer_kernel(nj, xn_ref, dn_ref, upt_ref, n1_ref, u1_ref, rk2_ref, u2_ref, h_ref, o_ref,
                 acc_ref, at0_ref, at1_ref, wg_ref):
    j = pl.program_id(1)
    tt = xn_ref.shape[0]
    te = dn_ref.shape[0]
    nb = te // _PEER_EB
    nh = tt // _PEER_TOK
    gpb = _PEER_EB // PEER_NKEYS
    at = (at0_ref, at1_ref)

    @pl.when(j == 0)
    def _():
        acc_ref[...] = jnp.zeros_like(acc_ref)

    def scores(k, at_ref):
        rows = pl.ds(pl.multiple_of(k * _PEER_EB, _PEER_EB), _PEER_EB)
        for hf in range(nh):
            at_ref[hf] = lax.dot_general(dn_ref[rows, :], xn_ref[hf * _PEER_TOK:(hf + 1) * _PEER_TOK, :],
                                         (((1,), (1,)), ((), ())), preferred_element_type=F32)

    def weigh(k, at_ref):
        for g in range(gpb):
            c = (j * nb + k) * gpb + g
            er = slice(g * PEER_NKEYS, (g + 1) * PEER_NKEYS)
            wr = slice(k * _PEER_EB + g * PEER_NKEYS, k * _PEER_EB + (g + 1) * PEER_NKEYS)
            for hf in range(nh):
                tw = slice(hf * _PEER_TOK, (hf + 1) * _PEER_TOK)
                n1s = [n1_ref[hp, pl.ds(c, 1), tw] for hp in range(PEER_HEADS // 2)]
                u1s = [u1_ref[hp, pl.ds(c, 1), tw] for hp in range(PEER_HEADS // 2)]
                for lt in range(_PEER_TOK // _PEER_LANES):
                    ls = slice(lt * _PEER_LANES, (lt + 1) * _PEER_LANES)
                    tl = slice(hf * _PEER_TOK + lt * _PEER_LANES, hf * _PEER_TOK + (lt + 1) * _PEER_LANES)
                    w = None
                    for hp in range(PEER_HEADS // 2):
                        n1 = _as_halves(jnp.broadcast_to(n1s[hp][:, ls], (PEER_NKEYS, _PEER_LANES)))
                        u1 = _as_halves(jnp.broadcast_to(u1s[hp][:, ls], (PEER_NKEYS, _PEER_LANES)))
                        rk2 = _as_halves(rk2_ref[hp, :, tl])
                        u2 = _as_halves(u2_ref[hp, :, tl])
                        t = jnp.where(rk2 < n1, u2, jnp.zeros((), BF16)) * u1
                        w = t if w is None else w + t
                    wu = pltpu.bitcast(w, jnp.uint32)
                    wf = (lax.bitcast_convert_type(wu << 16, F32)
                          + lax.bitcast_convert_type(wu & jnp.uint32(0xFFFF0000), F32))
                    a = at_ref[hf, er, ls]
                    gl = a / (1.0 + jnp.exp(a * (_GELU_C1 + _GELU_C2 * (a * a))))
                    wg_ref[hf, wr, ls] = (wf * gl).astype(BF16)

    scores(0, at[0])
    for k in range(nb):
        if k + 1 < nb:
            scores(k + 1, at[(k + 1) % 2])
        weigh(k, at[k % 2])
    for hf in range(nh):
        tw = slice(hf * _PEER_TOK, (hf + 1) * _PEER_TOK)
        acc_ref[:, tw] += jnp.dot(upt_ref[0], wg_ref[hf], preferred_element_type=F32)

    @pl.when(j == nj - 1)
    def _():
        o_ref[...] = h_ref[...] + acc_ref[...].T


def _peer(xn, down_b, layer, up_t3, n1, u1, rk2, u2, h, tt=512, te=_PEER_TE):
    T, D = xn.shape
    E = down_b.shape[1]
    nj = E // te
    nh = tt // _PEER_TOK
    rt = pl.BlockSpec((PEER_HEADS // 2, PEER_NKEYS, tt), lambda i, j: (0, 0, i))
    return pl.pallas_call(
        functools.partial(_peer_kernel, nj),
        grid=(T // tt, nj),
        in_specs=[pl.BlockSpec((tt, D), lambda i, j: (i, 0)),
                  pl.BlockSpec((None, te, D), lambda i, j: (layer, j, 0)),
                  pl.BlockSpec((1, D, te), lambda i, j: (j, 0, 0)),
                  rt, rt, rt, rt,
                  pl.BlockSpec((tt, D), lambda i, j: (i, 0))],
        out_specs=pl.BlockSpec((tt, D), lambda i, j: (i, 0)),
        out_shape=jax.ShapeDtypeStruct((T, D), F32),
        scratch_shapes=[pltpu.VMEM((D, tt), F32),
                        pltpu.VMEM((nh, _PEER_EB, _PEER_TOK), F32), pltpu.VMEM((nh, _PEER_EB, _PEER_TOK), F32),
                        pltpu.VMEM((nh, te, _PEER_TOK), BF16)],
        compiler_params=_cparams(("parallel", "arbitrary"), flags=_PEER_FLAGS),
        name="peer_experts",
    )(xn, down_b, up_t3, n1, u1, rk2, u2, h)


def _transpose_cast_kernel(x_ref, o_ref):
    o_ref[0] = x_ref[...].T.astype(o_ref.dtype)


def _transpose_blocks_bf16(x, layer, tr=_PEER_EB):
    _, R, C = x.shape
    return pl.pallas_call(
        _transpose_cast_kernel,
        grid=(R // tr,),
        in_specs=[pl.BlockSpec((None, tr, C), lambda i: (layer, i, 0))],
        out_specs=pl.BlockSpec((1, C, tr), lambda i: (i, 0, 0)),
        out_shape=jax.ShapeDtypeStruct((R // tr, C, tr), BF16),
        compiler_params=_cparams(("parallel",)),
        name="transpose_cast",
    )(x)


def _rep(v, n):
    return jnp.repeat(v, n, axis=-1)


def _layer(x2, B, S, layer, p, bias_all):
    T, D = x2.shape
    uh, ua, us = _inproj(x2, p["attn_norm_w"], p["w_in_pad"], layer)

    mh = _hgrn(uh.reshape(B, S, HG_COLS), p["hgrn_lower_bounds"],
               jnp.tile(p["hgrn_norm_w"], HGRN_HEADS).reshape(1, HGRN_W), layer)
    qw = jnp.tile(p["q_norm_w"], ATTN_HEADS).reshape(1, ATTN_W)
    kw = jnp.tile(p["k_norm_w"], ATTN_HEADS).reshape(1, ATTN_W)
    ma = _attn(ua.reshape(B, S, AT_COLS), qw, kw, bias_all)
    ms = _ssd(us.reshape(B, S, SS_COLS), p["ssm_conv_w"], p["ssm_conv_b"].reshape(1, -1),
              _rep(p["ssm_dt_bias"], HEAD_DIM).reshape(1, -1), _rep(p["ssm_a_log"], HEAD_DIM).reshape(1, -1),
              _rep(p["ssm_d"], HEAD_DIM).reshape(1, -1), p["ssm_norm_w"].reshape(1, -1))

    sk = p["peer_sub_keys"]
    zk = jnp.zeros_like(sk[0])
    sk_pad = jnp.stack([jnp.concatenate([sk[0], zk], axis=1), jnp.concatenate([zk, sk[1]], axis=1)])
    h, xn, sc = _outproj(x2, mh.reshape(T, HGRN_W), ma.reshape(T, ATTN_W), ms.reshape(T, SSM_W),
                         p["w_out"].astype(BF16), p["ffn_norm_w"], p["peer_w_query"].astype(BF16), sk_pad)
    n1, u1, rk2, u2 = _route(sc)
    return _peer(xn, p["peer_down_b"], layer, _transpose_blocks_bf16(p["peer_up"], layer, tr=_PEER_TE), n1, u1, rk2, u2, h)


_PER_LAYER = ("attn_norm_w", "hgrn_norm_w", "q_norm_w", "k_norm_w", "ssm_conv_w", "ssm_conv_b",
              "ssm_dt_bias", "ssm_a_log", "ssm_d", "ssm_norm_w", "w_out", "ffn_norm_w", "peer_w_query",
              "peer_sub_keys")


def kernel(x, attn_norm_w, w_in, hgrn_lower_bounds, hgrn_norm_w, q_norm_w, k_norm_w, rel_bias, ssm_conv_w, ssm_conv_b, ssm_dt_bias, ssm_a_log, ssm_d, ssm_norm_w, w_out, ffn_norm_w, peer_w_query, peer_sub_keys, peer_down, peer_up):
    stacked = dict(attn_norm_w=attn_norm_w, w_in=w_in, hgrn_norm_w=hgrn_norm_w, q_norm_w=q_norm_w,
                   k_norm_w=k_norm_w, ssm_conv_w=ssm_conv_w, ssm_conv_b=ssm_conv_b, ssm_dt_bias=ssm_dt_bias,
                   ssm_a_log=ssm_a_log, ssm_d=ssm_d, ssm_norm_w=ssm_norm_w, w_out=w_out, ffn_norm_w=ffn_norm_w,
                   peer_w_query=peer_w_query, peer_sub_keys=peer_sub_keys, peer_down=peer_down, peer_up=peer_up)
    B, S, D = x.shape
    bias_all = _bias_tables(rel_bias)
    down_b = peer_down.astype(BF16)
    lane_pad = -w_in.shape[2] % 128
    w_in_pad = jnp.pad(w_in, ((0, 0), (0, 0), (0, lane_pad))).astype(BF16)
    x2 = x.reshape(B * S, D)
    for layer in range(w_in.shape[0]):
        p = {k: stacked[k][layer] for k in _PER_LAYER}
        p["w_in_pad"] = w_in_pad
        p["hgrn_lower_bounds"] = hgrn_lower_bounds
        p["peer_down_b"] = down_b
        p["peer_up"] = peer_up
        x2 = _layer(x2, B, S, layer, p, bias_all)
    return x2.reshape(B, S, D)
```
